```python
import math
import jax, jax.numpy as jnp
from jax import lax
import numpy as np

D_MODEL = 1024
BATCH = 16
SEQ = 4096
DEPTH = 4
DEC_BATCH = 2
DEC_SEQ = 8192
PAST_LEN = 128

N_RET_HEADS = 4
RET_QK_DIM = 128
RET_V_DIM = 192
RET_QK_WIDTH = N_RET_HEADS * RET_QK_DIM
RET_V_WIDTH = N_RET_HEADS * RET_V_DIM
RET_CHUNK = 128
ROPE_BASE = 10000.0
N_FOURIER_GROUPS = 4
FOURIER_GROUP_DIM = 64
FOURIER_WIDTH = N_FOURIER_GROUPS * FOURIER_GROUP_DIM
IN_WIDTH = 2 * RET_QK_WIDTH + 2 * RET_V_WIDTH + FOURIER_WIDTH + 2 * D_MODEL
SPLIT_POINTS = (
    RET_QK_WIDTH,
    2 * RET_QK_WIDTH,
    2 * RET_QK_WIDTH + RET_V_WIDTH,
    2 * RET_QK_WIDTH + 2 * RET_V_WIDTH,
    2 * RET_QK_WIDTH + 2 * RET_V_WIDTH + FOURIER_WIDTH,
    2 * RET_QK_WIDTH + 2 * RET_V_WIDTH + FOURIER_WIDTH + D_MODEL,
)
N_EXPERT_GROUPS = 4
EXPERTS_PER_GROUP = 8
N_EXPERTS = N_EXPERT_GROUPS * EXPERTS_PER_GROUP
EXPERT_TOP_K = 2
D_EXPERT = 512
DEEPNORM_ALPHA = (2 * DEPTH) ** 0.25
DEEPNORM_BETA = (8 * DEPTH) ** -0.25
LN_EPS = 1e-5

kernel_name = "hybrid_retention_fnet_hiermoe_encoder"


def layer_norm(x, g, b):
    xf = x.astype(jnp.float32)
    mu = jnp.mean(xf, axis=-1, keepdims=True)
    var = jnp.mean(jnp.square(xf - mu), axis=-1, keepdims=True)
    y = (xf - mu) * lax.rsqrt(var + LN_EPS) * g.astype(jnp.float32) + b.astype(jnp.float32)
    return y.astype(x.dtype)


def rotary(x):
    seq_len = x.shape[1]
    half = x.shape[-1] // 2
    inv_freq = 1.0 / (ROPE_BASE ** (jnp.arange(half, dtype=jnp.float32) / half))
    ang = jnp.arange(seq_len, dtype=jnp.float32)[:, None] * inv_freq[None, :]
    cos = jnp.cos(ang)[None, :, None, :]
    sin = jnp.sin(ang)[None, :, None, :]
    x1, x2 = x[..., :half], x[..., half:]
    return jnp.concatenate([x1 * cos - x2 * sin, x2 * cos + x1 * sin], axis=-1)


def retention_chunkwise(q, k, v, log_gamma, strict):
    B, H, S, dk = q.shape
    dv = v.shape[-1]
    C = RET_CHUNK
    N = S // C
    pos = jnp.arange(C, dtype=jnp.float32)
    diff = pos[:, None] - pos[None, :]
    mask = (diff > 0) if strict else (diff >= 0)
    decay = jnp.where(mask[None], jnp.exp(log_gamma[:, None, None] * jnp.where(mask, diff, 0.0)[None]), 0.0)
    q_decay = jnp.exp(log_gamma[:, None] * (pos + 1.0)[None, :])
    k_decay = jnp.exp(log_gamma[:, None] * (C - 1.0 - pos)[None, :])
    chunk_decay = jnp.exp(log_gamma * C)
    qc = q.reshape(B, H, N, C, dk)
    kc = k.reshape(B, H, N, C, dk)
    vc = v.reshape(B, H, N, C, dv)
    scores = jnp.einsum('bhncd,bhnmd->bhncm', qc, kc) * decay[None, :, None]
    inner = jnp.einsum('bhncm,bhnme->bhnce', scores, vc)
    kv = jnp.einsum('bhncd,bhnce->nbhde', kc * k_decay[None, :, None, :, None], vc)

    def step(state, kv_n):
        return chunk_decay[None, :, None, None] * state + kv_n, state

    init = jnp.zeros((B, H, dk, dv), jnp.float32)
    _, prev_states = lax.scan(step, init, kv)
    cross = jnp.einsum('bhncd,nbhde->bhnce', qc * q_decay[None, :, None, :, None], prev_states)
    return (inner + cross).reshape(B, H, S, dv)


def retention_branch(q, k, v, gate_swish, decay_fwd_logit, decay_bwd_logit, gn_g, gn_b):
    B, S, _ = q.shape
    dt = q.dtype
    qh = rotary(q.astype(jnp.float32).reshape(B, S, N_RET_HEADS, RET_QK_DIM)) * (RET_QK_DIM ** -0.5)
    kh = rotary(k.astype(jnp.float32).reshape(B, S, N_RET_HEADS, RET_QK_DIM))
    vh = v.astype(jnp.float32).reshape(B, S, N_RET_HEADS, RET_V_DIM)
    qh, kh, vh = (a.transpose(0, 2, 1, 3) for a in (qh, kh, vh))
    log_g_f = jax.nn.log_sigmoid(decay_fwd_logit.astype(jnp.float32))
    log_g_b = jax.nn.log_sigmoid(decay_bwd_logit.astype(jnp.float32))
    fwd = retention_chunkwise(qh, kh, vh, log_g_f, False)
    bwd = jnp.flip(retention_chunkwise(jnp.flip(qh, 2), jnp.flip(kh, 2), jnp.flip(vh, 2), log_g_b, True), 2)
    o = (fwd + bwd).transpose(0, 2, 1, 3)
    mu = jnp.mean(o, axis=-1, keepdims=True)
    var = jnp.mean(jnp.square(o - mu), axis=-1, keepdims=True)
    o = ((o - mu) * lax.rsqrt(var + LN_EPS)).reshape(B, S, RET_V_WIDTH)
    o = o * gn_g.astype(jnp.float32) + gn_b.astype(jnp.float32)
    return (o * jax.nn.silu(gate_swish.astype(jnp.float32))).astype(dt)


def fourier_branch(f):
    B, S, _ = f.shape
    fg = f.astype(jnp.float32).reshape(B, S, N_FOURIER_GROUPS, FOURIER_GROUP_DIM)
    mixed = jnp.real(jnp.fft.fft2(fg, axes=(1, 3), norm='ortho'))
    return mixed.reshape(B, S, FOURIER_WIDTH).astype(f.dtype)


def token_mixer(x, w_in, decay_fwd, decay_bwd, gn_g, gn_b, w_ret_up, w_four_up, w_out):
    proj = jnp.einsum('bsd,de->bse', x, w_in)
    q, k, v, g_sw, f_in, gate_r, gate_f = jnp.split(proj, SPLIT_POINTS, axis=-1)
    r = retention_branch(q, k, v, g_sw, decay_fwd, decay_bwd, gn_g, gn_b)
    fo = fourier_branch(f_in)
    merged = (jax.nn.sigmoid(gate_r) * jnp.einsum('bsv,vd->bsd', r, w_ret_up)
              + jax.nn.sigmoid(gate_f) * jnp.einsum('bsf,fd->bsd', fo, w_four_up))
    return jnp.einsum('bsd,de->bse', merged, w_out)


def hier_moe(x, w_route_group, b_route_group, w_route_expert, b_route_expert, w_gate, w_up, w_down):
    B, S, D = x.shape
    t = x.reshape(B * S, D)
    T = t.shape[0]
    g_logits = (t @ w_route_group + b_route_group).astype(jnp.float32)
    g_probs = jax.nn.softmax(g_logits, axis=-1)
    _, g_idx = lax.top_k(g_logits, 1)
    g_onehot = jax.nn.one_hot(g_idx[:, 0], N_EXPERT_GROUPS, dtype=jnp.float32)
    g_gate = jnp.sum(g_probs * g_onehot, axis=-1)
    e_logits = (t @ w_route_expert + b_route_expert).astype(jnp.float32).reshape(T, N_EXPERT_GROUPS, EXPERTS_PER_GROUP)
    e_sel = jnp.einsum('tge,tg->te', e_logits, g_onehot)
    top_v, top_i = lax.top_k(e_sel, EXPERT_TOP_K)
    top_w = jax.nn.softmax(top_v, axis=-1)
    within = jnp.sum(jax.nn.one_hot(top_i, EXPERTS_PER_GROUP, dtype=jnp.float32) * top_w[..., None], axis=1)
    comb = ((g_onehot * g_gate[:, None])[:, :, None] * within[:, None, :]).reshape(T, N_EXPERTS).astype(x.dtype)
    out = jnp.zeros_like(t)
    for e in range(N_EXPERTS):
        h = jax.nn.silu(t @ w_gate[e]) * (t @ w_up[e])
        out = out + (comb[:, e:e + 1] * h) @ w_down[e]
    return out.reshape(B, S, D)


def trunk(x, ln_in_g, ln_in_b, w_in, ret_decay_fwd, ret_decay_bwd, ret_gn_g, ret_gn_b,
          w_ret_up, w_four_up, w_out, ln1_g, ln1_b, w_route_group, b_route_group,
          w_route_expert, b_route_expert, w_expert_gate, w_expert_up, w_expert_down, ln2_g, ln2_b):
    x = layer_norm(x, ln_in_g, ln_in_b)
    for l in range(DEPTH):
        mix = token_mixer(x, w_in[l], ret_decay_fwd[l], ret_decay_bwd[l], ret_gn_g[l], ret_gn_b[l],
                          w_ret_up[l], w_four_up[l], w_out[l])
        x = layer_norm(DEEPNORM_ALPHA * x + mix, ln1_g[l], ln1_b[l])
        ff = hier_moe(x, w_route_group[l], b_route_group[l], w_route_expert[l], b_route_expert[l],
                      w_expert_gate[l], w_expert_up[l], w_expert_down[l])
        x = layer_norm(DEEPNORM_ALPHA * x + ff, ln2_g[l], ln2_b[l])
    return x


def setup_inputs(seed: int = 0) -> dict:
    key = jax.random.key(seed)
    ks = jax.random.split(key, 24)
    f32 = jnp.float32
    nrm = lambda k, shape, scale: jax.random.normal(k, shape, f32) * scale
    decay_base = jnp.log(2.0 ** (5.0 + jnp.arange(N_RET_HEADS, dtype=f32)) - 1.0)
    return {
        'x_prompt': nrm(ks[0], (BATCH, SEQ, D_MODEL), 1.0),
        'x_sample': nrm(ks[1], (DEC_BATCH, DEC_SEQ, D_MODEL), 1.0),
        'ln_in_g': 1.0 + nrm(ks[2], (D_MODEL,), 0.02),
        'ln_in_b': nrm(ks[3], (D_MODEL,), 0.02),
        'w_in': nrm(ks[4], (DEPTH, D_MODEL, IN_WIDTH), D_MODEL ** -0.5),
        'ret_decay_fwd': decay_base[None, :] + nrm(ks[5], (DEPTH, N_RET_HEADS), 0.1),
        'ret_decay_bwd': decay_base[None, :] + nrm(ks[6], (DEPTH, N_RET_HEADS), 0.1),
        'ret_gn_g': 1.0 + nrm(ks[7], (DEPTH, RET_V_WIDTH), 0.02),
        'ret_gn_b': nrm(ks[8], (DEPTH, RET_V_WIDTH), 0.02),
        'w_ret_up': nrm(ks[9], (DEPTH, RET_V_WIDTH, D_MODEL), RET_V_WIDTH ** -0.5),
        'w_four_up': nrm(ks[10], (DEPTH, FOURIER_WIDTH, D_MODEL), FOURIER_WIDTH ** -0.5),
        'w_out': nrm(ks[11], (DEPTH, D_MODEL, D_MODEL), DEEPNORM_BETA * D_MODEL ** -0.5),
        'ln1_g': 1.0 + nrm(ks[12], (DEPTH, D_MODEL), 0.02),
        'ln1_b': nrm(ks[13], (DEPTH, D_MODEL), 0.02),
        'w_route_group': nrm(ks[14], (DEPTH, D_MODEL, N_EXPERT_GROUPS), D_MODEL ** -0.5),
        'b_route_group': nrm(ks[15], (DEPTH, N_EXPERT_GROUPS), 0.01),
        'w_route_expert': nrm(ks[16], (DEPTH, D_MODEL, N_EXPERTS), D_MODEL ** -0.5),
        'b_route_expert': nrm(ks[17], (DEPTH, N_EXPERTS), 0.01),
        'w_expert_gate': nrm(ks[18], (DEPTH, N_EXPERTS, D_MODEL, D_EXPERT), D_MODEL ** -0.5),
        'w_expert_up': nrm(ks[19], (DEPTH, N_EXPERTS, D_MODEL, D_EXPERT), D_MODEL ** -0.5),
        'w_expert_down': nrm(ks[20], (DEPTH, N_EXPERTS, D_EXPERT, D_MODEL), DEEPNORM_BETA * D_EXPERT ** -0.5),
        'ln2_g': 1.0 + nrm(ks[21], (DEPTH, D_MODEL), 0.02),
        'ln2_b': nrm(ks[22], (DEPTH, D_MODEL), 0.02),
    }


def reference(x_prompt, x_sample, ln_in_g, ln_in_b, w_in, ret_decay_fwd, ret_decay_bwd, ret_gn_g, ret_gn_b,
              w_ret_up, w_four_up, w_out, ln1_g, ln1_b, w_route_group, b_route_group,
              w_route_expert, b_route_expert, w_expert_gate, w_expert_up, w_expert_down, ln2_g, ln2_b):
    y_prompt = trunk(x_prompt, ln_in_g, ln_in_b, w_in, ret_decay_fwd, ret_decay_bwd, ret_gn_g, ret_gn_b,
                     w_ret_up, w_four_up, w_out, ln1_g, ln1_b, w_route_group, b_route_group,
                     w_route_expert, b_route_expert, w_expert_gate, w_expert_up, w_expert_down, ln2_g, ln2_b)
    y_sample = trunk(x_sample, ln_in_g, ln_in_b, w_in, ret_decay_fwd, ret_decay_bwd, ret_gn_g, ret_gn_b,
                     w_ret_up, w_four_up, w_out, ln1_g, ln1_b, w_route_group, b_route_group,
                     w_route_expert, b_route_expert, w_expert_gate, w_expert_up, w_expert_down, ln2_g, ln2_b)
    return (y_prompt, y_sample)
```

```python
import functools
import math

import jax
import jax.numpy as jnp
import numpy as np
from jax import lax
from jax.experimental import pallas as pl
from jax.experimental.pallas import tpu as pltpu

N_HEADS = 4
QK_DIM = 128
V_DIM = 192
V_PAD = 256
CHUNK = 128
ROPE_BASE = 10000.0
N_FGROUPS = 4
FG_DIM = 64
F_WIDTH = N_FGROUPS * FG_DIM
N_GROUPS = 4
EXPERTS_PER_GROUP = 8
N_EXPERTS = N_GROUPS * EXPERTS_PER_GROUP
LN_EPS = 1e-5
ROUTE_LANES = 128

LANE = 128
VMEM_LIMIT_BYTES = 56 * 1024 * 1024
MOE_TILE = 256
GATHER_TILE = 256
TOKEN_TILE = 256

F32 = jnp.float32
BF16 = jnp.bfloat16


def _cparams(sem):
    return pltpu.CompilerParams(dimension_semantics=sem, vmem_limit_bytes=VMEM_LIMIT_BYTES)


def _layer_norm_rows(y, g, b):
    mu = jnp.mean(y, axis=-1, keepdims=True)
    d = y - mu
    var = jnp.mean(d * d, axis=-1, keepdims=True)
    return d * lax.rsqrt(var + LN_EPS) * g + b


def _pack_halves(y):
    n = y.shape[1] // 2
    lo = pltpu.bitcast(y[:, :n].astype(BF16).astype(F32), jnp.uint32)
    hi = pltpu.bitcast(y[:, n:].astype(BF16).astype(F32), jnp.uint32)
    return (hi & jnp.uint32(0xFFFF0000)) | (lo >> 16)


def _unpack_halves(w):
    lo = pltpu.bitcast(w << 16, F32)
    hi = pltpu.bitcast(w & jnp.uint32(0xFFFF0000), F32)
    return lo, hi


def _ln_kernel(x_ref, g_ref, b_ref, o_ref):
    o_ref[...] = _layer_norm_rows(x_ref[...], g_ref[...], b_ref[...])


def _layer_norm(x, g, b, tm):
    t, d = x.shape
    return pl.pallas_call(
        _ln_kernel,
        out_shape=jax.ShapeDtypeStruct((t, d), F32),
        grid=(t // tm,),
        in_specs=[pl.BlockSpec((tm, d), lambda i: (i, 0)),
                  pl.BlockSpec((1, d), lambda i: (0, 0)),
                  pl.BlockSpec((1, d), lambda i: (0, 0))],
        out_specs=pl.BlockSpec((tm, d), lambda i: (i, 0)),
        compiler_params=_cparams(("parallel",)),
        name="ln_in",
    )(x, g.reshape(1, d), b.reshape(1, d))


def _inproj_kernel(x_ref, w_ref, cos_ref, sin_ref, bd_ref,
                   qk_ref, v_ref, sg_ref, z_ref, gates_ref, *, d_model):
    x = x_ref[...].astype(BF16)
    cos = cos_ref[...]
    sin = sin_ref[...]
    qk_w = 2 * N_HEADS * QK_DIM
    v_w = N_HEADS * V_PAD
    off_v = qk_w
    off_g = off_v + v_w
    off_f = off_g + v_w
    off_gr = off_f + F_WIDTH
    cw = 256

    def proj(c0):
        return jnp.dot(x, w_ref[:, c0:c0 + cw], preferred_element_type=F32)

    for c0 in range(0, qk_w, cw):
        acc = proj(c0)
        scale = QK_DIM ** -0.5 if c0 < N_HEADS * QK_DIM else 1.0
        parts = []
        for hh in range(cw // QK_DIM):
            a = acc[:, hh * QK_DIM:(hh + 1) * QK_DIM]
            rot = a * cos + pltpu.roll(a, QK_DIM // 2, axis=1) * sin
            if scale != 1.0:
                rot = rot * scale
            parts.append(rot.astype(BF16))
        qk_ref[:, c0:c0 + cw] = jnp.concatenate(parts, axis=1)
    for c0 in range(0, v_w, cw):
        v_ref[:, c0:c0 + cw] = proj(off_v + c0).astype(BF16)
    for c0 in range(0, v_w, cw):
        a = proj(off_g + c0)
        sg_ref[:, c0:c0 + cw] = (a * jax.nn.sigmoid(a)).astype(BF16)
    f = proj(off_f).astype(BF16)
    z_ref[...] = jnp.dot(f, bd_ref[...], preferred_element_type=F32).astype(BF16)
    for c0 in range(0, 2 * d_model, cw):
        gates_ref[:, c0:c0 + cw] = jax.nn.sigmoid(proj(off_gr + c0)).astype(BF16)


def _inproj(x, w_pad, cos_t, sin_t, bd, tm, n_p_tiles, sp_tiles, ss_tiles):
    t, d = x.shape
    n_cols = w_pad.shape[1]

    def pos_map(i):
        return (jnp.where(i < n_p_tiles, i % sp_tiles, (i - n_p_tiles) % ss_tiles), 0)

    outs = (jax.ShapeDtypeStruct((t, 2 * N_HEADS * QK_DIM), BF16),
            jax.ShapeDtypeStruct((t, N_HEADS * V_PAD), BF16),
            jax.ShapeDtypeStruct((t, N_HEADS * V_PAD), BF16),
            jax.ShapeDtypeStruct((t, 2 * F_WIDTH), BF16),
            jax.ShapeDtypeStruct((t, 2 * d), BF16))
    row = lambda w: pl.BlockSpec((tm, w), lambda i: (i, 0))
    return pl.pallas_call(
        functools.partial(_inproj_kernel, d_model=d),
        out_shape=outs,
        grid=(t // tm,),
        in_specs=[row(d),
                  pl.BlockSpec((d, n_cols), lambda i: (0, 0)),
                  pl.BlockSpec((tm, QK_DIM), pos_map),
                  pl.BlockSpec((tm, QK_DIM), pos_map),
                  pl.BlockSpec((F_WIDTH, 2 * F_WIDTH), lambda i: (0, 0))],
        out_specs=tuple(row(o.shape[1]) for o in outs),
        compiler_params=_cparams(("parallel",)),
        name="inproj",
    )(x, w_pad, cos_t, sin_t, bd)


def _log_sigmoid(x):
    return jnp.minimum(x, 0.0) - jnp.log1p(jnp.exp(-jnp.abs(x)))


def _retention_kernel(dec_ref, q_ref, k_ref, v_ref, sg_ref, gng_ref, gnb_ref, o_ref,
                      rb_all, rf_s, rb_s, *, n_chunks):
    h = pl.program_id(1)
    c = CHUNK
    lf = _log_sigmoid(jnp.full((1, c), dec_ref[0, h], F32))
    lb = _log_sigmoid(jnp.full((1, c), dec_ref[1, h], F32))
    lf_w = _log_sigmoid(jnp.full((1, V_PAD), dec_ref[0, h], F32))
    lb_w = _log_sigmoid(jnp.full((1, V_PAD), dec_ref[1, h], F32))
    row = lax.broadcasted_iota(jnp.int32, (c, c), 0).astype(F32)
    col = lax.broadcasted_iota(jnp.int32, (c, c), 1).astype(F32)
    diff = row - col
    dmat = jnp.where(diff >= 0.0, jnp.exp(lf * jnp.maximum(diff, 0.0)),
                     jnp.exp(lb * jnp.maximum(-diff, 0.0)))
    qd_f = jnp.exp(lf * (row + 1.0))
    kd_f = jnp.exp(lf * (c - 1.0 - row))
    qd_b = jnp.exp(lb * (c - row))
    kd_b = jnp.exp(lb * row)
    cf = jnp.exp(lf_w * float(c))
    cb = jnp.exp(lb_w * float(c))
    lane = lax.broadcasted_iota(jnp.int32, (c, V_PAD), 1)
    vmask = (lane < V_DIM).astype(F32)
    gn_g = gng_ref[...]
    gn_b = gnb_ref[...]
    tdot = (((0,), (0,)), ((), ()))

    rf_s[...] = jnp.zeros_like(rf_s)
    rb_s[...] = jnp.zeros_like(rb_s)

    def bwd_step(t, carry):
        ci = n_chunks - 1 - t
        r0 = pl.multiple_of(ci * c, c)
        rb = rb_s[...]
        rb_all[ci] = rb.astype(BF16)
        kk = (k_ref[pl.ds(r0, c), :].astype(F32) * kd_b).astype(BF16)
        vv = v_ref[pl.ds(r0, c), :]
        rb_s[...] = cb * rb + lax.dot_general(kk, vv, tdot, preferred_element_type=F32)
        return carry

    lax.fori_loop(0, n_chunks, bwd_step, 0)

    def fwd_step(ci, carry):
        r0 = pl.multiple_of(ci * c, c)
        q = q_ref[pl.ds(r0, c), :]
        k = k_ref[pl.ds(r0, c), :]
        vv = v_ref[pl.ds(r0, c), :]
        qf = q.astype(F32)
        s = lax.dot_general(q, k, (((1,), (1,)), ((), ())), preferred_element_type=F32)
        p = (s * dmat).astype(BF16)
        rf = rf_s[...]
        o = jnp.dot(p, vv, preferred_element_type=F32)
        o = o + jnp.dot((qf * qd_f).astype(BF16), rf.astype(BF16), preferred_element_type=F32)
        o = o + jnp.dot((qf * qd_b).astype(BF16), rb_all[ci], preferred_element_type=F32)
        kk = (k.astype(F32) * kd_f).astype(BF16)
        rf_s[...] = cf * rf + lax.dot_general(kk, vv, tdot, preferred_element_type=F32)
        mu = jnp.sum(o, axis=-1, keepdims=True) * (1.0 / V_DIM)
        d = (o - mu) * vmask
        var = jnp.sum(d * d, axis=-1, keepdims=True) * (1.0 / V_DIM)
        y = d * lax.rsqrt(var + LN_EPS) * gn_g + gn_b
        o_ref[pl.ds(r0, c), :] = (y * sg_ref[pl.ds(r0, c), :].astype(F32)).astype(BF16)
        return carry

    lax.fori_loop(0, n_chunks, fwd_step, 0)


def _retention(dec, qk, v, sg, gn_g, gn_b, n_batch, seq, row_block0):
    t = qk.shape[0]
    n_chunks = seq // CHUNK
    rb = lambda b: b + row_block0
    return pl.pallas_call(
        functools.partial(_retention_kernel, n_chunks=n_chunks),
        out_shape=jax.ShapeDtypeStruct((n_batch * seq, N_HEADS * V_PAD), BF16),
        grid_spec=pltpu.PrefetchScalarGridSpec(
            num_scalar_prefetch=0,
            grid=(n_batch, N_HEADS),
            in_specs=[pl.BlockSpec(memory_space=pltpu.SMEM),
                      pl.BlockSpec((seq, QK_DIM), lambda b, h: (rb(b), h)),
                      pl.BlockSpec((seq, QK_DIM), lambda b, h: (rb(b), N_HEADS + h)),
                      pl.BlockSpec((seq, V_PAD), lambda b, h: (rb(b), h)),
                      pl.BlockSpec((seq, V_PAD), lambda b, h: (rb(b), h)),
                      pl.BlockSpec((1, V_PAD), lambda b, h: (0, h)),
                      pl.BlockSpec((1, V_PAD), lambda b, h: (0, h))],
            out_specs=pl.BlockSpec((seq, V_PAD), lambda b, h: (b, h)),
            scratch_shapes=[pltpu.VMEM((n_chunks, CHUNK, V_PAD), BF16),
                            pltpu.VMEM((CHUNK, V_PAD), F32),
                            pltpu.VMEM((CHUNK, V_PAD), F32)]),
        compiler_params=_cparams(("parallel", "arbitrary")),
        name="retention",
    )(dec, qk, qk, v, sg, gn_g, gn_b)


def _seqdft_kernel(c_ref, s_ref, zc_ref, zs_ref, o_ref, acc_ref):
    kk = pl.program_id(2)

    @pl.when(kk == 0)
    def _():
        acc_ref[...] = jnp.zeros_like(acc_ref)

    acc_ref[...] += (jnp.dot(c_ref[...], zc_ref[...], preferred_element_type=F32)
                     - jnp.dot(s_ref[...], zs_ref[...], preferred_element_type=F32))

    @pl.when(kk == pl.num_programs(2) - 1)
    def _():
        o_ref[...] = acc_ref[...].astype(o_ref.dtype)


def _seq_dft(cmat, smat, z):
    s = cmat.shape[0]
    n = z.shape[1] // 2
    tm = min(1024, s)
    tn = min(1024, n)
    tk = min(512, s)
    nj = n // tn
    return pl.pallas_call(
        _seqdft_kernel,
        out_shape=jax.ShapeDtypeStruct((s, n), BF16),
        grid=(s // tm, nj, s // tk),
        in_specs=[pl.BlockSpec((tm, tk), lambda i, j, k: (i, k)),
                  pl.BlockSpec((tm, tk), lambda i, j, k: (i, k)),
                  pl.BlockSpec((tk, tn), lambda i, j, k: (k, j)),
                  pl.BlockSpec((tk, tn), lambda i, j, k: (k, nj + j))],
        out_specs=pl.BlockSpec((tm, tn), lambda i, j, k: (i, j)),
        scratch_shapes=[pltpu.VMEM((tm, tn), F32)],
        compiler_params=_cparams(("parallel", "parallel", "arbitrary")),
        name="seq_dft",
    )(cmat, smat, z, z)


def _merge_kernel(r_ref, fo_ref, gates_ref, x_ref, wr_ref, wf_ref, wo_ref, g1_ref, b1_ref,
                  wrh_ref, wrl_ref, rb_ref, x1_ref, x1p_ref, route_ref, *, alpha, d_model):
    a = jnp.dot(r_ref[...], wr_ref[...], preferred_element_type=F32)
    b = jnp.dot(fo_ref[...], wf_ref[...], preferred_element_type=F32)
    merged = (gates_ref[:, :d_model].astype(F32) * a + gates_ref[:, d_model:].astype(F32) * b)
    mix = jnp.dot(merged.astype(BF16), wo_ref[...], preferred_element_type=F32)
    x1 = _layer_norm_rows(alpha * x_ref[...] + mix, g1_ref[...], b1_ref[...])
    x1_ref[...] = x1
    x1p_ref[...] = _pack_halves(x1)

    xh = x1.astype(BF16)
    xl = (x1 - xh.astype(F32)).astype(BF16)
    logits = (jnp.dot(xh, wrh_ref[...], preferred_element_type=F32)
              + jnp.dot(xh, wrl_ref[...], preferred_element_type=F32)
              + jnp.dot(xl, wrh_ref[...], preferred_element_type=F32)) + rb_ref[...]
    tm = logits.shape[0]
    lane = lax.broadcasted_iota(jnp.int32, (tm, ROUTE_LANES), 1)
    neg = jnp.float32(-jnp.inf)
    big = jnp.int32(ROUTE_LANES)

    def top1(mask):
        vals = jnp.where(mask, logits, neg)
        m = jnp.max(vals, axis=-1, keepdims=True)
        idx = jnp.min(jnp.where(mask & (vals == m), lane, big), axis=-1, keepdims=True)
        return m, idx

    gmask = lane < N_GROUPS
    gm, gi = top1(gmask)
    gsum = jnp.sum(jnp.where(gmask, jnp.exp(logits - gm), 0.0), axis=-1, keepdims=True)
    g_gate = 1.0 / gsum
    lo = N_GROUPS + gi * EXPERTS_PER_GROUP
    emask = (lane >= lo) & (lane < lo + EXPERTS_PER_GROUP)
    m1, i1 = top1(emask)
    m2, i2 = top1(emask & (lane != i1))
    e2 = jnp.exp(m2 - m1)
    w1 = 1.0 / (1.0 + e2)
    w2 = e2 / (1.0 + e2)
    rec = jnp.where(lane == 0, (i1 - N_GROUPS).astype(F32), 0.0)
    rec = jnp.where(lane == 1, (i2 - N_GROUPS).astype(F32), rec)
    rec = jnp.where(lane == 2, g_gate * w1, rec)
    rec = jnp.where(lane == 3, g_gate * w2, rec)
    route_ref[...] = rec


def _merge(r, fo, gates, x, wr, wf, wo, g1, b1, wrh, wrl, rbias, alpha, tm):
    t, d = x.shape
    row = lambda w: pl.BlockSpec((tm, w), lambda i: (i, 0))
    full = lambda a: pl.BlockSpec(a.shape, lambda i: (0, 0))
    g1 = g1.reshape(1, d)
    b1 = b1.reshape(1, d)
    return pl.pallas_call(
        functools.partial(_merge_kernel, alpha=alpha, d_model=d),
        out_shape=(jax.ShapeDtypeStruct((t, d), F32),
                   jax.ShapeDtypeStruct((t, d // 2), jnp.uint32),
                   jax.ShapeDtypeStruct((t, ROUTE_LANES), F32)),
        grid=(t // tm,),
        in_specs=[row(r.shape[1]), row(fo.shape[1]), row(gates.shape[1]), row(d),
                  full(wr), full(wf), full(wo), full(g1), full(b1), full(wrh), full(wrl), full(rbias)],
        out_specs=(row(d), row(d // 2), row(ROUTE_LANES)),
        compiler_params=_cparams(("parallel",)),
        name="merge_router",
    )(r, fo, gates, x, wr, wf, wo, g1, b1, wrh, wrl, rbias)


def _gather_kernel(idx_ref, tab_ref, o_ref, sem, *, rows):
    def issue(i, carry):
        src = idx_ref[0, 0, i]
        pltpu.make_async_copy(tab_ref.at[pl.ds(src, 1)], o_ref.at[pl.ds(i, 1)], sem).start()
        return carry

    lax.fori_loop(0, rows, issue, 0)
    pltpu.make_async_copy(tab_ref.at[pl.ds(0, rows)], o_ref, sem).wait()


def _gather_rows(table, idx, rows=GATHER_TILE):
    n = idx.shape[0]
    w = table.shape[1]
    idx3 = idx.reshape(n // rows, 1, rows)
    return pl.pallas_call(
        functools.partial(_gather_kernel, rows=rows),
        out_shape=jax.ShapeDtypeStruct((n, w), table.dtype),
        grid=(n // rows,),
        in_specs=[pl.BlockSpec((1, 1, rows), lambda i: (i, 0, 0), memory_space=pltpu.SMEM),
                  pl.BlockSpec(memory_space=pl.ANY)],
        out_specs=pl.BlockSpec((rows, w), lambda i: (i, 0)),
        scratch_shapes=[pltpu.SemaphoreType.DMA],
        compiler_params=_cparams(("arbitrary",)),
        name="gather_rows",
    )(idx3, table)


def _expert_kernel(te_ref, nv_ref, xs_ref, wg_ref, wu_ref, wd_ref, o_ref):
    i = pl.program_id(0)

    @pl.when(i < nv_ref[0])
    def _():
        x_lo, x_hi = _unpack_halves(xs_ref[...])
        x_lo = x_lo.astype(BF16)
        x_hi = x_hi.astype(BF16)
        half = x_lo.shape[1]

        def up(w_ref):
            return (jnp.dot(x_lo, w_ref[:half, :], preferred_element_type=F32)
                    + jnp.dot(x_hi, w_ref[half:, :], preferred_element_type=F32))

        g = up(wg_ref)
        hmid = (g * jax.nn.sigmoid(g) * up(wu_ref)).astype(BF16)
        o_ref[...] = _pack_halves(jnp.dot(hmid, wd_ref[...], preferred_element_type=F32))

    @pl.when(i >= nv_ref[0])
    def _():
        o_ref[...] = jnp.zeros_like(o_ref)


def _expert_mlp(tile_expert, n_valid, xs, wg, wu, wd):
    p, hw = xs.shape
    tm = MOE_TILE
    d, de = wg.shape[1], wg.shape[2]
    xmap = lambda i, te, nv: (jnp.minimum(i, nv[0] - 1), 0)
    wmap = lambda i, te, nv: (te[i], 0, 0)
    return pl.pallas_call(
        _expert_kernel,
        out_shape=jax.ShapeDtypeStruct((p, hw), jnp.uint32),
        grid_spec=pltpu.PrefetchScalarGridSpec(
            num_scalar_prefetch=2,
            grid=(p // tm,),
            in_specs=[pl.BlockSpec((tm, hw), xmap),
                      pl.BlockSpec((None, d, de), wmap),
                      pl.BlockSpec((None, d, de), wmap),
                      pl.BlockSpec((None, de, d), wmap)],
            out_specs=pl.BlockSpec((tm, hw), lambda i, te, nv: (i, 0))),
        compiler_params=_cparams(("arbitrary",)),
        name="expert_mlp",
    )(tile_expert, n_valid, xs, wg, wu, wd)


def _combine_kernel(y_ref, route_ref, x_ref, g_ref, b_ref, o_ref, *, alpha):
    half = y_ref.shape[1] // 2
    a_lo, a_hi = _unpack_halves(y_ref[:, :half])
    b_lo, b_hi = _unpack_halves(y_ref[:, half:])
    c1 = route_ref[:, 2:3]
    c2 = route_ref[:, 3:4]
    ff = jnp.concatenate([c1 * a_lo + c2 * b_lo, c1 * a_hi + c2 * b_hi], axis=1)
    o_ref[...] = _layer_norm_rows(alpha * x_ref[...] + ff, g_ref[...], b_ref[...])


def _combine(yg, route, x1, g2, b2, alpha, tm):
    t, d = x1.shape
    row = lambda w: pl.BlockSpec((tm, w), lambda i: (i, 0))
    vec = pl.BlockSpec((1, d), lambda i: (0, 0))
    return pl.pallas_call(
        functools.partial(_combine_kernel, alpha=alpha),
        out_shape=jax.ShapeDtypeStruct((t, d), F32),
        grid=(t // tm,),
        in_specs=[row(yg.shape[1]), row(ROUTE_LANES), row(d), vec, vec],
        out_specs=row(d),
        compiler_params=_cparams(("parallel",)),
        name="combine_ln2",
    )(yg, route, x1, g2.reshape(1, d), b2.reshape(1, d))


def _rope_tables(seq):
    half = QK_DIM // 2
    inv_freq = 1.0 / (ROPE_BASE ** (jnp.arange(half, dtype=F32) / half))
    ang = jnp.arange(seq, dtype=F32)[:, None] * inv_freq[None, :]
    cos, sin = jnp.cos(ang), jnp.sin(ang)
    return jnp.concatenate([cos, cos], axis=1), jnp.concatenate([-sin, sin], axis=1)


def _channel_dft_matrix():
    n = np.arange(FG_DIM)
    ang = 2.0 * np.pi * np.outer(n, n) / FG_DIM
    eye = np.eye(N_FGROUPS)
    bd = np.concatenate([np.kron(eye, np.cos(ang)), np.kron(eye, np.sin(ang))], axis=1) / math.sqrt(FG_DIM)
    return jnp.asarray(bd, dtype=BF16)


def _seq_dft_matrices(seq):
    blk = 64 if seq % 64 == 0 else 1
    k = jnp.arange(seq, dtype=jnp.int32)[:, None]
    hi = jnp.arange(seq // blk, dtype=jnp.int32)[None, :] * blk
    lo = jnp.arange(blk, dtype=jnp.int32)[None, :]
    w = 2.0 * math.pi / seq
    a_hi = ((k * hi) % seq).astype(F32) * w
    a_lo = ((k * lo) % seq).astype(F32) * w
    ch, sh = jnp.cos(a_hi)[:, :, None], jnp.sin(a_hi)[:, :, None]
    cl, sl = jnp.cos(a_lo)[:, None, :], jnp.sin(a_lo)[:, None, :]
    scale = seq ** -0.5
    cmat = ((ch * cl - sh * sl) * scale).reshape(seq, seq).astype(BF16)
    smat = ((sh * cl + ch * sl) * scale).reshape(seq, seq).astype(BF16)
    return cmat, smat


def _pad_heads(w, axis):
    shp = list(w.shape)
    shp[axis:axis + 1] = [N_HEADS, V_DIM]
    w = w.reshape(shp)
    pad = [(0, 0)] * w.ndim
    pad[axis + 1] = (0, V_PAD - V_DIM)
    w = jnp.pad(w, pad)
    shp[axis:axis + 2] = [N_HEADS * V_PAD]
    return w.reshape(shp)


def _pad_in_weight(w_in):
    qk_w = 2 * N_HEADS * QK_DIM
    vw = N_HEADS * V_DIM
    qk = w_in[:, :qk_w]
    v = _pad_heads(w_in[:, qk_w:qk_w + vw], 1)
    g = _pad_heads(w_in[:, qk_w + vw:qk_w + 2 * vw], 1)
    rest = w_in[:, qk_w + 2 * vw:]
    return jnp.concatenate([qk, v, g, rest], axis=1).astype(BF16)


def _routing_plan(route, n_tiles):
    t = route.shape[0]
    e = route[:, :2].astype(jnp.int32).reshape(-1)
    onehot = (e[:, None] == jnp.arange(N_EXPERTS, dtype=jnp.int32)[None, :]).astype(jnp.int32)
    csum = jnp.cumsum(onehot, axis=0)
    counts = csum[-1]
    rank = jnp.take_along_axis(csum, e[:, None], axis=1)[:, 0] - 1
    tiles_per = (counts + MOE_TILE - 1) // MOE_TILE
    tile_end = jnp.cumsum(tiles_per)
    tile_start = tile_end - tiles_per
    pos = tile_start[e] * MOE_TILE + rank
    n_valid = tile_end[-1]
    src = jnp.zeros((n_tiles * MOE_TILE,), jnp.int32).at[pos].set(jnp.arange(2 * t, dtype=jnp.int32) // 2)
    tile_ids = jnp.arange(n_tiles, dtype=jnp.int32)
    tile_expert = jnp.minimum(jnp.searchsorted(tile_end, tile_ids, side="right"), N_EXPERTS - 1).astype(jnp.int32)
    last_e = tile_expert[jnp.maximum(n_valid - 1, 0)]
    tile_expert = jnp.where(tile_ids < n_valid, tile_expert, last_e)
    return src, pos, tile_expert, n_valid.reshape(1).astype(jnp.int32)


def _largest_tile(cands, *dims):
    for c in cands:
        if all(d % c == 0 for d in dims):
            return c
    raise ValueError(f"no tile in {cands} divides {dims}")


def kernel(x_prompt, x_sample, ln_in_g, ln_in_b, w_in, ret_decay_fwd, ret_decay_bwd, ret_gn_g, ret_gn_b, w_ret_up, w_four_up, w_out, ln1_g, ln1_b, w_route_group, b_route_group, w_route_expert, b_route_expert, w_expert_gate, w_expert_up, w_expert_down, ln2_g, ln2_b):
    bp, sp, d = x_prompt.shape
    bs, ss, _ = x_sample.shape
    depth = w_in.shape[0]
    tp, ts = bp * sp, bs * ss
    t = tp + ts
    assert sp % CHUNK == 0 and ss % CHUNK == 0
    assert tp % ss == 0, "sample sequences must start on a sequence-sized row block"
    alpha = float((2 * depth) ** 0.25)

    tm_in = _largest_tile((512, 256, 128), sp, ss)
    tm_tok = _largest_tile((TOKEN_TILE, 128), t)
    assert (2 * t) % GATHER_TILE == 0

    cos_t, sin_t = _rope_tables(max(sp, ss))
    bd = _channel_dft_matrix()
    dft = {s: _seq_dft_matrices(s) for s in sorted({sp, ss})}

    x = jnp.concatenate([x_prompt.reshape(tp, d), x_sample.reshape(ts, d)], axis=0)
    x = _layer_norm(x, ln_in_g, ln_in_b, tm_tok)

    n_tiles = (2 * t) // MOE_TILE + N_EXPERTS
    for l in range(depth):
        w_pad = _pad_in_weight(w_in[l])
        dec = jnp.stack([ret_decay_fwd[l], ret_decay_bwd[l]]).astype(F32)
        gn_g = _pad_heads(ret_gn_g[l], 0).reshape(1, -1).astype(F32)
        gn_b = _pad_heads(ret_gn_b[l], 0).reshape(1, -1).astype(F32)
        wr = _pad_heads(w_ret_up[l], 0).astype(BF16)
        wf = w_four_up[l].astype(BF16)
        wo = w_out[l].astype(BF16)
        w_rt = jnp.concatenate([w_route_group[l], w_route_expert[l]], axis=1)
        w_rt = jnp.pad(w_rt, ((0, 0), (0, ROUTE_LANES - w_rt.shape[1])))
        wrh = w_rt.astype(BF16)
        wrl = (w_rt - wrh.astype(F32)).astype(BF16)
        rbias = jnp.concatenate([b_route_group[l], b_route_expert[l]])
        rbias = jnp.pad(rbias, (0, ROUTE_LANES - rbias.shape[0])).reshape(1, ROUTE_LANES).astype(F32)
        wg = w_expert_gate[l].astype(BF16)
        wu = w_expert_up[l].astype(BF16)
        wd = w_expert_down[l].astype(BF16)

        qk, v, sg, z, gates = _inproj(x, w_pad, cos_t, sin_t, bd, tm_in,
                                      tp // tm_in, sp // tm_in, ss // tm_in)

        r = jnp.concatenate([
            _retention(dec, qk, v, sg, gn_g, gn_b, bp, sp, 0),
            _retention(dec, qk, v, sg, gn_g, gn_b, bs, ss, tp // ss)], axis=0)

        fo_parts = []
        for (nb, s, r0) in ((bp, sp, 0), (bs, ss, tp)):
            zz = z[r0:r0 + nb * s].reshape(nb, s, 2, F_WIDTH).transpose(1, 2, 0, 3).reshape(s, 2 * nb * F_WIDTH)
            fo = _seq_dft(dft[s][0], dft[s][1], zz)
            fo_parts.append(fo.reshape(s, nb, F_WIDTH).transpose(1, 0, 2).reshape(nb * s, F_WIDTH))
        fo = jnp.concatenate(fo_parts, axis=0)

        x1, x1p, route = _merge(r, fo, gates, x, wr, wf, wo, ln1_g[l], ln1_b[l], wrh, wrl, rbias, alpha, tm_tok)

        src, pos, tile_expert, n_valid = _routing_plan(route, n_tiles)
        xs = _gather_rows(x1p, src)
        ys = _expert_mlp(tile_expert, n_valid, xs, wg, wu, wd)
        yg = _gather_rows(ys, pos).reshape(t, d)
        x = _combine(yg, route, x1, ln2_g[l], ln2_b[l], alpha, tm_tok)

    return x[:tp].reshape(bp, sp, d), x[tp:].reshape(bs, ss, d)
```

```python
import functools
import math

import jax
import jax.numpy as jnp
import numpy as np
from jax import lax
from jax.experimental import pallas as pl
from jax.experimental.pallas import tpu as pltpu
from jax.experimental.pallas import tpu_sc as plsc

N_HEADS = 4
QK_DIM = 128
V_DIM = 192
V_PAD = 256
CHUNK = 128
ROPE_BASE = 10000.0
N_FGROUPS = 4
FG_DIM = 64
F_WIDTH = N_FGROUPS * FG_DIM
N_GROUPS = 4
EXPERTS_PER_GROUP = 8
N_EXPERTS = N_GROUPS * EXPERTS_PER_GROUP
LN_EPS = 1e-5
ROUTE_LANES = 128

LANE = 128
VMEM_LIMIT_BYTES = 56 * 1024 * 1024
MOE_TILE = 256
GATHER_TILE = 256
SC_GATHER_WINDOW = 128
SC_ROW_SPLIT = 2
TOKEN_TILE = 256
RET_UNROLL = 8

F32 = jnp.float32
BF16 = jnp.bfloat16


def _cparams(sem):
    return pltpu.CompilerParams(dimension_semantics=sem, vmem_limit_bytes=VMEM_LIMIT_BYTES)


def _layer_norm_rows(y, g, b):
    mu = jnp.mean(y, axis=-1, keepdims=True)
    d = y - mu
    var = jnp.mean(d * d, axis=-1, keepdims=True)
    return d * lax.rsqrt(var + LN_EPS) * g + b


def _pack_halves(y):
    n = y.shape[1] // 2
    lo = pltpu.bitcast(y[:, :n].astype(BF16).astype(F32), jnp.uint32)
    hi = pltpu.bitcast(y[:, n:].astype(BF16).astype(F32), jnp.uint32)
    return (hi & jnp.uint32(0xFFFF0000)) | (lo >> 16)


def _unpack_halves(w):
    lo = pltpu.bitcast(w << 16, F32)
    hi = pltpu.bitcast(w & jnp.uint32(0xFFFF0000), F32)
    return lo, hi


def _ln_kernel(xp_ref, xs_ref, g_ref, b_ref, o_ref, *, n_p_tiles):
    i = pl.program_id(0)

    @pl.when(i < n_p_tiles)
    def _():
        o_ref[...] = _layer_norm_rows(xp_ref[...], g_ref[...], b_ref[...])

    @pl.when(i >= n_p_tiles)
    def _():
        o_ref[...] = _layer_norm_rows(xs_ref[...], g_ref[...], b_ref[...])


def _layer_norm_concat(xp, xs, g, b, tm):
    (tp, d), ts = xp.shape, xs.shape[0]
    n_p, n_s = tp // tm, ts // tm
    return pl.pallas_call(
        functools.partial(_ln_kernel, n_p_tiles=n_p),
        out_shape=jax.ShapeDtypeStruct((tp + ts, d), F32),
        grid=(n_p + n_s,),
        in_specs=[pl.BlockSpec((tm, d), lambda i: (jnp.minimum(i, n_p - 1), 0)),
                  pl.BlockSpec((tm, d), lambda i: (jnp.maximum(i - n_p, 0), 0)),
                  pl.BlockSpec((1, d), lambda i: (0, 0)),
                  pl.BlockSpec((1, d), lambda i: (0, 0))],
        out_specs=pl.BlockSpec((tm, d), lambda i: (i, 0)),
        compiler_params=_cparams(("arbitrary",)),
        name="ln_in",
    )(xp, xs, g.reshape(1, d), b.reshape(1, d))


def _inproj_kernel(x_ref, w_ref, cos_ref, sin_ref, bd_ref,
                   qk_ref, v_ref, sg_ref, z_ref, gates_ref, *, d_model):
    x = x_ref[...].astype(BF16)
    cos = cos_ref[...]
    sin = sin_ref[...]
    qk_w = 2 * N_HEADS * QK_DIM
    v_w = N_HEADS * V_PAD
    off_v = qk_w
    off_g = off_v + v_w
    off_f = off_g + v_w
    off_gr = off_f + F_WIDTH
    cw = 256

    def proj(c0):
        return jnp.dot(x, w_ref[:, c0:c0 + cw], preferred_element_type=F32)

    for c0 in range(0, qk_w, cw):
        acc = proj(c0)
        scale = QK_DIM ** -0.5 if c0 < N_HEADS * QK_DIM else 1.0
        parts = []
        for hh in range(cw // QK_DIM):
            a = acc[:, hh * QK_DIM:(hh + 1) * QK_DIM]
            rot = a * cos + pltpu.roll(a, QK_DIM // 2, axis=1) * sin
            if scale != 1.0:
                rot = rot * scale
            parts.append(rot.astype(BF16))
        qk_ref[:, c0:c0 + cw] = jnp.concatenate(parts, axis=1)
    for c0 in range(0, v_w, cw):
        v_ref[:, c0:c0 + cw] = proj(off_v + c0).astype(BF16)
    for c0 in range(0, v_w, cw):
        a = proj(off_g + c0)
        sg_ref[:, c0:c0 + cw] = (a * jax.nn.sigmoid(a)).astype(BF16)
    f = proj(off_f).astype(BF16)
    z_ref[...] = jnp.dot(f, bd_ref[...], preferred_element_type=F32).astype(BF16)
    for c0 in range(0, 2 * d_model, cw):
        gates_ref[:, c0:c0 + cw] = jax.nn.sigmoid(proj(off_gr + c0)).astype(BF16)


def _inproj(x, w_pad, cos_t, sin_t, bd, tm, n_p_tiles, sp_tiles, ss_tiles):
    t, d = x.shape
    n_cols = w_pad.shape[1]

    def pos_map(i):
        return (jnp.where(i < n_p_tiles, i % sp_tiles, (i - n_p_tiles) % ss_tiles), 0)

    outs = (jax.ShapeDtypeStruct((t, 2 * N_HEADS * QK_DIM), BF16),
            jax.ShapeDtypeStruct((t, N_HEADS * V_PAD), BF16),
            jax.ShapeDtypeStruct((t, N_HEADS * V_PAD), BF16),
            jax.ShapeDtypeStruct((t, 2 * F_WIDTH), BF16),
            jax.ShapeDtypeStruct((t, 2 * d), BF16))
    row = lambda w: pl.BlockSpec((tm, w), lambda i: (i, 0))
    return pl.pallas_call(
        functools.partial(_inproj_kernel, d_model=d),
        out_shape=outs,
        grid=(t // tm,),
        in_specs=[row(d),
                  pl.BlockSpec((d, n_cols), lambda i: (0, 0)),
                  pl.BlockSpec((tm, QK_DIM), pos_map),
                  pl.BlockSpec((tm, QK_DIM), pos_map),
                  pl.BlockSpec((F_WIDTH, 2 * F_WIDTH), lambda i: (0, 0))],
        out_specs=tuple(row(o.shape[1]) for o in outs),
        compiler_params=_cparams(("parallel",)),
        name="inproj",
    )(x, w_pad, cos_t, sin_t, bd)


def _log_sigmoid(x):
    return jnp.minimum(x, 0.0) - jnp.log1p(jnp.exp(-jnp.abs(x)))


def _retention_kernel(dec_ref, q_ref, k_ref, v_ref, sg_ref, gng_ref, gnb_ref, o_ref,
                      r_all, rf_s, rb_s, *, n_chunks):
    h = pl.program_id(1)
    c = CHUNK
    lf = _log_sigmoid(jnp.full((1, c), dec_ref[0, h], F32))
    lb = _log_sigmoid(jnp.full((1, c), dec_ref[1, h], F32))
    lf_w = _log_sigmoid(jnp.full((1, V_PAD), dec_ref[0, h], F32))
    lb_w = _log_sigmoid(jnp.full((1, V_PAD), dec_ref[1, h], F32))
    row = lax.broadcasted_iota(jnp.int32, (c, c), 0).astype(F32)
    col = lax.broadcasted_iota(jnp.int32, (c, c), 1).astype(F32)
    diff = row - col
    dmat = jnp.where(diff >= 0.0, jnp.exp(lf * jnp.maximum(diff, 0.0)),
                     jnp.exp(lb * jnp.maximum(-diff, 0.0)))
    qd_f = jnp.exp(lf * (row + 1.0))
    kd_f = jnp.exp(lf * (c - 1.0 - row))
    qd_b = jnp.exp(lb * (c - row))
    kd_b = jnp.exp(lb * row)
    cf = jnp.exp(lf_w * float(c))
    cb = jnp.exp(lb_w * float(c))
    lane = lax.broadcasted_iota(jnp.int32, (c, V_PAD), 1)
    vmask = (lane < V_DIM).astype(F32)
    gn_g = gng_ref[...]
    gn_b = gnb_ref[...]
    tdot = (((0,), (0,)), ((), ()))

    rf_s[...] = jnp.zeros_like(rf_s)
    rb_s[...] = jnp.zeros_like(rb_s)

    def state_step(t, carry):
        for (ci, kd, cdec, r_s, half) in ((t, kd_f, cf, rf_s, 0), (n_chunks - 1 - t, kd_b, cb, rb_s, 1)):
            r0 = pl.multiple_of(ci * c, c)
            kk = (k_ref[pl.ds(r0, c), :].astype(F32) * kd).astype(BF16)
            kv = lax.dot_general(kk, v_ref[pl.ds(r0, c), :], tdot, preferred_element_type=F32)
            r = r_s[...]
            r_all[ci, half * c:(half + 1) * c, :] = r.astype(BF16)
            r_s[...] = cdec * r + kv
        return carry

    lax.fori_loop(0, n_chunks, state_step, 0, unroll=RET_UNROLL)

    def out_step(ci, carry):
        r0 = pl.multiple_of(ci * c, c)
        q = q_ref[pl.ds(r0, c), :]
        qf = q.astype(F32)
        s = lax.dot_general(q, k_ref[pl.ds(r0, c), :], (((1,), (1,)), ((), ())), preferred_element_type=F32)
        p = (s * dmat).astype(BF16)
        o = jnp.dot(p, v_ref[pl.ds(r0, c), :], preferred_element_type=F32)
        qq = jnp.concatenate([(qf * qd_f).astype(BF16), (qf * qd_b).astype(BF16)], axis=1)
        o = o + jnp.dot(qq, r_all[ci], preferred_element_type=F32)
        mu = jnp.sum(o, axis=-1, keepdims=True) * (1.0 / V_DIM)
        d = (o - mu) * vmask
        var = jnp.sum(d * d, axis=-1, keepdims=True) * (1.0 / V_DIM)
        y = d * lax.rsqrt(var + LN_EPS) * gn_g + gn_b
        o_ref[pl.ds(r0, c), :] = (y * sg_ref[pl.ds(r0, c), :].astype(F32)).astype(BF16)
        return carry

    lax.fori_loop(0, n_chunks, out_step, 0, unroll=RET_UNROLL)


def _retention(dec, qk, v, sg, gn_g, gn_b, n_batch, seq, row_block0):
    n_chunks = seq // CHUNK
    rb = lambda b: b + row_block0
    return pl.pallas_call(
        functools.partial(_retention_kernel, n_chunks=n_chunks),
        out_shape=jax.ShapeDtypeStruct((n_batch * seq, N_HEADS * V_PAD), BF16),
        grid_spec=pltpu.PrefetchScalarGridSpec(
            num_scalar_prefetch=0,
            grid=(n_batch, N_HEADS),
            in_specs=[pl.BlockSpec(memory_space=pltpu.SMEM),
                      pl.BlockSpec((seq, QK_DIM), lambda b, h: (rb(b), h)),
                      pl.BlockSpec((seq, QK_DIM), lambda b, h: (rb(b), N_HEADS + h)),
                      pl.BlockSpec((seq, V_PAD), lambda b, h: (rb(b), h)),
                      pl.BlockSpec((seq, V_PAD), lambda b, h: (rb(b), h)),
                      pl.BlockSpec((1, V_PAD), lambda b, h: (0, h)),
                      pl.BlockSpec((1, V_PAD), lambda b, h: (0, h))],
            out_specs=pl.BlockSpec((seq, V_PAD), lambda b, h: (b, h)),
            scratch_shapes=[pltpu.VMEM((n_chunks, 2 * CHUNK, V_PAD), BF16),
                            pltpu.VMEM((CHUNK, V_PAD), F32),
                            pltpu.VMEM((CHUNK, V_PAD), F32)]),
        compiler_params=_cparams(("parallel", "arbitrary")),
        name="retention",
    )(dec, qk, qk, v, sg, gn_g, gn_b)


def _seqdft_kernel(c_ref, s_ref, zc_ref, zs_ref, o_ref, acc_ref):
    kk = pl.program_id(2)

    @pl.when(kk == 0)
    def _():
        acc_ref[...] = jnp.zeros_like(acc_ref)

    acc_ref[...] += (jnp.dot(c_ref[...], zc_ref[...], preferred_element_type=F32)
                     - jnp.dot(s_ref[...], zs_ref[...], preferred_element_type=F32))

    @pl.when(kk == pl.num_programs(2) - 1)
    def _():
        o_ref[...] = acc_ref[...].astype(o_ref.dtype)


def _seq_dft(cmat, smat, z):
    s = cmat.shape[0]
    n = z.shape[1] // 2
    tm = min(1024, s)
    tn = min(1024, n)
    tk = min(512, s)
    nj = n // tn
    return pl.pallas_call(
        _seqdft_kernel,
        out_shape=jax.ShapeDtypeStruct((s, n), BF16),
        grid=(s // tm, nj, s // tk),
        in_specs=[pl.BlockSpec((tm, tk), lambda i, j, k: (i, k)),
                  pl.BlockSpec((tm, tk), lambda i, j, k: (i, k)),
                  pl.BlockSpec((tk, tn), lambda i, j, k: (k, j)),
                  pl.BlockSpec((tk, tn), lambda i, j, k: (k, nj + j))],
        out_specs=pl.BlockSpec((tm, tn), lambda i, j, k: (i, j)),
        scratch_shapes=[pltpu.VMEM((tm, tn), F32)],
        compiler_params=_cparams(("parallel", "parallel", "arbitrary")),
        name="seq_dft",
    )(cmat, smat, z, z)


def _merge_kernel(rp_ref, rs_ref, fop_ref, fos_ref, gates_ref, x_ref, wr_ref, wf_ref, wo_ref, g1_ref, b1_ref,
                  wr2_ref, rb_ref, x1_ref, x1p_ref, route_ref, cnt_ref, *, alpha, d_model, n_p_tiles):
    is_p = pl.program_id(0) < n_p_tiles
    r = jnp.where(is_p, rp_ref[...], rs_ref[...])
    fo = jnp.where(is_p, fop_ref[...], fos_ref[...])
    a = jnp.dot(r, wr_ref[...], preferred_element_type=F32)
    b = jnp.dot(fo, wf_ref[...], preferred_element_type=F32)
    merged = (gates_ref[:, :d_model].astype(F32) * a + gates_ref[:, d_model:].astype(F32) * b)
    mix = jnp.dot(merged.astype(BF16), wo_ref[...], preferred_element_type=F32)
    x1 = _layer_norm_rows(alpha * x_ref[...] + mix, g1_ref[...], b1_ref[...])
    x1_ref[...] = x1
    x1p_ref[...] = _pack_halves(x1)

    xh = x1.astype(BF16)
    xl = (x1 - xh.astype(F32)).astype(BF16)
    s2 = (jnp.dot(xh, wr2_ref[...], preferred_element_type=F32)
          + jnp.dot(xl, wr2_ref[...], preferred_element_type=F32))
    logits = s2 + pltpu.roll(s2, ROUTE_LANES // 2, axis=1) + rb_ref[...]
    tm = logits.shape[0]
    lane = lax.broadcasted_iota(jnp.int32, (tm, ROUTE_LANES), 1)
    neg = jnp.float32(-jnp.inf)
    big = jnp.int32(ROUTE_LANES)

    def top1(mask):
        vals = jnp.where(mask, logits, neg)
        m = jnp.max(vals, axis=-1, keepdims=True)
        idx = jnp.min(jnp.where(mask & (vals == m), lane, big), axis=-1, keepdims=True)
        return m, idx

    gmask = lane < N_GROUPS
    gm, gi = top1(gmask)
    gsum = jnp.sum(jnp.where(gmask, jnp.exp(logits - gm), 0.0), axis=-1, keepdims=True)
    g_gate = 1.0 / gsum
    lo = N_GROUPS + gi * EXPERTS_PER_GROUP
    emask = (lane >= lo) & (lane < lo + EXPERTS_PER_GROUP)
    m1, i1 = top1(emask)
    m2, i2 = top1(emask & (lane != i1))
    e2 = jnp.exp(m2 - m1)
    w1 = 1.0 / (1.0 + e2)
    w2 = e2 / (1.0 + e2)
    oh1 = (lane == i1 - N_GROUPS).astype(F32)
    oh2 = (lane == i2 - N_GROUPS).astype(F32)
    earlier = (lax.broadcasted_iota(jnp.int32, (tm, tm), 1)
               < lax.broadcasted_iota(jnp.int32, (tm, tm), 0)).astype(BF16)
    tot1 = jnp.sum(oh1, axis=0, keepdims=True)
    tot2 = jnp.sum(oh2, axis=0, keepdims=True)
    pre1 = jnp.dot(earlier, oh1.astype(BF16), preferred_element_type=F32)
    pre2 = jnp.dot(earlier, oh2.astype(BF16), preferred_element_type=F32) + tot1
    r1 = jnp.sum(pre1 * oh1, axis=-1, keepdims=True)
    r2 = jnp.sum(pre2 * oh2, axis=-1, keepdims=True)
    cnt_ref[0] = tot1 + tot2

    rec = jnp.where(lane == 0, (i1 - N_GROUPS).astype(F32), 0.0)
    rec = jnp.where(lane == 1, (i2 - N_GROUPS).astype(F32), rec)
    rec = jnp.where(lane == 2, g_gate * w1, rec)
    rec = jnp.where(lane == 3, g_gate * w2, rec)
    rec = jnp.where(lane == 4, r1, rec)
    rec = jnp.where(lane == 5, r2, rec)
    route_ref[...] = rec


def _merge(rp, rs, fop, fos, gates, x, wr, wf, wo, g1, b1, wr2, rbias, alpha, tm):
    t, d = x.shape
    n_p = rp.shape[0] // tm
    row = lambda w: pl.BlockSpec((tm, w), lambda i: (i, 0))
    row_p = lambda w: pl.BlockSpec((tm, w), lambda i: (jnp.minimum(i, n_p - 1), 0))
    row_s = lambda w: pl.BlockSpec((tm, w), lambda i: (jnp.maximum(i - n_p, 0), 0))
    full = lambda a: pl.BlockSpec(a.shape, lambda i: (0, 0))
    g1 = g1.reshape(1, d)
    b1 = b1.reshape(1, d)
    return pl.pallas_call(
        functools.partial(_merge_kernel, alpha=alpha, d_model=d, n_p_tiles=n_p),
        out_shape=(jax.ShapeDtypeStruct((t, d), F32),
                   jax.ShapeDtypeStruct((t, d // 2), jnp.uint32),
                   jax.ShapeDtypeStruct((t, ROUTE_LANES), F32),
                   jax.ShapeDtypeStruct((t // tm, 1, ROUTE_LANES), F32)),
        grid=(t // tm,),
        in_specs=[row_p(rp.shape[1]), row_s(rs.shape[1]), row_p(fop.shape[1]), row_s(fos.shape[1]),
                  row(gates.shape[1]), row(d),
                  full(wr), full(wf), full(wo), full(g1), full(b1), full(wr2), full(rbias)],
        out_specs=(row(d), row(d // 2), row(ROUTE_LANES),
                   pl.BlockSpec((1, 1, ROUTE_LANES), lambda i: (i, 0, 0))),
        compiler_params=_cparams(("arbitrary",)),
        name="merge_router",
    )(rp, rs, fop, fos, gates, x, wr, wf, wo, g1, b1, wr2, rbias)


def _split_rows(table, idx):
    pieces = idx[:, None] * SC_ROW_SPLIT + jnp.arange(SC_ROW_SPLIT, dtype=idx.dtype)[None, :]
    return table.reshape(table.shape[0] * SC_ROW_SPLIT, table.shape[1] // SC_ROW_SPLIT), pieces.reshape(1, -1)


def _sc_mesh():
    return plsc.VectorSubcoreMesh(core_axis_name="c", subcore_axis_name="s")


def _sc_gather_rows(table, idx):
    n_rows, w_full = idx.shape[0], table.shape[1]
    table, idx = _split_rows(table, idx)
    n, w = idx.shape[1], table.shape[1]

    @functools.partial(pl.kernel, out_type=jax.ShapeDtypeStruct((n, w), table.dtype), mesh=_sc_mesh(),
                       scratch_types=[])
    def gather(tab_hbm, idx_hbm, out_hbm):
        def body(idx_vmem, out_vmem):
            pltpu.sync_copy(tab_hbm.at[idx_vmem.at[0]], out_vmem)

        pltpu.emit_pipeline(
            body,
            grid=(n // SC_GATHER_WINDOW,),
            in_specs=[pl.BlockSpec((1, SC_GATHER_WINDOW), lambda i: (0, i))],
            out_specs=[pl.BlockSpec((SC_GATHER_WINDOW, w), lambda i: (i, 0))],
            core_axis_name=("c", "s"),
            dimension_semantics=(pltpu.PARALLEL,),
        )(idx_hbm, out_hbm)

    return gather(table, idx).reshape(n_rows, w_full)


def _sc_scatter_rows(src, idx, n_out):
    assert idx.shape[0] == n_out
    w_full = src.shape[1]
    src, idx = _split_rows(src, idx)
    n, w = idx.shape[1], src.shape[1]
    src_windows = src.shape[0] // SC_GATHER_WINDOW

    @functools.partial(pl.kernel, out_type=jax.ShapeDtypeStruct((n, w), src.dtype), mesh=_sc_mesh(),
                       scratch_types=[])
    def scatter(src_hbm, idx_hbm, out_hbm):
        def body(src_vmem, idx_vmem):
            pltpu.sync_copy(src_vmem, out_hbm.at[idx_vmem.at[0]])

        pltpu.emit_pipeline(
            body,
            grid=(n // SC_GATHER_WINDOW,),
            in_specs=[pl.BlockSpec((SC_GATHER_WINDOW, w), lambda i: (i % src_windows, 0)),
                      pl.BlockSpec((1, SC_GATHER_WINDOW), lambda i: (0, i))],
            out_specs=[],
            core_axis_name=("c", "s"),
            dimension_semantics=(pltpu.PARALLEL,),
        )(src_hbm, idx_hbm)

    return scatter(src, idx).reshape(n_out, w_full)


def _expert_kernel(te_ref, nv_ref, xs_ref, wg_ref, wu_ref, wd_ref, o_ref, wg_s, wu_s, wd_s):
    i = pl.program_id(0)

    @pl.when((i == 0) | (te_ref[i] != te_ref[jnp.maximum(i - 1, 0)]))
    def _():
        wg_s[...] = wg_ref[...].astype(BF16)
        wu_s[...] = wu_ref[...].astype(BF16)
        wd_s[...] = wd_ref[...].astype(BF16)

    @pl.when(i < nv_ref[0])
    def _():
        x_lo, x_hi = _unpack_halves(xs_ref[...])
        x_lo = x_lo.astype(BF16)
        x_hi = x_hi.astype(BF16)
        half = x_lo.shape[1]

        def up(w_s):
            return (jnp.dot(x_lo, w_s[:half, :], preferred_element_type=F32)
                    + jnp.dot(x_hi, w_s[half:, :], preferred_element_type=F32))

        g = up(wg_s)
        hmid = (g * jax.nn.sigmoid(g) * up(wu_s)).astype(BF16)
        o_ref[...] = _pack_halves(jnp.dot(hmid, wd_s[...], preferred_element_type=F32))

    @pl.when(i >= nv_ref[0])
    def _():
        o_ref[...] = jnp.zeros_like(o_ref)


def _expert_mlp(tile_expert, n_valid, xs, wg, wu, wd):
    p, hw = xs.shape
    tm = MOE_TILE
    d, de = wg.shape[1], wg.shape[2]
    xmap = lambda i, te, nv: (jnp.minimum(i, nv[0] - 1), 0)
    wmap = lambda i, te, nv: (te[i], 0, 0)
    return pl.pallas_call(
        _expert_kernel,
        out_shape=jax.ShapeDtypeStruct((p, hw), jnp.uint32),
        grid_spec=pltpu.PrefetchScalarGridSpec(
            num_scalar_prefetch=2,
            grid=(p // tm,),
            in_specs=[pl.BlockSpec((tm, hw), xmap),
                      pl.BlockSpec((None, d, de), wmap),
                      pl.BlockSpec((None, d, de), wmap),
                      pl.BlockSpec((None, de, d), wmap)],
            out_specs=pl.BlockSpec((tm, hw), lambda i, te, nv: (i, 0)),
            scratch_shapes=[pltpu.VMEM((d, de), BF16), pltpu.VMEM((d, de), BF16), pltpu.VMEM((de, d), BF16)]),
        compiler_params=_cparams(("arbitrary",)),
        name="expert_mlp",
    )(tile_expert, n_valid, xs, wg, wu, wd)


def _combine_kernel(y_ref, route_ref, x_ref, g_ref, b_ref, o_ref, *, alpha):
    half = y_ref.shape[1] // 2
    a_lo, a_hi = _unpack_halves(y_ref[:, :half])
    b_lo, b_hi = _unpack_halves(y_ref[:, half:])
    c1 = route_ref[:, 2:3]
    c2 = route_ref[:, 3:4]
    ff = jnp.concatenate([c1 * a_lo + c2 * b_lo, c1 * a_hi + c2 * b_hi], axis=1)
    o_ref[...] = _layer_norm_rows(alpha * x_ref[...] + ff, g_ref[...], b_ref[...])


def _combine(yg, route, x1, g2, b2, alpha, tm, row0=0, n_rows=None):
    d = x1.shape[1]
    n_rows = x1.shape[0] if n_rows is None else n_rows
    blk0 = row0 // tm
    row = lambda w: pl.BlockSpec((tm, w), lambda i: (i + blk0, 0))
    vec = pl.BlockSpec((1, d), lambda i: (0, 0))
    return pl.pallas_call(
        functools.partial(_combine_kernel, alpha=alpha),
        out_shape=jax.ShapeDtypeStruct((n_rows, d), F32),
        grid=(n_rows // tm,),
        in_specs=[row(yg.shape[1]), row(ROUTE_LANES), row(d), vec, vec],
        out_specs=pl.BlockSpec((tm, d), lambda i: (i, 0)),
        compiler_params=_cparams(("parallel",)),
        name="combine_ln2",
    )(yg, route, x1, g2.reshape(1, d), b2.reshape(1, d))


def _rope_tables(seq):
    half = QK_DIM // 2
    inv_freq = 1.0 / (ROPE_BASE ** (jnp.arange(half, dtype=F32) / half))
    ang = jnp.arange(seq, dtype=F32)[:, None] * inv_freq[None, :]
    cos, sin = jnp.cos(ang), jnp.sin(ang)
    return jnp.concatenate([cos, cos], axis=1), jnp.concatenate([-sin, sin], axis=1)


def _channel_dft_matrix():
    n = np.arange(FG_DIM)
    ang = 2.0 * np.pi * np.outer(n, n) / FG_DIM
    eye = np.eye(N_FGROUPS)
    bd = np.concatenate([np.kron(eye, np.cos(ang)), np.kron(eye, np.sin(ang))], axis=1) / math.sqrt(FG_DIM)
    return jnp.asarray(bd, dtype=BF16)


def _seq_dft_matrices(seq):
    blk = 64 if seq % 64 == 0 else 1
    k = jnp.arange(seq, dtype=jnp.int32)[:, None]
    hi = jnp.arange(seq // blk, dtype=jnp.int32)[None, :] * blk
    lo = jnp.arange(blk, dtype=jnp.int32)[None, :]
    w = 2.0 * math.pi / seq
    a_hi = ((k * hi) % seq).astype(F32) * w
    a_lo = ((k * lo) % seq).astype(F32) * w
    ch, sh = jnp.cos(a_hi)[:, :, None], jnp.sin(a_hi)[:, :, None]
    cl, sl = jnp.cos(a_lo)[:, None, :], jnp.sin(a_lo)[:, None, :]
    scale = seq ** -0.5
    cmat = ((ch * cl - sh * sl) * scale).reshape(seq, seq).astype(BF16)
    smat = ((sh * cl + ch * sl) * scale).reshape(seq, seq).astype(BF16)
    return cmat, smat


def _pad_heads(w, axis):
    shp = list(w.shape)
    shp[axis:axis + 1] = [N_HEADS, V_DIM]
    w = w.reshape(shp)
    pad = [(0, 0)] * w.ndim
    pad[axis + 1] = (0, V_PAD - V_DIM)
    w = jnp.pad(w, pad)
    shp[axis:axis + 2] = [N_HEADS * V_PAD]
    return w.reshape(shp)


def _pad_in_weight(w_in):
    qk_w = 2 * N_HEADS * QK_DIM
    vw = N_HEADS * V_DIM
    qk = w_in[:, :qk_w]
    v = _pad_heads(w_in[:, qk_w:qk_w + vw], 1)
    g = _pad_heads(w_in[:, qk_w + vw:qk_w + 2 * vw], 1)
    rest = w_in[:, qk_w + 2 * vw:]
    return jnp.concatenate([qk, v, g, rest], axis=1).astype(BF16)


def _routing_plan(route, tile_counts, tm_tok, n_tiles):
    t = route.shape[0]
    experts = jnp.arange(N_EXPERTS, dtype=jnp.int32)
    cnt = tile_counts[:, 0, :N_EXPERTS].astype(jnp.int32)
    before = jnp.cumsum(cnt, axis=0) - cnt
    counts = jnp.sum(cnt, axis=0)
    tiles_per = (counts + MOE_TILE - 1) // MOE_TILE
    tile_end = jnp.cumsum(tiles_per)
    start_row = (tile_end - tiles_per) * MOE_TILE
    n_valid = tile_end[-1]

    base = (start_row[None, :] + before).astype(F32)
    base_tok = jnp.broadcast_to(base[:, None, :], (t // tm_tok, tm_tok, N_EXPERTS)).reshape(t, N_EXPERTS)
    e = route[:, 0:2]
    rank = route[:, 4:6]
    sel = e[:, :, None] == experts.astype(F32)[None, None, :]
    pos = (jnp.sum(jnp.where(sel, base_tok[:, None, :], 0.0), axis=-1) + rank).astype(jnp.int32)

    n_pad = n_tiles * MOE_TILE - 2 * t
    pads = tiles_per * MOE_TILE - counts
    pad_end = jnp.cumsum(pads)
    p = jnp.arange(n_pad, dtype=jnp.int32)
    pe = jnp.minimum(jnp.sum((p[:, None] >= pad_end[None, :]).astype(jnp.int32), axis=1), N_EXPERTS - 1)
    in_expert = start_row[pe] + counts[pe] + (p - (pad_end[pe] - pads[pe]))
    pad_pos = jnp.where(p < pad_end[-1], in_expert, n_valid * MOE_TILE + (p - pad_end[-1]))

    tile_ids = jnp.arange(n_tiles, dtype=jnp.int32)
    tile_expert = jnp.minimum(jnp.sum((tile_ids[:, None] >= tile_end[None, :]).astype(jnp.int32), axis=1),
                              N_EXPERTS - 1)
    last_e = tile_expert[jnp.maximum(n_valid - 1, 0)]
    tile_expert = jnp.where(tile_ids < n_valid, tile_expert, last_e).astype(jnp.int32)
    return pos, pad_pos.astype(jnp.int32), tile_expert, n_valid.reshape(1).astype(jnp.int32)


def _largest_tile(cands, *dims):
    for c in cands:
        if all(d % c == 0 for d in dims):
            return c
    raise ValueError(f"no tile in {cands} divides {dims}")


def kernel(x_prompt, x_sample, ln_in_g, ln_in_b, w_in, ret_decay_fwd, ret_decay_bwd, ret_gn_g, ret_gn_b, w_ret_up, w_four_up, w_out, ln1_g, ln1_b, w_route_group, b_route_group, w_route_expert, b_route_expert, w_expert_gate, w_expert_up, w_expert_down, ln2_g, ln2_b):
    bp, sp, d = x_prompt.shape
    bs, ss, _ = x_sample.shape
    depth = w_in.shape[0]
    tp, ts = bp * sp, bs * ss
    t = tp + ts
    assert sp % CHUNK == 0 and ss % CHUNK == 0
    assert tp % ss == 0, "sample sequences must start on a sequence-sized row block"
    alpha = float((2 * depth) ** 0.25)

    tm_in = _largest_tile((512, 256, 128), sp, ss)
    tm_tok = _largest_tile((TOKEN_TILE, 128), tp, ts)
    n_tiles = (2 * t) // MOE_TILE + N_EXPERTS
    sc_rows = SC_GATHER_WINDOW // SC_ROW_SPLIT
    assert (2 * t) % MOE_TILE == 0 and t % sc_rows == 0

    cos_t, sin_t = _rope_tables(max(sp, ss))
    bd = _channel_dft_matrix()
    dft = {s: _seq_dft_matrices(s) for s in sorted({sp, ss})}

    x = _layer_norm_concat(x_prompt.reshape(tp, d), x_sample.reshape(ts, d), ln_in_g, ln_in_b, tm_tok)

    for l in range(depth):
        w_pad = _pad_in_weight(w_in[l])
        dec = jnp.stack([ret_decay_fwd[l], ret_decay_bwd[l]]).astype(F32)
        gn_g = _pad_heads(ret_gn_g[l], 0).reshape(1, -1).astype(F32)
        gn_b = _pad_heads(ret_gn_b[l], 0).reshape(1, -1).astype(F32)
        wr = _pad_heads(w_ret_up[l], 0).astype(BF16)
        wf = w_four_up[l].astype(BF16)
        wo = w_out[l].astype(BF16)
        w_rt = jnp.concatenate([w_route_group[l], w_route_expert[l]], axis=1)
        w_rt = jnp.pad(w_rt, ((0, 0), (0, ROUTE_LANES // 2 - w_rt.shape[1])))
        wrh = w_rt.astype(BF16)
        wr2 = jnp.concatenate([wrh, (w_rt - wrh.astype(F32)).astype(BF16)], axis=1)
        rbias = jnp.concatenate([b_route_group[l], b_route_expert[l]])
        rbias = jnp.pad(rbias, (0, ROUTE_LANES - rbias.shape[0])).reshape(1, ROUTE_LANES).astype(F32)

        qk, v, sg, z, gates = _inproj(x, w_pad, cos_t, sin_t, bd, tm_in,
                                      tp // tm_in, sp // tm_in, ss // tm_in)

        r_p = _retention(dec, qk, v, sg, gn_g, gn_b, bp, sp, 0)
        r_s = _retention(dec, qk, v, sg, gn_g, gn_b, bs, ss, tp // ss)

        fo_parts = []
        for (nb, s, r0) in ((bp, sp, 0), (bs, ss, tp)):
            zz = z[r0:r0 + nb * s].reshape(nb, s, 2, F_WIDTH).transpose(1, 2, 0, 3).reshape(s, 2 * nb * F_WIDTH)
            fo = _seq_dft(dft[s][0], dft[s][1], zz)
            fo_parts.append(fo.reshape(s, nb, F_WIDTH).transpose(1, 0, 2).reshape(nb * s, F_WIDTH))

        x1, x1p, route, tile_counts = _merge(r_p, r_s, fo_parts[0], fo_parts[1], gates, x, wr, wf, wo,
                                             ln1_g[l], ln1_b[l], wr2, rbias, alpha, tm_tok)

        pos, pad_pos, tile_expert, n_valid = _routing_plan(route, tile_counts, tm_tok, n_tiles)
        xs = _sc_scatter_rows(x1p, jnp.concatenate([pos[:, 0], pos[:, 1], pad_pos]), n_tiles * MOE_TILE)
        ys = _expert_mlp(tile_expert, n_valid, xs, w_expert_gate[l], w_expert_up[l], w_expert_down[l])
        yg = _sc_gather_rows(ys, pos.reshape(-1)).reshape(t, d)
        if l + 1 < depth:
            x = _combine(yg, route, x1, ln2_g[l], ln2_b[l], alpha, tm_tok)
        else:
            y_p = _combine(yg, route, x1, ln2_g[l], ln2_b[l], alpha, tm_tok, 0, tp)
            y_s = _combine(yg, route, x1, ln2_g[l], ln2_b[l], alpha, tm_tok, tp, ts)

    return y_p.reshape(bp, sp, d), y_s.reshape(bs, ss, d)
```

```python
import functools
import math

import jax
import jax.numpy as jnp
import numpy as np
from jax import lax
from jax.experimental import pallas as pl
from jax.experimental.pallas import tpu as pltpu
from jax.experimental.pallas import tpu_sc as plsc

N_HEADS = 4
QK_DIM = 128
V_DIM = 192
V_PAD = 256
CHUNK = 128
ROPE_BASE = 10000.0
N_FGROUPS = 4
FG_DIM = 64
F_WIDTH = N_FGROUPS * FG_DIM
N_GROUPS = 4
EXPERTS_PER_GROUP = 8
N_EXPERTS = N_GROUPS * EXPERTS_PER_GROUP
LN_EPS = 1e-5
ROUTE_LANES = 128

LANE = 128
VMEM_LIMIT_BYTES = 56 * 1024 * 1024
MOE_TILE = 512
GATHER_TILE = 256
SC_GATHER_WINDOW = 128
N_PIECES = 2
PIECE_W = 256
TOKEN_TILE = 256
MERGE_TILE = 512
RANK_TILE = 256
RET_UNROLL = 8

F32 = jnp.float32
BF16 = jnp.bfloat16


def _cparams(sem):
    return pltpu.CompilerParams(dimension_semantics=sem, vmem_limit_bytes=VMEM_LIMIT_BYTES)


def _layer_norm_rows(y, g, b):
    mu = jnp.mean(y, axis=-1, keepdims=True)
    d = y - mu
    var = jnp.mean(d * d, axis=-1, keepdims=True)
    return d * lax.rsqrt(var + LN_EPS) * g + b


def _pack_halves(y):
    n = y.shape[1] // 2
    lo = pltpu.bitcast(y[:, :n].astype(BF16).astype(F32), jnp.uint32)
    hi = pltpu.bitcast(y[:, n:].astype(BF16).astype(F32), jnp.uint32)
    return (hi & jnp.uint32(0xFFFF0000)) | (lo >> 16)


def _unpack_halves(w):
    lo = pltpu.bitcast(w << 16, F32)
    hi = pltpu.bitcast(w & jnp.uint32(0xFFFF0000), F32)
    return lo, hi


def _store_pieces(ref, packed):
    for s in range(N_PIECES):
        ref[s] = packed[:, s * PIECE_W:(s + 1) * PIECE_W]


def _unpack_pieces(pieces):
    halves = [_unpack_halves(p) for p in pieces]
    return [h[0] for h in halves] + [h[1] for h in halves]


def _ln_kernel(xp_ref, xs_ref, g_ref, b_ref, o_ref, *, n_p_tiles):
    i = pl.program_id(0)

    @pl.when(i < n_p_tiles)
    def _():
        o_ref[...] = _layer_norm_rows(xp_ref[...], g_ref[...], b_ref[...])

    @pl.when(i >= n_p_tiles)
    def _():
        o_ref[...] = _layer_norm_rows(xs_ref[...], g_ref[...], b_ref[...])


def _layer_norm_concat(xp, xs, g, b, tm):
    (tp, d), ts = xp.shape, xs.shape[0]
    n_p, n_s = tp // tm, ts // tm
    return pl.pallas_call(
        functools.partial(_ln_kernel, n_p_tiles=n_p),
        out_shape=jax.ShapeDtypeStruct((tp + ts, d), F32),
        grid=(n_p + n_s,),
        in_specs=[pl.BlockSpec((tm, d), lambda i: (jnp.minimum(i, n_p - 1), 0)),
                  pl.BlockSpec((tm, d), lambda i: (jnp.maximum(i - n_p, 0), 0)),
                  pl.BlockSpec((1, d), lambda i: (0, 0)),
                  pl.BlockSpec((1, d), lambda i: (0, 0))],
        out_specs=pl.BlockSpec((tm, d), lambda i: (i, 0)),
        compiler_params=_cparams(("arbitrary",)),
        name="ln_in",
    )(xp, xs, g.reshape(1, d), b.reshape(1, d))


def _inproj_kernel(x_ref, w_ref, cos_ref, sin_ref, bd_ref,
                   qk_ref, v_ref, sg_ref, z_ref, gates_ref, *, d_model):
    x = x_ref[...].astype(BF16)
    cos = cos_ref[...]
    sin = sin_ref[...]
    qk_w = 2 * N_HEADS * QK_DIM
    v_w = N_HEADS * V_PAD
    off_v = qk_w
    off_g = off_v + v_w
    off_f = off_g + v_w
    off_gr = off_f + F_WIDTH
    cw = 256

    def proj(c0):
        return jnp.dot(x, w_ref[:, c0:c0 + cw], preferred_element_type=F32)

    for c0 in range(0, qk_w, cw):
        acc = proj(c0)
        scale = QK_DIM ** -0.5 if c0 < N_HEADS * QK_DIM else 1.0
        parts = []
        for hh in range(cw // QK_DIM):
            a = acc[:, hh * QK_DIM:(hh + 1) * QK_DIM]
            rot = a * cos + pltpu.roll(a, QK_DIM // 2, axis=1) * sin
            if scale != 1.0:
                rot = rot * scale
            parts.append(rot.astype(BF16))
        qk_ref[:, c0:c0 + cw] = jnp.concatenate(parts, axis=1)
    for c0 in range(0, v_w, cw):
        v_ref[:, c0:c0 + cw] = proj(off_v + c0).astype(BF16)
    for c0 in range(0, v_w, cw):
        a = proj(off_g + c0)
        sg_ref[:, c0:c0 + cw] = (a * jax.nn.sigmoid(a)).astype(BF16)
    f = proj(off_f).astype(BF16)
    z_ref[...] = jnp.dot(f, bd_ref[...], preferred_element_type=F32).astype(BF16)
    for c0 in range(0, 2 * d_model, cw):
        gates_ref[:, c0:c0 + cw] = jax.nn.sigmoid(proj(off_gr + c0)).astype(BF16)


def _inproj(x, w_pad, cos_t, sin_t, bd, tm, n_p_tiles, sp_tiles, ss_tiles):
    t, d = x.shape
    n_cols = w_pad.shape[1]

    def pos_map(i):
        return (jnp.where(i < n_p_tiles, i % sp_tiles, (i - n_p_tiles) % ss_tiles), 0)

    outs = (jax.ShapeDtypeStruct((t, 2 * N_HEADS * QK_DIM), BF16),
            jax.ShapeDtypeStruct((t, N_HEADS * V_PAD), BF16),
            jax.ShapeDtypeStruct((t, N_HEADS * V_PAD), BF16),
            jax.ShapeDtypeStruct((t, 2 * F_WIDTH), BF16),
            jax.ShapeDtypeStruct((t, 2 * d), BF16))
    row = lambda w: pl.BlockSpec((tm, w), lambda i: (i, 0))
    return pl.pallas_call(
        functools.partial(_inproj_kernel, d_model=d),
        out_shape=outs,
        grid=(t // tm,),
        in_specs=[row(d),
                  pl.BlockSpec((d, n_cols), lambda i: (0, 0)),
                  pl.BlockSpec((tm, QK_DIM), pos_map),
                  pl.BlockSpec((tm, QK_DIM), pos_map),
                  pl.BlockSpec((F_WIDTH, 2 * F_WIDTH), lambda i: (0, 0))],
        out_specs=tuple(row(o.shape[1]) for o in outs),
        compiler_params=_cparams(("parallel",)),
        name="inproj",
    )(x, w_pad, cos_t, sin_t, bd)


def _log_sigmoid(x):
    return jnp.minimum(x, 0.0) - jnp.log1p(jnp.exp(-jnp.abs(x)))


def _retention_kernel(dec_ref, q_ref, k_ref, v_ref, sg_ref, gng_ref, gnb_ref, o_ref,
                      r_all, rf_s, rb_s, *, n_chunks):
    h = pl.program_id(1)
    c = CHUNK
    lf = _log_sigmoid(jnp.full((1, c), dec_ref[0, h], F32))
    lb = _log_sigmoid(jnp.full((1, c), dec_ref[1, h], F32))
    lf_w = _log_sigmoid(jnp.full((1, V_PAD), dec_ref[0, h], F32))
    lb_w = _log_sigmoid(jnp.full((1, V_PAD), dec_ref[1, h], F32))
    row = lax.broadcasted_iota(jnp.int32, (c, c), 0).astype(F32)
    col = lax.broadcasted_iota(jnp.int32, (c, c), 1).astype(F32)
    diff = row - col
    dmat = jnp.where(diff >= 0.0, jnp.exp(lf * jnp.maximum(diff, 0.0)),
                     jnp.exp(lb * jnp.maximum(-diff, 0.0)))
    qd_f = jnp.exp(lf * (row + 1.0))
    kd_f = jnp.exp(lf * (c - 1.0 - row))
    qd_b = jnp.exp(lb * (c - row))
    kd_b = jnp.exp(lb * row)
    cf = jnp.exp(lf_w * float(c))
    cb = jnp.exp(lb_w * float(c))
    lane = lax.broadcasted_iota(jnp.int32, (c, V_PAD), 1)
    vmask = (lane < V_DIM).astype(F32)
    gn_g = gng_ref[...]
    gn_b = gnb_ref[...]
    tdot = (((0,), (0,)), ((), ()))

    rf_s[...] = jnp.zeros_like(rf_s)
    rb_s[...] = jnp.zeros_like(rb_s)

    def state_step(t, carry):
        for (ci, kd, cdec, r_s, half) in ((t, kd_f, cf, rf_s, 0), (n_chunks - 1 - t, kd_b, cb, rb_s, 1)):
            r0 = pl.multiple_of(ci * c, c)
            kk = (k_ref[pl.ds(r0, c), :].astype(F32) * kd).astype(BF16)
            kv = lax.dot_general(kk, v_ref[pl.ds(r0, c), :], tdot, preferred_element_type=F32)
            r = r_s[...]
            r_all[ci, half * c:(half + 1) * c, :] = r.astype(BF16)
            r_s[...] = cdec * r + kv
        return carry

    lax.fori_loop(0, n_chunks, state_step, 0, unroll=RET_UNROLL)

    def out_step(ci, carry):
        r0 = pl.multiple_of(ci * c, c)
        q = q_ref[pl.ds(r0, c), :]
        qf = q.astype(F32)
        s = lax.dot_general(q, k_ref[pl.ds(r0, c), :], (((1,), (1,)), ((), ())), preferred_element_type=F32)
        p = (s * dmat).astype(BF16)
        o = jnp.dot(p, v_ref[pl.ds(r0, c), :], preferred_element_type=F32)
        qq = jnp.concatenate([(qf * qd_f).astype(BF16), (qf * qd_b).astype(BF16)], axis=1)
        o = o + jnp.dot(qq, r_all[ci], preferred_element_type=F32)
        mu = jnp.sum(o, axis=-1, keepdims=True) * (1.0 / V_DIM)
        d = (o - mu) * vmask
        var = jnp.sum(d * d, axis=-1, keepdims=True) * (1.0 / V_DIM)
        y = d * lax.rsqrt(var + LN_EPS) * gn_g + gn_b
        o_ref[pl.ds(r0, c), :] = (y * sg_ref[pl.ds(r0, c), :].astype(F32)).astype(BF16)
        return carry

    lax.fori_loop(0, n_chunks, out_step, 0, unroll=RET_UNROLL)


def _retention(dec, qk, v, sg, gn_g, gn_b, n_batch, seq, row_block0):
    n_chunks = seq // CHUNK
    rb = lambda b: b + row_block0
    return pl.pallas_call(
        functools.partial(_retention_kernel, n_chunks=n_chunks),
        out_shape=jax.ShapeDtypeStruct((n_batch * seq, N_HEADS * V_PAD), BF16),
        grid_spec=pltpu.PrefetchScalarGridSpec(
            num_scalar_prefetch=0,
            grid=(n_batch, N_HEADS),
            in_specs=[pl.BlockSpec(memory_space=pltpu.SMEM),
                      pl.BlockSpec((seq, QK_DIM), lambda b, h: (rb(b), h)),
                      pl.BlockSpec((seq, QK_DIM), lambda b, h: (rb(b), N_HEADS + h)),
                      pl.BlockSpec((seq, V_PAD), lambda b, h: (rb(b), h)),
                      pl.BlockSpec((seq, V_PAD), lambda b, h: (rb(b), h)),
                      pl.BlockSpec((1, V_PAD), lambda b, h: (0, h)),
                      pl.BlockSpec((1, V_PAD), lambda b, h: (0, h))],
            out_specs=pl.BlockSpec((seq, V_PAD), lambda b, h: (b, h)),
            scratch_shapes=[pltpu.VMEM((n_chunks, 2 * CHUNK, V_PAD), BF16),
                            pltpu.VMEM((CHUNK, V_PAD), F32),
                            pltpu.VMEM((CHUNK, V_PAD), F32)]),
        compiler_params=_cparams(("parallel", "arbitrary")),
        name="retention",
    )(dec, qk, qk, v, sg, gn_g, gn_b)


def _seqdft_kernel(c_ref, s_ref, zc_ref, zs_ref, o_ref, acc_ref):
    kk = pl.program_id(1)
    b = pl.program_id(2)
    part = (jnp.dot(c_ref[...], zc_ref[...], preferred_element_type=F32)
            - jnp.dot(s_ref[...], zs_ref[...], preferred_element_type=F32))

    @pl.when(kk == 0)
    def _():
        acc_ref[b] = part

    @pl.when(kk > 0)
    def _():
        acc_ref[b] += part

    @pl.when(kk == pl.num_programs(1) - 1)
    def _():
        o_ref[b] = acc_ref[b].astype(o_ref.dtype)


def _seq_dft(cmat, smat, z, n_batch, seq, row_block0):
    tm = min(1024, seq)
    tk = min(1024, seq)
    z3 = z.reshape(z.shape[0] // seq, seq, z.shape[1])
    out = pl.pallas_call(
        _seqdft_kernel,
        out_shape=jax.ShapeDtypeStruct((n_batch, seq, F_WIDTH), BF16),
        grid=(seq // tm, seq // tk, n_batch),
        in_specs=[pl.BlockSpec((tm, tk), lambda i, k, b: (i, k)),
                  pl.BlockSpec((tm, tk), lambda i, k, b: (i, k)),
                  pl.BlockSpec((None, tk, F_WIDTH), lambda i, k, b: (b + row_block0, k, 0)),
                  pl.BlockSpec((None, tk, F_WIDTH), lambda i, k, b: (b + row_block0, k, 1))],
        out_specs=pl.BlockSpec((n_batch, tm, F_WIDTH), lambda i, k, b: (0, i, 0)),
        scratch_shapes=[pltpu.VMEM((n_batch, tm, F_WIDTH), F32)],
        compiler_params=_cparams(("parallel", "arbitrary", "arbitrary")),
        name="seq_dft",
    )(cmat, smat, z3, z3)
    return out.reshape(n_batch * seq, F_WIDTH)


def _merge_kernel(rp_ref, rs_ref, fop_ref, fos_ref, gates_ref, x_ref, wr_ref, wf_ref, wo_ref, g1_ref, b1_ref,
                  wr2_ref, rb_ref, x1_ref, x1p_ref, route_ref, cnt_ref, *, alpha, d_model, n_p_tiles):
    is_p = pl.program_id(0) < n_p_tiles
    for j in range(x_ref.shape[0] // RANK_TILE):
        rows = pl.ds(j * RANK_TILE, RANK_TILE)
        _merge_subtile(is_p, rows, j, rp_ref, rs_ref, fop_ref, fos_ref, gates_ref, x_ref, wr_ref, wf_ref, wo_ref,
                       g1_ref, b1_ref, wr2_ref, rb_ref, x1_ref, x1p_ref, route_ref, cnt_ref,
                       alpha=alpha, d_model=d_model)


def _merge_subtile(is_p, rows, j, rp_ref, rs_ref, fop_ref, fos_ref, gates_ref, x_ref, wr_ref, wf_ref, wo_ref,
                   g1_ref, b1_ref, wr2_ref, rb_ref, x1_ref, x1p_ref, route_ref, cnt_ref, *, alpha, d_model):
    r = jnp.where(is_p, rp_ref[rows, :], rs_ref[rows, :])
    fo = jnp.where(is_p, fop_ref[rows, :], fos_ref[rows, :])
    a = jnp.dot(r, wr_ref[...], preferred_element_type=F32)
    b = jnp.dot(fo, wf_ref[...], preferred_element_type=F32)
    merged = (gates_ref[rows, :d_model].astype(F32) * a + gates_ref[rows, d_model:].astype(F32) * b)
    mix = jnp.dot(merged.astype(BF16), wo_ref[...], preferred_element_type=F32)
    x1 = _layer_norm_rows(alpha * x_ref[rows, :] + mix, g1_ref[...], b1_ref[...])
    x1_ref[rows, :] = x1
    packed = _pack_halves(x1)
    for s in range(N_PIECES):
        x1p_ref[s, rows, :] = packed[:, s * PIECE_W:(s + 1) * PIECE_W]

    xh = x1.astype(BF16)
    xl = (x1 - xh.astype(F32)).astype(BF16)
    s2 = (jnp.dot(xh, wr2_ref[...], preferred_element_type=F32)
          + jnp.dot(xl, wr2_ref[...], preferred_element_type=F32))
    logits = s2 + pltpu.roll(s2, ROUTE_LANES // 2, axis=1) + rb_ref[...]
    tm = logits.shape[0]
    lane = lax.broadcasted_iota(jnp.int32, (tm, ROUTE_LANES), 1)
    neg = jnp.float32(-jnp.inf)
    big = jnp.int32(ROUTE_LANES)

    def top1(mask):
        vals = jnp.where(mask, logits, neg)
        m = jnp.max(vals, axis=-1, keepdims=True)
        idx = jnp.min(jnp.where(mask & (vals == m), lane, big), axis=-1, keepdims=True)
        return m, idx

    gmask = lane < N_GROUPS
    gm, gi = top1(gmask)
    gsum = jnp.sum(jnp.where(gmask, jnp.exp(logits - gm), 0.0), axis=-1, keepdims=True)
    g_gate = 1.0 / gsum
    lo = N_GROUPS + gi * EXPERTS_PER_GROUP
    emask = (lane >= lo) & (lane < lo + EXPERTS_PER_GROUP)
    m1, i1 = top1(emask)
    m2, i2 = top1(emask & (lane != i1))
    e2 = jnp.exp(m2 - m1)
    w1 = 1.0 / (1.0 + e2)
    w2 = e2 / (1.0 + e2)
    oh1 = (lane == i1 - N_GROUPS).astype(F32)
    oh2 = (lane == i2 - N_GROUPS).astype(F32)
    earlier = (lax.broadcasted_iota(jnp.int32, (tm, tm), 1)
               < lax.broadcasted_iota(jnp.int32, (tm, tm), 0)).astype(BF16)
    tot1 = jnp.sum(oh1, axis=0, keepdims=True)
    tot2 = jnp.sum(oh2, axis=0, keepdims=True)
    pre1 = jnp.dot(earlier, oh1.astype(BF16), preferred_element_type=F32)
    pre2 = jnp.dot(earlier, oh2.astype(BF16), preferred_element_type=F32) + tot1
    r1 = jnp.sum(pre1 * oh1, axis=-1, keepdims=True)
    r2 = jnp.sum(pre2 * oh2, axis=-1, keepdims=True)
    cnt_ref[j] = tot1 + tot2

    rec = jnp.where(lane == 0, (i1 - N_GROUPS).astype(F32), 0.0)
    rec = jnp.where(lane == 1, (i2 - N_GROUPS).astype(F32), rec)
    rec = jnp.where(lane == 2, g_gate * w1, rec)
    rec = jnp.where(lane == 3, g_gate * w2, rec)
    rec = jnp.where(lane == 4, r1, rec)
    rec = jnp.where(lane == 5, r2, rec)
    route_ref[rows, :] = rec


def _merge(rp, rs, fop, fos, gates, x, wr, wf, wo, g1, b1, wr2, rbias, alpha, tm):
    t, d = x.shape
    n_p = rp.shape[0] // tm
    row = lambda w: pl.BlockSpec((tm, w), lambda i: (i, 0))
    row_p = lambda w: pl.BlockSpec((tm, w), lambda i: (jnp.minimum(i, n_p - 1), 0))
    row_s = lambda w: pl.BlockSpec((tm, w), lambda i: (jnp.maximum(i - n_p, 0), 0))
    full = lambda a: pl.BlockSpec(a.shape, lambda i: (0, 0))
    g1 = g1.reshape(1, d)
    b1 = b1.reshape(1, d)
    return pl.pallas_call(
        functools.partial(_merge_kernel, alpha=alpha, d_model=d, n_p_tiles=n_p),
        out_shape=(jax.ShapeDtypeStruct((t, d), F32),
                   jax.ShapeDtypeStruct((N_PIECES, t, PIECE_W), jnp.uint32),
                   jax.ShapeDtypeStruct((t, ROUTE_LANES), F32),
                   jax.ShapeDtypeStruct((t // RANK_TILE, 1, ROUTE_LANES), F32)),
        grid=(t // tm,),
        in_specs=[row_p(rp.shape[1]), row_s(rs.shape[1]), row_p(fop.shape[1]), row_s(fos.shape[1]),
                  row(gates.shape[1]), row(d),
                  full(wr), full(wf), full(wo), full(g1), full(b1), full(wr2), full(rbias)],
        out_specs=(row(d), pl.BlockSpec((N_PIECES, tm, PIECE_W), lambda i: (0, i, 0)), row(ROUTE_LANES),
                   pl.BlockSpec((tm // RANK_TILE, 1, ROUTE_LANES), lambda i: (i, 0, 0))),
        compiler_params=_cparams(("arbitrary",)),
        name="merge_router",
    )(rp, rs, fop, fos, gates, x, wr, wf, wo, g1, b1, wr2, rbias)


def _sc_mesh():
    return plsc.VectorSubcoreMesh(core_axis_name="c", subcore_axis_name="s")


def _sc_gather_rows(table, idx):
    n, w = idx.shape[0], table.shape[1]
    idx = idx.reshape(1, n)

    @functools.partial(pl.kernel, out_type=jax.ShapeDtypeStruct((n, w), table.dtype), mesh=_sc_mesh(),
                       scratch_types=[])
    def gather(tab_hbm, idx_hbm, out_hbm):
        def body(idx_vmem, out_vmem):
            pltpu.sync_copy(tab_hbm.at[idx_vmem.at[0]], out_vmem)

        pltpu.emit_pipeline(
            body,
            grid=(n // SC_GATHER_WINDOW,),
            in_specs=[pl.BlockSpec((1, SC_GATHER_WINDOW), lambda i: (0, i))],
            out_specs=[pl.BlockSpec((SC_GATHER_WINDOW, w), lambda i: (i, 0))],
            core_axis_name=("c", "s"),
            dimension_semantics=(pltpu.PARALLEL,),
        )(idx_hbm, out_hbm)

    return gather(table, idx)


def _sc_scatter_rows(src, idx, n_out):
    assert idx.shape[0] == n_out
    n, w = n_out, src.shape[1]
    idx = idx.reshape(1, n)
    src_windows = src.shape[0] // SC_GATHER_WINDOW

    @functools.partial(pl.kernel, out_type=jax.ShapeDtypeStruct((n, w), src.dtype), mesh=_sc_mesh(),
                       scratch_types=[])
    def scatter(src_hbm, idx_hbm, out_hbm):
        def body(src_vmem, idx_vmem):
            pltpu.sync_copy(src_vmem, out_hbm.at[idx_vmem.at[0]])

        pltpu.emit_pipeline(
            body,
            grid=(n // SC_GATHER_WINDOW,),
            in_specs=[pl.BlockSpec((SC_GATHER_WINDOW, w), lambda i: (i % src_windows, 0)),
                      pl.BlockSpec((1, SC_GATHER_WINDOW), lambda i: (0, i))],
            out_specs=[],
            core_axis_name=("c", "s"),
            dimension_semantics=(pltpu.PARALLEL,),
        )(src_hbm, idx_hbm)

    return scatter(src, idx)


def _expert_kernel(te_ref, nv_ref, xs_ref, wg_ref, wu_ref, wd_ref, o_ref, wg_s, wu_s, wd_s):
    i = pl.program_id(0)

    @pl.when((i == 0) | (te_ref[i] != te_ref[jnp.maximum(i - 1, 0)]))
    def _():
        wg_s[...] = wg_ref[...].astype(BF16)
        wu_s[...] = wu_ref[...].astype(BF16)
        wd_s[...] = wd_ref[...].astype(BF16)

    @pl.when(i < nv_ref[0])
    def _():
        chunks = [c.astype(BF16) for c in _unpack_pieces([xs_ref[s] for s in range(N_PIECES)])]
        cw = chunks[0].shape[1]

        def up(w_s):
            acc = jnp.dot(chunks[0], w_s[:cw, :], preferred_element_type=F32)
            for j in range(1, len(chunks)):
                acc = acc + jnp.dot(chunks[j], w_s[j * cw:(j + 1) * cw, :], preferred_element_type=F32)
            return acc

        g = up(wg_s)
        hmid = (g * jax.nn.sigmoid(g) * up(wu_s)).astype(BF16)
        _store_pieces(o_ref, _pack_halves(jnp.dot(hmid, wd_s[...], preferred_element_type=F32)))

    @pl.when(i >= nv_ref[0])
    def _():
        o_ref[...] = jnp.zeros_like(o_ref)


def _expert_mlp(tile_expert, n_valid, xs, wg, wu, wd):
    p = xs.shape[1]
    tm = MOE_TILE
    d, de = wg.shape[1], wg.shape[2]
    xmap = lambda i, te, nv: (0, jnp.minimum(i, nv[0] - 1), 0)
    wmap = lambda i, te, nv: (te[i], 0, 0)
    return pl.pallas_call(
        _expert_kernel,
        out_shape=jax.ShapeDtypeStruct(xs.shape, jnp.uint32),
        grid_spec=pltpu.PrefetchScalarGridSpec(
            num_scalar_prefetch=2,
            grid=(p // tm,),
            in_specs=[pl.BlockSpec((N_PIECES, tm, PIECE_W), xmap),
                      pl.BlockSpec((None, d, de), wmap),
                      pl.BlockSpec((None, d, de), wmap),
                      pl.BlockSpec((None, de, d), wmap)],
            out_specs=pl.BlockSpec((N_PIECES, tm, PIECE_W), lambda i, te, nv: (0, i, 0)),
            scratch_shapes=[pltpu.VMEM((d, de), BF16), pltpu.VMEM((d, de), BF16), pltpu.VMEM((de, d), BF16)]),
        compiler_params=_cparams(("arbitrary",)),
        name="expert_mlp",
    )(tile_expert, n_valid, xs, wg, wu, wd)


def _combine_kernel(y_ref, route_ref, x_ref, g_ref, b_ref, o_ref, *, alpha):
    c1 = route_ref[:, 2:3]
    c2 = route_ref[:, 3:4]
    first = _unpack_pieces([y_ref[s, 0] for s in range(N_PIECES)])
    second = _unpack_pieces([y_ref[s, 1] for s in range(N_PIECES)])
    ff = jnp.concatenate([c1 * a + c2 * b for a, b in zip(first, second)], axis=1)
    o_ref[...] = _layer_norm_rows(alpha * x_ref[...] + ff, g_ref[...], b_ref[...])


def _combine(yg, route, x1, g2, b2, alpha, tm, row0=0, n_rows=None):
    d = x1.shape[1]
    n_rows = x1.shape[0] if n_rows is None else n_rows
    blk0 = row0 // tm
    row = lambda w: pl.BlockSpec((tm, w), lambda i: (i + blk0, 0))
    vec = pl.BlockSpec((1, d), lambda i: (0, 0))
    return pl.pallas_call(
        functools.partial(_combine_kernel, alpha=alpha),
        out_shape=jax.ShapeDtypeStruct((n_rows, d), F32),
        grid=(n_rows // tm,),
        in_specs=[pl.BlockSpec((N_PIECES, 2, tm, PIECE_W), lambda i: (0, 0, i + blk0, 0)),
                  row(ROUTE_LANES), row(d), vec, vec],
        out_specs=pl.BlockSpec((tm, d), lambda i: (i, 0)),
        compiler_params=_cparams(("parallel",)),
        name="combine_ln2",
    )(yg, route, x1, g2.reshape(1, d), b2.reshape(1, d))


def _rope_tables(seq):
    half = QK_DIM // 2
    inv_freq = 1.0 / (ROPE_BASE ** (jnp.arange(half, dtype=F32) / half))
    ang = jnp.arange(seq, dtype=F32)[:, None] * inv_freq[None, :]
    cos, sin = jnp.cos(ang), jnp.sin(ang)
    return jnp.concatenate([cos, cos], axis=1), jnp.concatenate([-sin, sin], axis=1)


def _channel_dft_matrix():
    n = np.arange(FG_DIM)
    ang = 2.0 * np.pi * np.outer(n, n) / FG_DIM
    eye = np.eye(N_FGROUPS)
    bd = np.concatenate([np.kron(eye, np.cos(ang)), np.kron(eye, np.sin(ang))], axis=1) / math.sqrt(FG_DIM)
    return jnp.asarray(bd, dtype=BF16)


def _seq_dft_matrices(seq):
    blk = 64 if seq % 64 == 0 else 1
    k = jnp.arange(seq, dtype=jnp.int32)[:, None]
    hi = jnp.arange(seq // blk, dtype=jnp.int32)[None, :] * blk
    lo = jnp.arange(blk, dtype=jnp.int32)[None, :]
    w = 2.0 * math.pi / seq
    a_hi = ((k * hi) % seq).astype(F32) * w
    a_lo = ((k * lo) % seq).astype(F32) * w
    ch, sh = jnp.cos(a_hi)[:, :, None], jnp.sin(a_hi)[:, :, None]
    cl, sl = jnp.cos(a_lo)[:, None, :], jnp.sin(a_lo)[:, None, :]
    scale = seq ** -0.5
    cmat = ((ch * cl - sh * sl) * scale).reshape(seq, seq).astype(BF16)
    smat = ((sh * cl + ch * sl) * scale).reshape(seq, seq).astype(BF16)
    return cmat, smat


def _pad_heads(w, axis):
    shp = list(w.shape)
    shp[axis:axis + 1] = [N_HEADS, V_DIM]
    w = w.reshape(shp)
    pad = [(0, 0)] * w.ndim
    pad[axis + 1] = (0, V_PAD - V_DIM)
    w = jnp.pad(w, pad)
    shp[axis:axis + 2] = [N_HEADS * V_PAD]
    return w.reshape(shp)


def _pad_in_weight(w_in):
    qk_w = 2 * N_HEADS * QK_DIM
    vw = N_HEADS * V_DIM
    qk = w_in[:, :qk_w]
    v = _pad_heads(w_in[:, qk_w:qk_w + vw], 1)
    g = _pad_heads(w_in[:, qk_w + vw:qk_w + 2 * vw], 1)
    rest = w_in[:, qk_w + 2 * vw:]
    return jnp.concatenate([qk, v, g, rest], axis=1).astype(BF16)


def _routing_plan(route, tile_counts, tm_tok, n_tiles):
    t = route.shape[0]
    experts = jnp.arange(N_EXPERTS, dtype=jnp.int32)
    cnt = tile_counts[:, 0, :N_EXPERTS].astype(jnp.int32)
    before = jnp.cumsum(cnt, axis=0) - cnt
    counts = jnp.sum(cnt, axis=0)
    tiles_per = (counts + MOE_TILE - 1) // MOE_TILE
    tile_end = jnp.cumsum(tiles_per)
    start_row = (tile_end - tiles_per) * MOE_TILE
    n_valid = tile_end[-1]

    base = (start_row[None, :] + before).astype(F32)
    base_tok = jnp.broadcast_to(base[:, None, :], (t // tm_tok, tm_tok, N_EXPERTS)).reshape(t, N_EXPERTS)
    e = route[:, 0:2]
    rank = route[:, 4:6]
    sel = e[:, :, None] == experts.astype(F32)[None, None, :]
    pos = (jnp.sum(jnp.where(sel, base_tok[:, None, :], 0.0), axis=-1) + rank).astype(jnp.int32)

    n_pad = n_tiles * MOE_TILE - 2 * t
    pads = tiles_per * MOE_TILE - counts
    pad_end = jnp.cumsum(pads)
    p = jnp.arange(n_pad, dtype=jnp.int32)
    pe = jnp.minimum(jnp.sum((p[:, None] >= pad_end[None, :]).astype(jnp.int32), axis=1), N_EXPERTS - 1)
    in_expert = start_row[pe] + counts[pe] + (p - (pad_end[pe] - pads[pe]))
    pad_pos = jnp.where(p < pad_end[-1], in_expert, n_valid * MOE_TILE + (p - pad_end[-1]))

    tile_ids = jnp.arange(n_tiles, dtype=jnp.int32)
    tile_expert = jnp.minimum(jnp.sum((tile_ids[:, None] >= tile_end[None, :]).astype(jnp.int32), axis=1),
                              N_EXPERTS - 1)
    last_e = tile_expert[jnp.maximum(n_valid - 1, 0)]
    tile_expert = jnp.where(tile_ids < n_valid, tile_expert, last_e).astype(jnp.int32)
    return pos, pad_pos.astype(jnp.int32), tile_expert, n_valid.reshape(1).astype(jnp.int32)


def _largest_tile(cands, *dims):
    for c in cands:
        if all(d % c == 0 for d in dims):
            return c
    raise ValueError(f"no tile in {cands} divides {dims}")


def kernel(x_prompt, x_sample, ln_in_g, ln_in_b, w_in, ret_decay_fwd, ret_decay_bwd, ret_gn_g, ret_gn_b, w_ret_up, w_four_up, w_out, ln1_g, ln1_b, w_route_group, b_route_group, w_route_expert, b_route_expert, w_expert_gate, w_expert_up, w_expert_down, ln2_g, ln2_b):
    bp, sp, d = x_prompt.shape
    bs, ss, _ = x_sample.shape
    depth = w_in.shape[0]
    tp, ts = bp * sp, bs * ss
    t = tp + ts
    assert sp % CHUNK == 0 and ss % CHUNK == 0
    assert tp % ss == 0 and t % sp == 0 and t % ss == 0, "sequences must start on sequence-sized row blocks"
    alpha = float((2 * depth) ** 0.25)

    tm_in = _largest_tile((512, 256, 128), sp, ss)
    tm_tok = _largest_tile((TOKEN_TILE, 128), tp, ts)
    tm_merge = _largest_tile((MERGE_TILE, RANK_TILE), tp, ts)
    n_tiles = (2 * t) // MOE_TILE + N_EXPERTS
    assert (2 * t) % MOE_TILE == 0 and t % SC_GATHER_WINDOW == 0

    cos_t, sin_t = _rope_tables(max(sp, ss))
    bd = _channel_dft_matrix()
    dft = {s: _seq_dft_matrices(s) for s in sorted({sp, ss})}

    x = _layer_norm_concat(x_prompt.reshape(tp, d), x_sample.reshape(ts, d), ln_in_g, ln_in_b, tm_tok)

    for l in range(depth):
        w_pad = _pad_in_weight(w_in[l])
        dec = jnp.stack([ret_decay_fwd[l], ret_decay_bwd[l]]).astype(F32)
        gn_g = _pad_heads(ret_gn_g[l], 0).reshape(1, -1).astype(F32)
        gn_b = _pad_heads(ret_gn_b[l], 0).reshape(1, -1).astype(F32)
        wr = _pad_heads(w_ret_up[l], 0).astype(BF16)
        wf = w_four_up[l].astype(BF16)
        wo = w_out[l].astype(BF16)
        w_rt = jnp.concatenate([w_route_group[l], w_route_expert[l]], axis=1)
        w_rt = jnp.pad(w_rt, ((0, 0), (0, ROUTE_LANES // 2 - w_rt.shape[1])))
        wrh = w_rt.astype(BF16)
        wr2 = jnp.concatenate([wrh, (w_rt - wrh.astype(F32)).astype(BF16)], axis=1)
        rbias = jnp.concatenate([b_route_group[l], b_route_expert[l]])
        rbias = jnp.pad(rbias, (0, ROUTE_LANES - rbias.shape[0])).reshape(1, ROUTE_LANES).astype(F32)

        qk, v, sg, z, gates = _inproj(x, w_pad, cos_t, sin_t, bd, tm_in,
                                      tp // tm_in, sp // tm_in, ss // tm_in)

        r_p = _retention(dec, qk, v, sg, gn_g, gn_b, bp, sp, 0)
        r_s = _retention(dec, qk, v, sg, gn_g, gn_b, bs, ss, tp // ss)

        fo_p = _seq_dft(dft[sp][0], dft[sp][1], z, bp, sp, 0)
        fo_s = _seq_dft(dft[ss][0], dft[ss][1], z, bs, ss, tp // ss)

        x1, x1p, route, tile_counts = _merge(r_p, r_s, fo_p, fo_s, gates, x, wr, wf, wo,
                                             ln1_g[l], ln1_b[l], wr2, rbias, alpha, tm_merge)

        pos, pad_pos, tile_expert, n_valid = _routing_plan(route, tile_counts, RANK_TILE, n_tiles)
        n_sorted = n_tiles * MOE_TILE
        piece_off = jnp.arange(N_PIECES, dtype=jnp.int32)[:, None] * n_sorted
        dst = jnp.concatenate([(piece_off + seg[None, :]).reshape(-1) for seg in (pos[:, 0], pos[:, 1], pad_pos)])
        xs = _sc_scatter_rows(x1p.reshape(N_PIECES * t, PIECE_W), dst, N_PIECES * n_sorted)
        ys = _expert_mlp(tile_expert, n_valid, xs.reshape(N_PIECES, n_sorted, PIECE_W),
                         w_expert_gate[l], w_expert_up[l], w_expert_down[l])
        src = (piece_off[:, :, None] + pos.T[None, :, :]).reshape(-1)
        yg = _sc_gather_rows(ys.reshape(N_PIECES * n_sorted, PIECE_W), src).reshape(N_PIECES, 2, t, PIECE_W)
        if l + 1 < depth:
            x = _combine(yg, route, x1, ln2_g[l], ln2_b[l], alpha, tm_tok)
        else:
            y_p = _combine(yg, route, x1, ln2_g[l], ln2_b[l], alpha, tm_tok, 0, tp)
            y_s = _combine(yg, route, x1, ln2_g[l], ln2_b[l], alpha, tm_tok, tp, ts)

    return y_p.reshape(bp, sp, d), y_s.reshape(bs, ss, d)
```

```python
import functools
import math

import jax
import jax.numpy as jnp
import numpy as np
from jax import lax
from jax.experimental import pallas as pl
from jax.experimental.pallas import tpu as pltpu
from jax.experimental.pallas import tpu_sc as plsc

N_HEADS = 4
QK_DIM = 128
V_DIM = 192
V_PAD = 256
CHUNK = 128
ROPE_BASE = 10000.0
N_FGROUPS = 4
FG_DIM = 64
F_WIDTH = N_FGROUPS * FG_DIM
N_GROUPS = 4
EXPERTS_PER_GROUP = 8
N_EXPERTS = N_GROUPS * EXPERTS_PER_GROUP
LN_EPS = 1e-5
ROUTE_LANES = 128

LANE = 128
VMEM_LIMIT_BYTES = 56 * 1024 * 1024
MOE_TILE = 512
GATHER_TILE = 256
SC_GATHER_WINDOW = 128
N_PIECES = 2
PIECE_W = 256
TOKEN_TILE = 256
MERGE_TILE = 512
RANK_TILE = 256
RET_UNROLL = 8

F32 = jnp.float32
BF16 = jnp.bfloat16


def _cparams(sem):
    return pltpu.CompilerParams(dimension_semantics=sem, vmem_limit_bytes=VMEM_LIMIT_BYTES)


def _layer_norm_rows(y, g, b):
    mu = jnp.mean(y, axis=-1, keepdims=True)
    d = y - mu
    var = jnp.mean(d * d, axis=-1, keepdims=True)
    return d * lax.rsqrt(var + LN_EPS) * g + b


def _pack_halves(y):
    n = y.shape[1] // 2
    lo = pltpu.bitcast(y[:, :n].astype(BF16).astype(F32), jnp.uint32)
    hi = pltpu.bitcast(y[:, n:].astype(BF16).astype(F32), jnp.uint32)
    return (hi & jnp.uint32(0xFFFF0000)) | (lo >> 16)


def _unpack_halves(w):
    lo = pltpu.bitcast(w << 16, F32)
    hi = pltpu.bitcast(w & jnp.uint32(0xFFFF0000), F32)
    return lo, hi


def _store_pieces(ref, packed):
    for s in range(N_PIECES):
        ref[s] = packed[:, s * PIECE_W:(s + 1) * PIECE_W]


def _unpack_pieces(pieces):
    halves = [_unpack_halves(p) for p in pieces]
    return [h[0] for h in halves] + [h[1] for h in halves]


def _combine_rows(y_ref, route_ref, x_ref, g_ref, b_ref, alpha):
    c1 = route_ref[:, 2:3]
    c2 = route_ref[:, 3:4]
    first = _unpack_pieces([y_ref[s, 0] for s in range(N_PIECES)])
    second = _unpack_pieces([y_ref[s, 1] for s in range(N_PIECES)])
    ff = jnp.concatenate([c1 * a + c2 * b for a, b in zip(first, second)], axis=1)
    return _layer_norm_rows(alpha * x_ref[...] + ff, g_ref[...], b_ref[...])


def _inproj_first_kernel(xraw_ref, g_ref, b_ref, *rest, d_model):
    _project(_layer_norm_rows(xraw_ref[...], g_ref[...], b_ref[...]), *rest, d_model=d_model)


def _inproj_next_kernel(y_ref, route_ref, x1_ref, g_ref, b_ref, *rest, alpha, d_model):
    _project(_combine_rows(y_ref, route_ref, x1_ref, g_ref, b_ref, alpha), *rest, d_model=d_model)


def _project(x_f32, w_ref, cos_ref, sin_ref, bd_ref,
             x_out_ref, qk_ref, v_ref, sg_ref, z_ref, gates_ref, *, d_model):
    x_out_ref[...] = x_f32
    x = x_f32.astype(BF16)
    cos = cos_ref[...]
    sin = sin_ref[...]
    qk_w = 2 * N_HEADS * QK_DIM
    v_w = N_HEADS * V_PAD
    off_v = qk_w
    off_g = off_v + v_w
    off_f = off_g + v_w
    off_gr = off_f + F_WIDTH
    cw = 256

    def proj(c0):
        return jnp.dot(x, w_ref[:, c0:c0 + cw], preferred_element_type=F32)

    for c0 in range(0, qk_w, cw):
        acc = proj(c0)
        scale = QK_DIM ** -0.5 if c0 < N_HEADS * QK_DIM else 1.0
        parts = []
        for hh in range(cw // QK_DIM):
            a = acc[:, hh * QK_DIM:(hh + 1) * QK_DIM]
            rot = a * cos + pltpu.roll(a, QK_DIM // 2, axis=1) * sin
            if scale != 1.0:
                rot = rot * scale
            parts.append(rot.astype(BF16))
        qk_ref[:, c0:c0 + cw] = jnp.concatenate(parts, axis=1)
    for c0 in range(0, v_w, cw):
        v_ref[:, c0:c0 + cw] = proj(off_v + c0).astype(BF16)
    for c0 in range(0, v_w, cw):
        a = proj(off_g + c0)
        sg_ref[:, c0:c0 + cw] = (a * jax.nn.sigmoid(a)).astype(BF16)
    f = proj(off_f).astype(BF16)
    z_ref[...] = jnp.dot(f, bd_ref[...], preferred_element_type=F32).astype(BF16)
    for c0 in range(0, 2 * d_model, cw):
        gates_ref[:, c0:c0 + cw] = jax.nn.sigmoid(proj(off_gr + c0)).astype(BF16)


def _inproj(prologue, g, b, w_pad, cos_t, sin_t, bd, tm, seq, alpha):
    first = len(prologue) == 1
    t, d = prologue[-1].shape
    n_cols = w_pad.shape[1]
    pos_map = lambda i: (i % (seq // tm), 0)
    row = lambda w: pl.BlockSpec((tm, w), lambda i: (i, 0))
    vec = pl.BlockSpec((1, d), lambda i: (0, 0))
    outs = (jax.ShapeDtypeStruct((t, d), F32),
            jax.ShapeDtypeStruct((t, 2 * N_HEADS * QK_DIM), BF16),
            jax.ShapeDtypeStruct((t, N_HEADS * V_PAD), BF16),
            jax.ShapeDtypeStruct((t, N_HEADS * V_PAD), BF16),
            jax.ShapeDtypeStruct((t, 2 * F_WIDTH), BF16),
            jax.ShapeDtypeStruct((t, 2 * d), BF16))
    if first:
        body = functools.partial(_inproj_first_kernel, d_model=d)
        pro_specs = [row(d)]
    else:
        body = functools.partial(_inproj_next_kernel, alpha=alpha, d_model=d)
        pro_specs = [pl.BlockSpec((N_PIECES, 2, tm, PIECE_W), lambda i: (0, 0, i, 0)), row(ROUTE_LANES), row(d)]
    return pl.pallas_call(
        body,
        out_shape=outs,
        grid=(t // tm,),
        in_specs=pro_specs + [vec, vec,
                              pl.BlockSpec((d, n_cols), lambda i: (0, 0)),
                              pl.BlockSpec((tm, QK_DIM), pos_map),
                              pl.BlockSpec((tm, QK_DIM), pos_map),
                              pl.BlockSpec((F_WIDTH, 2 * F_WIDTH), lambda i: (0, 0))],
        out_specs=tuple(row(o.shape[1]) for o in outs),
        compiler_params=_cparams(("parallel",)),
        name="inproj",
    )(*prologue, g.reshape(1, d), b.reshape(1, d), w_pad, cos_t, sin_t, bd)


def _log_sigmoid(x):
    return jnp.minimum(x, 0.0) - jnp.log1p(jnp.exp(-jnp.abs(x)))


def _retention_kernel(dec_ref, q_ref, k_ref, v_ref, sg_ref, gng_ref, gnb_ref, o_ref,
                      r_all, rf_s, rb_s, *, n_chunks):
    h = pl.program_id(1)
    c = CHUNK
    lf = _log_sigmoid(jnp.full((1, c), dec_ref[0, h], F32))
    lb = _log_sigmoid(jnp.full((1, c), dec_ref[1, h], F32))
    lf_w = _log_sigmoid(jnp.full((1, V_PAD), dec_ref[0, h], F32))
    lb_w = _log_sigmoid(jnp.full((1, V_PAD), dec_ref[1, h], F32))
    row = lax.broadcasted_iota(jnp.int32, (c, c), 0).astype(F32)
    col = lax.broadcasted_iota(jnp.int32, (c, c), 1).astype(F32)
    diff = row - col
    dmat = jnp.where(diff >= 0.0, jnp.exp(lf * jnp.maximum(diff, 0.0)),
                     jnp.exp(lb * jnp.maximum(-diff, 0.0)))
    qd_f = jnp.exp(lf * (row + 1.0))
    kd_f = jnp.exp(lf * (c - 1.0 - row))
    qd_b = jnp.exp(lb * (c - row))
    kd_b = jnp.exp(lb * row)
    cf = jnp.exp(lf_w * float(c))
    cb = jnp.exp(lb_w * float(c))
    lane = lax.broadcasted_iota(jnp.int32, (c, V_PAD), 1)
    vmask = (lane < V_DIM).astype(F32)
    gn_g = gng_ref[...]
    gn_b = gnb_ref[...]
    tdot = (((0,), (0,)), ((), ()))

    rf_s[...] = jnp.zeros_like(rf_s)
    rb_s[...] = jnp.zeros_like(rb_s)

    def state_step(t, carry):
        for (ci, kd, cdec, r_s, half) in ((t, kd_f, cf, rf_s, 0), (n_chunks - 1 - t, kd_b, cb, rb_s, 1)):
            r0 = pl.multiple_of(ci * c, c)
            kk = (k_ref[pl.ds(r0, c), :].astype(F32) * kd).astype(BF16)
            kv = lax.dot_general(kk, v_ref[pl.ds(r0, c), :], tdot, preferred_element_type=F32)
            r = r_s[...]
            r_all[ci, half * c:(half + 1) * c, :] = r.astype(BF16)
            r_s[...] = cdec * r + kv
        return carry

    lax.fori_loop(0, n_chunks, state_step, 0, unroll=RET_UNROLL)

    def out_step(ci, carry):
        r0 = pl.multiple_of(ci * c, c)
        q = q_ref[pl.ds(r0, c), :]
        qf = q.astype(F32)
        s = lax.dot_general(q, k_ref[pl.ds(r0, c), :], (((1,), (1,)), ((), ())), preferred_element_type=F32)
        p = (s * dmat).astype(BF16)
        o = jnp.dot(p, v_ref[pl.ds(r0, c), :], preferred_element_type=F32)
        qq = jnp.concatenate([(qf * qd_f).astype(BF16), (qf * qd_b).astype(BF16)], axis=1)
        o = o + jnp.dot(qq, r_all[ci], preferred_element_type=F32)
        mu = jnp.sum(o, axis=-1, keepdims=True) * (1.0 / V_DIM)
        d = (o - mu) * vmask
        var = jnp.sum(d * d, axis=-1, keepdims=True) * (1.0 / V_DIM)
        y = d * lax.rsqrt(var + LN_EPS) * gn_g + gn_b
        o_ref[pl.ds(r0, c), :] = (y * sg_ref[pl.ds(r0, c), :].astype(F32)).astype(BF16)
        return carry

    lax.fori_loop(0, n_chunks, out_step, 0, unroll=RET_UNROLL)


def _retention(dec, qk, v, sg, gn_g, gn_b, n_batch, seq, row_block0):
    n_chunks = seq // CHUNK
    rb = lambda b: b + row_block0
    return pl.pallas_call(
        functools.partial(_retention_kernel, n_chunks=n_chunks),
        out_shape=jax.ShapeDtypeStruct((n_batch * seq, N_HEADS * V_PAD), BF16),
        grid_spec=pltpu.PrefetchScalarGridSpec(
            num_scalar_prefetch=0,
            grid=(n_batch, N_HEADS),
            in_specs=[pl.BlockSpec(memory_space=pltpu.SMEM),
                      pl.BlockSpec((seq, QK_DIM), lambda b, h: (rb(b), h)),
                      pl.BlockSpec((seq, QK_DIM), lambda b, h: (rb(b), N_HEADS + h)),
                      pl.BlockSpec((seq, V_PAD), lambda b, h: (rb(b), h)),
                      pl.BlockSpec((seq, V_PAD), lambda b, h: (rb(b), h)),
                      pl.BlockSpec((1, V_PAD), lambda b, h: (0, h)),
                      pl.BlockSpec((1, V_PAD), lambda b, h: (0, h))],
            out_specs=pl.BlockSpec((seq, V_PAD), lambda b, h: (b, h)),
            scratch_shapes=[pltpu.VMEM((n_chunks, 2 * CHUNK, V_PAD), BF16),
                            pltpu.VMEM((CHUNK, V_PAD), F32),
                            pltpu.VMEM((CHUNK, V_PAD), F32)]),
        compiler_params=_cparams(("parallel", "arbitrary")),
        name="retention",
    )(dec, qk, qk, v, sg, gn_g, gn_b)


def _seqdft_kernel(c_ref, s_ref, zc_ref, zs_ref, o_ref, acc_ref):
    kk = pl.program_id(1)
    b = pl.program_id(2)
    part = (jnp.dot(c_ref[...], zc_ref[...], preferred_element_type=F32)
            - jnp.dot(s_ref[...], zs_ref[...], preferred_element_type=F32))

    @pl.when(kk == 0)
    def _():
        acc_ref[b] = part

    @pl.when(kk > 0)
    def _():
        acc_ref[b] += part

    @pl.when(kk == pl.num_programs(1) - 1)
    def _():
        o_ref[b] = acc_ref[b].astype(o_ref.dtype)


def _seq_dft(cmat, smat, z, n_batch, seq, row_block0):
    tm = min(1024, seq)
    tk = min(1024, seq)
    z3 = z.reshape(z.shape[0] // seq, seq, z.shape[1])
    out = pl.pallas_call(
        _seqdft_kernel,
        out_shape=jax.ShapeDtypeStruct((n_batch, seq, F_WIDTH), BF16),
        grid=(seq // tm, seq // tk, n_batch),
        in_specs=[pl.BlockSpec((tm, tk), lambda i, k, b: (i, k)),
                  pl.BlockSpec((tm, tk), lambda i, k, b: (i, k)),
                  pl.BlockSpec((None, tk, F_WIDTH), lambda i, k, b: (b + row_block0, k, 0)),
                  pl.BlockSpec((None, tk, F_WIDTH), lambda i, k, b: (b + row_block0, k, 1))],
        out_specs=pl.BlockSpec((n_batch, tm, F_WIDTH), lambda i, k, b: (0, i, 0)),
        scratch_shapes=[pltpu.VMEM((n_batch, tm, F_WIDTH), F32)],
        compiler_params=_cparams(("parallel", "arbitrary", "arbitrary")),
        name="seq_dft",
    )(cmat, smat, z3, z3)
    return out.reshape(n_batch * seq, F_WIDTH)


def _merge_kernel(r_ref, fo_ref, gates_ref, x_ref, wr_ref, wf_ref, wo_ref, g1_ref, b1_ref,
                  wr2_ref, rb_ref, x1_ref, x1p_ref, route_ref, cnt_ref, *, alpha, d_model):
    for j in range(x_ref.shape[0] // RANK_TILE):
        rows = pl.ds(j * RANK_TILE, RANK_TILE)
        _merge_subtile(rows, j, r_ref, fo_ref, gates_ref, x_ref, wr_ref, wf_ref, wo_ref,
                       g1_ref, b1_ref, wr2_ref, rb_ref, x1_ref, x1p_ref, route_ref, cnt_ref,
                       alpha=alpha, d_model=d_model)


def _merge_subtile(rows, j, r_ref, fo_ref, gates_ref, x_ref, wr_ref, wf_ref, wo_ref,
                   g1_ref, b1_ref, wr2_ref, rb_ref, x1_ref, x1p_ref, route_ref, cnt_ref, *, alpha, d_model):
    a = jnp.dot(r_ref[rows, :], wr_ref[...], preferred_element_type=F32)
    b = jnp.dot(fo_ref[rows, :], wf_ref[...], preferred_element_type=F32)
    merged = (gates_ref[rows, :d_model].astype(F32) * a + gates_ref[rows, d_model:].astype(F32) * b)
    mix = jnp.dot(merged.astype(BF16), wo_ref[...], preferred_element_type=F32)
    x1 = _layer_norm_rows(alpha * x_ref[rows, :] + mix, g1_ref[...], b1_ref[...])
    x1_ref[rows, :] = x1
    packed = _pack_halves(x1)
    for s in range(N_PIECES):
        x1p_ref[s, rows, :] = packed[:, s * PIECE_W:(s + 1) * PIECE_W]

    xh = x1.astype(BF16)
    xl = (x1 - xh.astype(F32)).astype(BF16)
    s2 = (jnp.dot(xh, wr2_ref[...], preferred_element_type=F32)
          + jnp.dot(xl, wr2_ref[...], preferred_element_type=F32))
    logits = s2 + pltpu.roll(s2, ROUTE_LANES // 2, axis=1) + rb_ref[...]
    tm = logits.shape[0]
    lane = lax.broadcasted_iota(jnp.int32, (tm, ROUTE_LANES), 1)
    neg = jnp.float32(-jnp.inf)
    big = jnp.int32(ROUTE_LANES)

    def top1(mask):
        vals = jnp.where(mask, logits, neg)
        m = jnp.max(vals, axis=-1, keepdims=True)
        idx = jnp.min(jnp.where(mask & (vals == m), lane, big), axis=-1, keepdims=True)
        return m, idx

    gmask = lane < N_GROUPS
    gm, gi = top1(gmask)
    gsum = jnp.sum(jnp.where(gmask, jnp.exp(logits - gm), 0.0), axis=-1, keepdims=True)
    g_gate = 1.0 / gsum
    lo = N_GROUPS + gi * EXPERTS_PER_GROUP
    emask = (lane >= lo) & (lane < lo + EXPERTS_PER_GROUP)
    m1, i1 = top1(emask)
    m2, i2 = top1(emask & (lane != i1))
    e2 = jnp.exp(m2 - m1)
    w1 = 1.0 / (1.0 + e2)
    w2 = e2 / (1.0 + e2)
    oh1 = (lane == i1 - N_GROUPS).astype(F32)
    oh2 = (lane == i2 - N_GROUPS).astype(F32)
    earlier = (lax.broadcasted_iota(jnp.int32, (tm, tm), 1)
               < lax.broadcasted_iota(jnp.int32, (tm, tm), 0)).astype(BF16)
    tot1 = jnp.sum(oh1, axis=0, keepdims=True)
    tot2 = jnp.sum(oh2, axis=0, keepdims=True)
    pre1 = jnp.dot(earlier, oh1.astype(BF16), preferred_element_type=F32)
    pre2 = jnp.dot(earlier, oh2.astype(BF16), preferred_element_type=F32) + tot1
    r1 = jnp.sum(pre1 * oh1, axis=-1, keepdims=True)
    r2 = jnp.sum(pre2 * oh2, axis=-1, keepdims=True)
    cnt_ref[j] = tot1 + tot2

    rec = jnp.where(lane == 0, (i1 - N_GROUPS).astype(F32), 0.0)
    rec = jnp.where(lane == 1, (i2 - N_GROUPS).astype(F32), rec)
    rec = jnp.where(lane == 2, g_gate * w1, rec)
    rec = jnp.where(lane == 3, g_gate * w2, rec)
    rec = jnp.where(lane == 4, r1, rec)
    rec = jnp.where(lane == 5, r2, rec)
    route_ref[rows, :] = rec


def _merge(r, fo, gates, x, wr, wf, wo, g1, b1, wr2, rbias, alpha, tm):
    t, d = x.shape
    row = lambda w: pl.BlockSpec((tm, w), lambda i: (i, 0))
    full = lambda a: pl.BlockSpec(a.shape, lambda i: (0, 0))
    g1 = g1.reshape(1, d)
    b1 = b1.reshape(1, d)
    return pl.pallas_call(
        functools.partial(_merge_kernel, alpha=alpha, d_model=d),
        out_shape=(jax.ShapeDtypeStruct((t, d), F32),
                   jax.ShapeDtypeStruct((N_PIECES, t, PIECE_W), jnp.uint32),
                   jax.ShapeDtypeStruct((t, ROUTE_LANES), F32),
                   jax.ShapeDtypeStruct((t // RANK_TILE, 1, ROUTE_LANES), F32)),
        grid=(t // tm,),
        in_specs=[row(r.shape[1]), row(fo.shape[1]), row(gates.shape[1]), row(d),
                  full(wr), full(wf), full(wo), full(g1), full(b1), full(wr2), full(rbias)],
        out_specs=(row(d), pl.BlockSpec((N_PIECES, tm, PIECE_W), lambda i: (0, i, 0)), row(ROUTE_LANES),
                   pl.BlockSpec((tm // RANK_TILE, 1, ROUTE_LANES), lambda i: (i, 0, 0))),
        compiler_params=_cparams(("parallel",)),
        name="merge_router",
    )(r, fo, gates, x, wr, wf, wo, g1, b1, wr2, rbias)


def _sc_mesh():
    return plsc.VectorSubcoreMesh(core_axis_name="c", subcore_axis_name="s")


def _sc_gather_rows(table, idx):
    n, w = idx.shape[0], table.shape[1]
    idx = idx.reshape(1, n)

    @functools.partial(pl.kernel, out_type=jax.ShapeDtypeStruct((n, w), table.dtype), mesh=_sc_mesh(),
                       scratch_types=[])
    def gather(tab_hbm, idx_hbm, out_hbm):
        def body(idx_vmem, out_vmem):
            pltpu.sync_copy(tab_hbm.at[idx_vmem.at[0]], out_vmem)

        pltpu.emit_pipeline(
            body,
            grid=(n // SC_GATHER_WINDOW,),
            in_specs=[pl.BlockSpec((1, SC_GATHER_WINDOW), lambda i: (0, i))],
            out_specs=[pl.BlockSpec((SC_GATHER_WINDOW, w), lambda i: (i, 0))],
            core_axis_name=("c", "s"),
            dimension_semantics=(pltpu.PARALLEL,),
        )(idx_hbm, out_hbm)

    return gather(table, idx)


def _sc_scatter_rows(src, idx, n_out):
    assert idx.shape[0] == n_out
    n, w = n_out, src.shape[1]
    idx = idx.reshape(1, n)
    src_windows = src.shape[0] // SC_GATHER_WINDOW

    @functools.partial(pl.kernel, out_type=jax.ShapeDtypeStruct((n, w), src.dtype), mesh=_sc_mesh(),
                       scratch_types=[])
    def scatter(src_hbm, idx_hbm, out_hbm):
        def body(src_vmem, idx_vmem):
            pltpu.sync_copy(src_vmem, out_hbm.at[idx_vmem.at[0]])

        pltpu.emit_pipeline(
            body,
            grid=(n // SC_GATHER_WINDOW,),
            in_specs=[pl.BlockSpec((SC_GATHER_WINDOW, w), lambda i: (i % src_windows, 0)),
                      pl.BlockSpec((1, SC_GATHER_WINDOW), lambda i: (0, i))],
            out_specs=[],
            core_axis_name=("c", "s"),
            dimension_semantics=(pltpu.PARALLEL,),
        )(src_hbm, idx_hbm)

    return scatter(src, idx)


def _expert_kernel(te_ref, nv_ref, xs_ref, wg_ref, wu_ref, wd_ref, o_ref, wg_s, wu_s, wd_s):
    i = pl.program_id(0)

    @pl.when((i == 0) | (te_ref[i] != te_ref[jnp.maximum(i - 1, 0)]))
    def _():
        wg_s[...] = wg_ref[...].astype(BF16)
        wu_s[...] = wu_ref[...].astype(BF16)
        wd_s[...] = wd_ref[...].astype(BF16)

    @pl.when(i < nv_ref[0])
    def _():
        chunks = [c.astype(BF16) for c in _unpack_pieces([xs_ref[s] for s in range(N_PIECES)])]
        cw = chunks[0].shape[1]

        def up(w_s):
            acc = jnp.dot(chunks[0], w_s[:cw, :], preferred_element_type=F32)
            for j in range(1, len(chunks)):
                acc = acc + jnp.dot(chunks[j], w_s[j * cw:(j + 1) * cw, :], preferred_element_type=F32)
            return acc

        g = up(wg_s)
        hmid = (g * jax.nn.sigmoid(g) * up(wu_s)).astype(BF16)
        _store_pieces(o_ref, _pack_halves(jnp.dot(hmid, wd_s[...], preferred_element_type=F32)))

    @pl.when(i >= nv_ref[0])
    def _():
        o_ref[...] = jnp.zeros_like(o_ref)


def _expert_mlp(tile_expert, n_valid, xs, wg, wu, wd, layer, tm):
    p = xs.shape[1]
    d, de = wg.shape[1], wg.shape[2]
    xmap = lambda i, te, nv: (0, jnp.minimum(i, nv[0] - 1), 0)
    wmap = lambda i, te, nv: (layer * N_EXPERTS + te[i], 0, 0)
    return pl.pallas_call(
        _expert_kernel,
        out_shape=jax.ShapeDtypeStruct(xs.shape, jnp.uint32),
        grid_spec=pltpu.PrefetchScalarGridSpec(
            num_scalar_prefetch=2,
            grid=(p // tm,),
            in_specs=[pl.BlockSpec((N_PIECES, tm, PIECE_W), xmap),
                      pl.BlockSpec((None, d, de), wmap),
                      pl.BlockSpec((None, d, de), wmap),
                      pl.BlockSpec((None, de, d), wmap)],
            out_specs=pl.BlockSpec((N_PIECES, tm, PIECE_W), lambda i, te, nv: (0, i, 0)),
            scratch_shapes=[pltpu.VMEM((d, de), BF16), pltpu.VMEM((d, de), BF16), pltpu.VMEM((de, d), BF16)]),
        compiler_params=_cparams(("arbitrary",)),
        name="expert_mlp",
    )(tile_expert, n_valid, xs, wg, wu, wd)


def _combine_kernel(y_ref, route_ref, x_ref, g_ref, b_ref, o_ref, *, alpha):
    o_ref[...] = _combine_rows(y_ref, route_ref, x_ref, g_ref, b_ref, alpha)


def _combine(yg, route, x1, g2, b2, alpha, tm):
    n_rows, d = x1.shape
    row = lambda w: pl.BlockSpec((tm, w), lambda i: (i, 0))
    vec = pl.BlockSpec((1, d), lambda i: (0, 0))
    return pl.pallas_call(
        functools.partial(_combine_kernel, alpha=alpha),
        out_shape=jax.ShapeDtypeStruct((n_rows, d), F32),
        grid=(n_rows // tm,),
        in_specs=[pl.BlockSpec((N_PIECES, 2, tm, PIECE_W), lambda i: (0, 0, i, 0)),
                  row(ROUTE_LANES), row(d), vec, vec],
        out_specs=pl.BlockSpec((tm, d), lambda i: (i, 0)),
        compiler_params=_cparams(("parallel",)),
        name="combine_ln2",
    )(yg, route, x1, g2.reshape(1, d), b2.reshape(1, d))


def _rope_tables(seq):
    half = QK_DIM // 2
    inv_freq = 1.0 / (ROPE_BASE ** (jnp.arange(half, dtype=F32) / half))
    ang = jnp.arange(seq, dtype=F32)[:, None] * inv_freq[None, :]
    cos, sin = jnp.cos(ang), jnp.sin(ang)
    return jnp.concatenate([cos, cos], axis=1), jnp.concatenate([-sin, sin], axis=1)


def _channel_dft_matrix():
    n = np.arange(FG_DIM)
    ang = 2.0 * np.pi * np.outer(n, n) / FG_DIM
    eye = np.eye(N_FGROUPS)
    bd = np.concatenate([np.kron(eye, np.cos(ang)), np.kron(eye, np.sin(ang))], axis=1) / math.sqrt(FG_DIM)
    return jnp.asarray(bd, dtype=BF16)


def _seq_dft_matrices(seq):
    blk = 64 if seq % 64 == 0 else 1
    k = jnp.arange(seq, dtype=jnp.int32)[:, None]
    hi = jnp.arange(seq // blk, dtype=jnp.int32)[None, :] * blk
    lo = jnp.arange(blk, dtype=jnp.int32)[None, :]
    w = 2.0 * math.pi / seq
    a_hi = ((k * hi) % seq).astype(F32) * w
    a_lo = ((k * lo) % seq).astype(F32) * w
    ch, sh = jnp.cos(a_hi)[:, :, None], jnp.sin(a_hi)[:, :, None]
    cl, sl = jnp.cos(a_lo)[:, None, :], jnp.sin(a_lo)[:, None, :]
    scale = seq ** -0.5
    cmat = ((ch * cl - sh * sl) * scale).reshape(seq, seq).astype(BF16)
    smat = ((sh * cl + ch * sl) * scale).reshape(seq, seq).astype(BF16)
    return cmat, smat


def _pad_heads(w, axis):
    shp = list(w.shape)
    shp[axis:axis + 1] = [N_HEADS, V_DIM]
    w = w.reshape(shp)
    pad = [(0, 0)] * w.ndim
    pad[axis + 1] = (0, V_PAD - V_DIM)
    w = jnp.pad(w, pad)
    shp[axis:axis + 2] = [N_HEADS * V_PAD]
    return w.reshape(shp)


def _pad_in_weight(w_in):
    qk_w = 2 * N_HEADS * QK_DIM
    vw = N_HEADS * V_DIM
    qk = w_in[:, :qk_w]
    v = _pad_heads(w_in[:, qk_w:qk_w + vw], 1)
    g = _pad_heads(w_in[:, qk_w + vw:qk_w + 2 * vw], 1)
    rest = w_in[:, qk_w + 2 * vw:]
    return jnp.concatenate([qk, v, g, rest], axis=1).astype(BF16)


def _routing_plan(route, tile_counts, tm_tok, n_tiles, moe_tile):
    t = route.shape[0]
    experts = jnp.arange(N_EXPERTS, dtype=jnp.int32)
    cnt = tile_counts[:, 0, :N_EXPERTS].astype(jnp.int32)
    before = jnp.cumsum(cnt, axis=0) - cnt
    counts = jnp.sum(cnt, axis=0)
    tiles_per = (counts + moe_tile - 1) // moe_tile
    tile_end = jnp.cumsum(tiles_per)
    start_row = (tile_end - tiles_per) * moe_tile
    n_valid = tile_end[-1]

    base = (start_row[None, :] + before).astype(F32)
    base_tok = jnp.broadcast_to(base[:, None, :], (t // tm_tok, tm_tok, N_EXPERTS)).reshape(t, N_EXPERTS)
    e = route[:, 0:2]
    rank = route[:, 4:6]
    sel = e[:, :, None] == experts.astype(F32)[None, None, :]
    pos = (jnp.sum(jnp.where(sel, base_tok[:, None, :], 0.0), axis=-1) + rank).astype(jnp.int32)

    n_pad = n_tiles * moe_tile - 2 * t
    pads = tiles_per * moe_tile - counts
    pad_end = jnp.cumsum(pads)
    p = jnp.arange(n_pad, dtype=jnp.int32)
    pe = jnp.minimum(jnp.sum((p[:, None] >= pad_end[None, :]).astype(jnp.int32), axis=1), N_EXPERTS - 1)
    in_expert = start_row[pe] + counts[pe] + (p - (pad_end[pe] - pads[pe]))
    pad_pos = jnp.where(p < pad_end[-1], in_expert, n_valid * moe_tile + (p - pad_end[-1]))

    tile_ids = jnp.arange(n_tiles, dtype=jnp.int32)
    tile_expert = jnp.minimum(jnp.sum((tile_ids[:, None] >= tile_end[None, :]).astype(jnp.int32), axis=1),
                              N_EXPERTS - 1)
    last_e = tile_expert[jnp.maximum(n_valid - 1, 0)]
    tile_expert = jnp.where(tile_ids < n_valid, tile_expert, last_e).astype(jnp.int32)
    return pos, pad_pos.astype(jnp.int32), tile_expert, n_valid.reshape(1).astype(jnp.int32)


def _largest_tile(cands, *dims):
    for c in cands:
        if all(d % c == 0 for d in dims):
            return c
    raise ValueError(f"no tile in {cands} divides {dims}")


def kernel(x_prompt, x_sample, ln_in_g, ln_in_b, w_in, ret_decay_fwd, ret_decay_bwd, ret_gn_g, ret_gn_b, w_ret_up, w_four_up, w_out, ln1_g, ln1_b, w_route_group, b_route_group, w_route_expert, b_route_expert, w_expert_gate, w_expert_up, w_expert_down, ln2_g, ln2_b):
    bp, sp, d = x_prompt.shape
    bs, ss, _ = x_sample.shape
    depth = w_in.shape[0]
    alpha = float((2 * depth) ** 0.25)

    cos_t, sin_t = _rope_tables(max(sp, ss))
    bd = _channel_dft_matrix()
    dft = {s: _seq_dft_matrices(s) for s in sorted({sp, ss})}
    wg_all = w_expert_gate.reshape((depth * N_EXPERTS,) + w_expert_gate.shape[2:])
    wu_all = w_expert_up.reshape((depth * N_EXPERTS,) + w_expert_up.shape[2:])
    wd_all = w_expert_down.reshape((depth * N_EXPERTS,) + w_expert_down.shape[2:])

    batches = []
    for xin in (x_prompt, x_sample):
        nb, seq, _ = xin.shape
        t = nb * seq
        assert seq % CHUNK == 0 and t % SC_GATHER_WINDOW == 0
        moe_tile = MOE_TILE if 2 * t >= 4 * MOE_TILE * N_EXPERTS else MOE_TILE // 2
        assert (2 * t) % moe_tile == 0
        batches.append(dict(nb=nb, seq=seq, t=t, moe_tile=moe_tile, n_tiles=(2 * t) // moe_tile + N_EXPERTS,
                            tm_in=_largest_tile((512, 256, 128), seq),
                            tm_tok=_largest_tile((TOKEN_TILE, 128), t),
                            tm_merge=_largest_tile((MERGE_TILE, RANK_TILE), t),
                            prologue=(xin.reshape(t, d),)))

    for l in range(depth):
        w_pad = _pad_in_weight(w_in[l])
        dec = jnp.stack([ret_decay_fwd[l], ret_decay_bwd[l]]).astype(F32)
        gn_g = _pad_heads(ret_gn_g[l], 0).reshape(1, -1).astype(F32)
        gn_b = _pad_heads(ret_gn_b[l], 0).reshape(1, -1).astype(F32)
        wr = _pad_heads(w_ret_up[l], 0).astype(BF16)
        wf = w_four_up[l].astype(BF16)
        wo = w_out[l].astype(BF16)
        w_rt = jnp.concatenate([w_route_group[l], w_route_expert[l]], axis=1)
        w_rt = jnp.pad(w_rt, ((0, 0), (0, ROUTE_LANES // 2 - w_rt.shape[1])))
        wrh = w_rt.astype(BF16)
        wr2 = jnp.concatenate([wrh, (w_rt - wrh.astype(F32)).astype(BF16)], axis=1)
        rbias = jnp.concatenate([b_route_group[l], b_route_expert[l]])
        rbias = jnp.pad(rbias, (0, ROUTE_LANES - rbias.shape[0])).reshape(1, ROUTE_LANES).astype(F32)
        pro_g, pro_b = (ln_in_g, ln_in_b) if l == 0 else (ln2_g[l - 1], ln2_b[l - 1])

        for bt in batches:
            nb, seq, t, moe_tile, n_tiles = bt["nb"], bt["seq"], bt["t"], bt["moe_tile"], bt["n_tiles"]
            x, qk, v, sg, z, gates = _inproj(bt["prologue"], pro_g, pro_b, w_pad, cos_t, sin_t, bd,
                                             bt["tm_in"], seq, alpha)
            r = _retention(dec, qk, v, sg, gn_g, gn_b, nb, seq, 0)
            fo = _seq_dft(dft[seq][0], dft[seq][1], z, nb, seq, 0)
            x1, x1p, route, tile_counts = _merge(r, fo, gates, x, wr, wf, wo, ln1_g[l], ln1_b[l], wr2, rbias,
                                                 alpha, bt["tm_merge"])

            pos, pad_pos, tile_expert, n_valid = _routing_plan(route, tile_counts, RANK_TILE, n_tiles, moe_tile)
            n_sorted = n_tiles * moe_tile
            piece_off = jnp.arange(N_PIECES, dtype=jnp.int32)[:, None] * n_sorted
            dst = jnp.concatenate([(piece_off + seg[None, :]).reshape(-1)
                                   for seg in (pos[:, 0], pos[:, 1], pad_pos)])
            xs = _sc_scatter_rows(x1p.reshape(N_PIECES * t, PIECE_W), dst, N_PIECES * n_sorted)
            ys = _expert_mlp(tile_expert, n_valid, xs.reshape(N_PIECES, n_sorted, PIECE_W),
                             wg_all, wu_all, wd_all, l, moe_tile)
            src = (piece_off[:, :, None] + pos.T[None, :, :]).reshape(-1)
            yg = _sc_gather_rows(ys.reshape(N_PIECES * n_sorted, PIECE_W), src).reshape(N_PIECES, 2, t, PIECE_W)
            bt["prologue"] = (yg, route, x1)

    outs = [_combine(*bt["prologue"], ln2_g[depth - 1], ln2_b[depth - 1], alpha, bt["tm_tok"]) for bt in batches]
    return outs[0].reshape(bp, sp, d), outs[1].reshape(bs, ss, d)
```

```python
import functools
import math

import jax
import jax.numpy as jnp
import numpy as np
from jax import lax
from jax.experimental import pallas as pl
from jax.experimental.pallas import tpu as pltpu
from jax.experimental.pallas import tpu_sc as plsc

N_HEADS = 4
QK_DIM = 128
V_DIM = 192
V_PAD = 256
CHUNK = 128
ROPE_BASE = 10000.0
N_FGROUPS = 4
FG_DIM = 64
F_WIDTH = N_FGROUPS * FG_DIM
N_GROUPS = 4
EXPERTS_PER_GROUP = 8
N_EXPERTS = N_GROUPS * EXPERTS_PER_GROUP
LN_EPS = 1e-5
ROUTE_LANES = 128

LANE = 128
VMEM_LIMIT_BYTES = 56 * 1024 * 1024
MOE_TILE = 512
GATHER_TILE = 256
SC_GATHER_WINDOW = 128
N_PIECES = 2
PIECE_W = 256
TOKEN_TILE = 256
MERGE_TILE = 512
RANK_TILE = 256
RET_UNROLL = 8

F32 = jnp.float32
BF16 = jnp.bfloat16


def _cparams(sem):
    return pltpu.CompilerParams(dimension_semantics=sem, vmem_limit_bytes=VMEM_LIMIT_BYTES)


def _layer_norm_rows(y, g, b):
    mu = jnp.mean(y, axis=-1, keepdims=True)
    d = y - mu
    var = jnp.mean(d * d, axis=-1, keepdims=True)
    return d * lax.rsqrt(var + LN_EPS) * g + b


def _pack_halves(y):
    n = y.shape[1] // 2
    lo = pltpu.bitcast(y[:, :n].astype(BF16).astype(F32), jnp.uint32)
    hi = pltpu.bitcast(y[:, n:].astype(BF16).astype(F32), jnp.uint32)
    return (hi & jnp.uint32(0xFFFF0000)) | (lo >> 16)


def _unpack_halves(w):
    lo = pltpu.bitcast(w << 16, F32)
    hi = pltpu.bitcast(w & jnp.uint32(0xFFFF0000), F32)
    return lo, hi


def _store_pieces(ref, packed):
    for s in range(N_PIECES):
        ref[s] = packed[:, s * PIECE_W:(s + 1) * PIECE_W]


def _unpack_pieces(pieces):
    halves = [_unpack_halves(p) for p in pieces]
    return [h[0] for h in halves] + [h[1] for h in halves]


def _combine_rows(y_ref, route_ref, x_ref, g_ref, b_ref, alpha):
    c1 = route_ref[:, 2:3]
    c2 = route_ref[:, 3:4]
    first = _unpack_pieces([y_ref[s, 0] for s in range(N_PIECES)])
    second = _unpack_pieces([y_ref[s, 1] for s in range(N_PIECES)])
    ff = jnp.concatenate([c1 * a + c2 * b for a, b in zip(first, second)], axis=1)
    return _layer_norm_rows(alpha * x_ref[...] + ff, g_ref[...], b_ref[...])


def _inproj_first_kernel(xraw_ref, g_ref, b_ref, *rest, d_model):
    _project(_layer_norm_rows(xraw_ref[...], g_ref[...], b_ref[...]), *rest, d_model=d_model)


def _inproj_next_kernel(y_ref, route_ref, x1_ref, g_ref, b_ref, *rest, alpha, d_model):
    _project(_combine_rows(y_ref, route_ref, x1_ref, g_ref, b_ref, alpha), *rest, d_model=d_model)


def _project(x_f32, w_ref, cos_ref, sin_ref, bd_ref,
             x_out_ref, qk_ref, v_ref, sg_ref, z_ref, gates_ref, *, d_model):
    x_out_ref[...] = x_f32
    x = x_f32.astype(BF16)
    cos = cos_ref[...]
    sin = sin_ref[...]
    qk_w = 2 * N_HEADS * QK_DIM
    v_w = N_HEADS * V_PAD
    off_v = qk_w
    off_g = off_v + v_w
    off_f = off_g + v_w
    off_gr = off_f + F_WIDTH
    cw = 256

    def proj(c0):
        return jnp.dot(x, w_ref[:, c0:c0 + cw], preferred_element_type=F32)

    for c0 in range(0, qk_w, cw):
        acc = proj(c0)
        scale = QK_DIM ** -0.5 if c0 < N_HEADS * QK_DIM else 1.0
        parts = []
        for hh in range(cw // QK_DIM):
            a = acc[:, hh * QK_DIM:(hh + 1) * QK_DIM]
            rot = a * cos + pltpu.roll(a, QK_DIM // 2, axis=1) * sin
            if scale != 1.0:
                rot = rot * scale
            parts.append(rot.astype(BF16))
        qk_ref[:, c0:c0 + cw] = jnp.concatenate(parts, axis=1)
    for c0 in range(0, v_w, cw):
        v_ref[:, c0:c0 + cw] = proj(off_v + c0).astype(BF16)
    for c0 in range(0, v_w, cw):
        a = proj(off_g + c0)
        sg_ref[:, c0:c0 + cw] = (a * jax.nn.sigmoid(a)).astype(BF16)
    f = proj(off_f).astype(BF16)
    z_ref[...] = jnp.dot(f, bd_ref[...], preferred_element_type=F32).astype(BF16)
    for c0 in range(0, 2 * d_model, cw):
        gates_ref[:, c0:c0 + cw] = jax.nn.sigmoid(proj(off_gr + c0)).astype(BF16)


def _inproj(prologue, g, b, w_pad, cos_t, sin_t, bd, tm, seq, alpha):
    first = len(prologue) == 1
    t, d = prologue[-1].shape
    n_cols = w_pad.shape[1]
    pos_map = lambda i: (i % (seq // tm), 0)
    row = lambda w: pl.BlockSpec((tm, w), lambda i: (i, 0))
    vec = pl.BlockSpec((1, d), lambda i: (0, 0))
    outs = (jax.ShapeDtypeStruct((t, d), F32),
            jax.ShapeDtypeStruct((t, 2 * N_HEADS * QK_DIM), BF16),
            jax.ShapeDtypeStruct((t, N_HEADS * V_PAD), BF16),
            jax.ShapeDtypeStruct((t, N_HEADS * V_PAD), BF16),
            jax.ShapeDtypeStruct((t, 2 * F_WIDTH), BF16),
            jax.ShapeDtypeStruct((t, 2 * d), BF16))
    if first:
        body = functools.partial(_inproj_first_kernel, d_model=d)
        pro_specs = [row(d)]
    else:
        body = functools.partial(_inproj_next_kernel, alpha=alpha, d_model=d)
        pro_specs = [pl.BlockSpec((N_PIECES, 2, tm, PIECE_W), lambda i: (0, 0, i, 0)), row(ROUTE_LANES), row(d)]
    return pl.pallas_call(
        body,
        out_shape=outs,
        grid=(t // tm,),
        in_specs=pro_specs + [vec, vec,
                              pl.BlockSpec((d, n_cols), lambda i: (0, 0)),
                              pl.BlockSpec((tm, QK_DIM), pos_map),
                              pl.BlockSpec((tm, QK_DIM), pos_map),
                              pl.BlockSpec((F_WIDTH, 2 * F_WIDTH), lambda i: (0, 0))],
        out_specs=tuple(row(o.shape[1]) for o in outs),
        compiler_params=_cparams(("parallel",)),
        name="inproj",
    )(*prologue, g.reshape(1, d), b.reshape(1, d), w_pad, cos_t, sin_t, bd)


def _log_sigmoid(x):
    return jnp.minimum(x, 0.0) - jnp.log1p(jnp.exp(-jnp.abs(x)))


def _retention_kernel(dec_ref, q_ref, k_ref, v_ref, sg_ref, gng_ref, gnb_ref, o_ref,
                      r_all, rf_s, rb_s, *, n_chunks):
    h = pl.program_id(1)
    c = CHUNK
    lf = _log_sigmoid(jnp.full((1, c), dec_ref[0, h], F32))
    lb = _log_sigmoid(jnp.full((1, c), dec_ref[1, h], F32))
    lf_w = _log_sigmoid(jnp.full((1, V_PAD), dec_ref[0, h], F32))
    lb_w = _log_sigmoid(jnp.full((1, V_PAD), dec_ref[1, h], F32))
    row = lax.broadcasted_iota(jnp.int32, (c, c), 0).astype(F32)
    col = lax.broadcasted_iota(jnp.int32, (c, c), 1).astype(F32)
    diff = row - col
    dmat = jnp.where(diff >= 0.0, jnp.exp(lf * jnp.maximum(diff, 0.0)),
                     jnp.exp(lb * jnp.maximum(-diff, 0.0)))
    qd_f = jnp.exp(lf * (row + 1.0))
    kd_f = jnp.exp(lf * (c - 1.0 - row))
    qd_b = jnp.exp(lb * (c - row))
    kd_b = jnp.exp(lb * row)
    cf = jnp.exp(lf_w * float(c))
    cb = jnp.exp(lb_w * float(c))
    lane = lax.broadcasted_iota(jnp.int32, (c, V_PAD), 1)
    vmask = (lane < V_DIM).astype(F32)
    gn_g = gng_ref[...]
    gn_b = gnb_ref[...]
    tdot = (((0,), (0,)), ((), ()))

    rf_s[...] = jnp.zeros_like(rf_s)
    rb_s[...] = jnp.zeros_like(rb_s)

    def state_step(t, carry):
        for (ci, kd, cdec, r_s, half) in ((t, kd_f, cf, rf_s, 0), (n_chunks - 1 - t, kd_b, cb, rb_s, 1)):
            r0 = pl.multiple_of(ci * c, c)
            kk = (k_ref[pl.ds(r0, c), :].astype(F32) * kd).astype(BF16)
            kv = lax.dot_general(kk, v_ref[pl.ds(r0, c), :], tdot, preferred_element_type=F32)
            r = r_s[...]
            r_all[ci, half * c:(half + 1) * c, :] = r.astype(BF16)
            r_s[...] = cdec * r + kv
        return carry

    lax.fori_loop(0, n_chunks, state_step, 0, unroll=RET_UNROLL)

    def out_step(ci, carry):
        r0 = pl.multiple_of(ci * c, c)
        q = q_ref[pl.ds(r0, c), :]
        qf = q.astype(F32)
        s = lax.dot_general(q, k_ref[pl.ds(r0, c), :], (((1,), (1,)), ((), ())), preferred_element_type=F32)
        p = (s * dmat).astype(BF16)
        o = jnp.dot(p, v_ref[pl.ds(r0, c), :], preferred_element_type=F32)
        qq = jnp.concatenate([(qf * qd_f).astype(BF16), (qf * qd_b).astype(BF16)], axis=1)
        o = o + jnp.dot(qq, r_all[ci], preferred_element_type=F32)
        mu = jnp.sum(o, axis=-1, keepdims=True) * (1.0 / V_DIM)
        d = (o - mu) * vmask
        var = jnp.sum(d * d, axis=-1, keepdims=True) * (1.0 / V_DIM)
        y = d * lax.rsqrt(var + LN_EPS) * gn_g + gn_b
        o_ref[pl.ds(r0, c), :] = (y * sg_ref[pl.ds(r0, c), :].astype(F32)).astype(BF16)
        return carry

    lax.fori_loop(0, n_chunks, out_step, 0, unroll=RET_UNROLL)


def _retention(dec, qk, v, sg, gn_g, gn_b, n_batch, seq, row_block0):
    n_chunks = seq // CHUNK
    rb = lambda b: b + row_block0
    return pl.pallas_call(
        functools.partial(_retention_kernel, n_chunks=n_chunks),
        out_shape=jax.ShapeDtypeStruct((n_batch * seq, N_HEADS * V_PAD), BF16),
        grid_spec=pltpu.PrefetchScalarGridSpec(
            num_scalar_prefetch=0,
            grid=(n_batch, N_HEADS),
            in_specs=[pl.BlockSpec(memory_space=pltpu.SMEM),
                      pl.BlockSpec((seq, QK_DIM), lambda b, h: (rb(b), h)),
                      pl.BlockSpec((seq, QK_DIM), lambda b, h: (rb(b), N_HEADS + h)),
                      pl.BlockSpec((seq, V_PAD), lambda b, h: (rb(b), h)),
                      pl.BlockSpec((seq, V_PAD), lambda b, h: (rb(b), h)),
                      pl.BlockSpec((1, V_PAD), lambda b, h: (0, h)),
                      pl.BlockSpec((1, V_PAD), lambda b, h: (0, h))],
            out_specs=pl.BlockSpec((seq, V_PAD), lambda b, h: (b, h)),
            scratch_shapes=[pltpu.VMEM((n_chunks, 2 * CHUNK, V_PAD), BF16),
                            pltpu.VMEM((CHUNK, V_PAD), F32),
                            pltpu.VMEM((CHUNK, V_PAD), F32)]),
        compiler_params=_cparams(("parallel", "arbitrary")),
        name="retention",
    )(dec, qk, qk, v, sg, gn_g, gn_b)


def _seqdft_kernel(c_ref, s_ref, zc_ref, zs_ref, o_ref, acc_ref):
    kk = pl.program_id(1)
    b = pl.program_id(2)
    part = (jnp.dot(c_ref[...], zc_ref[...], preferred_element_type=F32)
            - jnp.dot(s_ref[...], zs_ref[...], preferred_element_type=F32))

    @pl.when(kk == 0)
    def _():
        acc_ref[b] = part

    @pl.when(kk > 0)
    def _():
        acc_ref[b] += part

    @pl.when(kk == pl.num_programs(1) - 1)
    def _():
        o_ref[b] = acc_ref[b].astype(o_ref.dtype)


def _seq_dft(cmat, smat, z, n_batch, seq, row_block0):
    tm = min(1024, seq)
    tk = min(1024, seq)
    z3 = z.reshape(z.shape[0] // seq, seq, z.shape[1])
    out = pl.pallas_call(
        _seqdft_kernel,
        out_shape=jax.ShapeDtypeStruct((n_batch, seq, F_WIDTH), BF16),
        grid=(seq // tm, seq // tk, n_batch),
        in_specs=[pl.BlockSpec((tm, tk), lambda i, k, b: (i, k)),
                  pl.BlockSpec((tm, tk), lambda i, k, b: (i, k)),
                  pl.BlockSpec((None, tk, F_WIDTH), lambda i, k, b: (b + row_block0, k, 0)),
                  pl.BlockSpec((None, tk, F_WIDTH), lambda i, k, b: (b + row_block0, k, 1))],
        out_specs=pl.BlockSpec((n_batch, tm, F_WIDTH), lambda i, k, b: (0, i, 0)),
        scratch_shapes=[pltpu.VMEM((n_batch, tm, F_WIDTH), F32)],
        compiler_params=_cparams(("parallel", "arbitrary", "arbitrary")),
        name="seq_dft",
    )(cmat, smat, z3, z3)
    return out.reshape(n_batch * seq, F_WIDTH)


def _merge_kernel(r_ref, fo_ref, gates_ref, x_ref, wr_ref, wf_ref, wo_ref, g1_ref, b1_ref,
                  wr2_ref, rb_ref, x1_ref, x1p_ref, route_ref, cnt_ref, *, alpha, d_model):
    for j in range(x_ref.shape[0] // RANK_TILE):
        rows = pl.ds(j * RANK_TILE, RANK_TILE)
        _merge_subtile(rows, j, r_ref, fo_ref, gates_ref, x_ref, wr_ref, wf_ref, wo_ref,
                       g1_ref, b1_ref, wr2_ref, rb_ref, x1_ref, x1p_ref, route_ref, cnt_ref,
                       alpha=alpha, d_model=d_model)


def _merge_subtile(rows, j, r_ref, fo_ref, gates_ref, x_ref, wr_ref, wf_ref, wo_ref,
                   g1_ref, b1_ref, wr2_ref, rb_ref, x1_ref, x1p_ref, route_ref, cnt_ref, *, alpha, d_model):
    a = jnp.dot(r_ref[rows, :], wr_ref[...], preferred_element_type=F32)
    b = jnp.dot(fo_ref[rows, :], wf_ref[...], preferred_element_type=F32)
    merged = (gates_ref[rows, :d_model].astype(F32) * a + gates_ref[rows, d_model:].astype(F32) * b)
    mix = jnp.dot(merged.astype(BF16), wo_ref[...], preferred_element_type=F32)
    x1 = _layer_norm_rows(alpha * x_ref[rows, :] + mix, g1_ref[...], b1_ref[...])
    x1_ref[rows, :] = x1
    packed = _pack_halves(x1)
    for s in range(N_PIECES):
        x1p_ref[s, rows, :] = packed[:, s * PIECE_W:(s + 1) * PIECE_W]

    xh = x1.astype(BF16)
    xl = (x1 - xh.astype(F32)).astype(BF16)
    s2 = (jnp.dot(xh, wr2_ref[...], preferred_element_type=F32)
          + jnp.dot(xl, wr2_ref[...], preferred_element_type=F32))
    logits = s2 + pltpu.roll(s2, ROUTE_LANES // 2, axis=1) + rb_ref[...]
    tm = logits.shape[0]
    lane = lax.broadcasted_iota(jnp.int32, (tm, ROUTE_LANES), 1)
    neg = jnp.float32(-jnp.inf)
    big = jnp.int32(ROUTE_LANES)

    def top1(mask):
        vals = jnp.where(mask, logits, neg)
        m = jnp.max(vals, axis=-1, keepdims=True)
        idx = jnp.min(jnp.where(mask & (vals == m), lane, big), axis=-1, keepdims=True)
        return m, idx

    gmask = lane < N_GROUPS
    gm, gi = top1(gmask)
    gsum = jnp.sum(jnp.where(gmask, jnp.exp(logits - gm), 0.0), axis=-1, keepdims=True)
    g_gate = 1.0 / gsum
    lo = N_GROUPS + gi * EXPERTS_PER_GROUP
    emask = (lane >= lo) & (lane < lo + EXPERTS_PER_GROUP)
    m1, i1 = top1(emask)
    m2, i2 = top1(emask & (lane != i1))
    e2 = jnp.exp(m2 - m1)
    w1 = 1.0 / (1.0 + e2)
    w2 = e2 / (1.0 + e2)
    oh1 = (lane == i1 - N_GROUPS).astype(F32)
    oh2 = (lane == i2 - N_GROUPS).astype(F32)
    earlier = (lax.broadcasted_iota(jnp.int32, (tm, tm), 1)
               < lax.broadcasted_iota(jnp.int32, (tm, tm), 0)).astype(BF16)
    tot1 = jnp.sum(oh1, axis=0, keepdims=True)
    tot2 = jnp.sum(oh2, axis=0, keepdims=True)
    pre1 = jnp.dot(earlier, oh1.astype(BF16), preferred_element_type=F32)
    pre2 = jnp.dot(earlier, oh2.astype(BF16), preferred_element_type=F32) + tot1
    r1 = jnp.sum(pre1 * oh1, axis=-1, keepdims=True)
    r2 = jnp.sum(pre2 * oh2, axis=-1, keepdims=True)
    cnt_ref[j] = tot1 + tot2

    rec = jnp.where(lane == 0, (i1 - N_GROUPS).astype(F32), 0.0)
    rec = jnp.where(lane == 1, (i2 - N_GROUPS).astype(F32), rec)
    rec = jnp.where(lane == 2, g_gate * w1, rec)
    rec = jnp.where(lane == 3, g_gate * w2, rec)
    rec = jnp.where(lane == 4, r1, rec)
    rec = jnp.where(lane == 5, r2, rec)
    route_ref[rows, :] = rec


def _merge(r, fo, gates, x, wr, wf, wo, g1, b1, wr2, rbias, alpha, tm):
    t, d = x.shape
    row = lambda w: pl.BlockSpec((tm, w), lambda i: (i, 0))
    full = lambda a: pl.BlockSpec(a.shape, lambda i: (0, 0))
    g1 = g1.reshape(1, d)
    b1 = b1.reshape(1, d)
    return pl.pallas_call(
        functools.partial(_merge_kernel, alpha=alpha, d_model=d),
        out_shape=(jax.ShapeDtypeStruct((t, d), F32),
                   jax.ShapeDtypeStruct((N_PIECES, t, PIECE_W), jnp.uint32),
                   jax.ShapeDtypeStruct((t, ROUTE_LANES), F32),
                   jax.ShapeDtypeStruct((t // RANK_TILE, 1, ROUTE_LANES), F32)),
        grid=(t // tm,),
        in_specs=[row(r.shape[1]), row(fo.shape[1]), row(gates.shape[1]), row(d),
                  full(wr), full(wf), full(wo), full(g1), full(b1), full(wr2), full(rbias)],
        out_specs=(row(d), pl.BlockSpec((N_PIECES, tm, PIECE_W), lambda i: (0, i, 0)), row(ROUTE_LANES),
                   pl.BlockSpec((tm // RANK_TILE, 1, ROUTE_LANES), lambda i: (i, 0, 0))),
        compiler_params=_cparams(("parallel",)),
        name="merge_router",
    )(r, fo, gates, x, wr, wf, wo, g1, b1, wr2, rbias)


def _sc_mesh():
    return plsc.VectorSubcoreMesh(core_axis_name="c", subcore_axis_name="s")


def _sc_gather_rows(table, idx):
    n, w = idx.shape[0], table.shape[1]
    idx = idx.reshape(1, n)

    @functools.partial(pl.kernel, out_type=jax.ShapeDtypeStruct((n, w), table.dtype), mesh=_sc_mesh(),
                       scratch_types=[])
    def gather(tab_hbm, idx_hbm, out_hbm):
        def body(idx_vmem, out_vmem):
            pltpu.sync_copy(tab_hbm.at[idx_vmem.at[0]], out_vmem)

        pltpu.emit_pipeline(
            body,
            grid=(n // SC_GATHER_WINDOW,),
            in_specs=[pl.BlockSpec((1, SC_GATHER_WINDOW), lambda i: (0, i))],
            out_specs=[pl.BlockSpec((SC_GATHER_WINDOW, w), lambda i: (i, 0))],
            core_axis_name=("c", "s"),
            dimension_semantics=(pltpu.PARALLEL,),
        )(idx_hbm, out_hbm)

    return gather(table, idx)


def _sc_scatter_rows(src, idx, n_out):
    assert idx.shape[0] == n_out
    n, w = n_out, src.shape[1]
    idx = idx.reshape(1, n)
    src_windows = src.shape[0] // SC_GATHER_WINDOW

    @functools.partial(pl.kernel, out_type=jax.ShapeDtypeStruct((n, w), src.dtype), mesh=_sc_mesh(),
                       scratch_types=[])
    def scatter(src_hbm, idx_hbm, out_hbm):
        def body(src_vmem, idx_vmem):
            pltpu.sync_copy(src_vmem, out_hbm.at[idx_vmem.at[0]])

        pltpu.emit_pipeline(
            body,
            grid=(n // SC_GATHER_WINDOW,),
            in_specs=[pl.BlockSpec((SC_GATHER_WINDOW, w), lambda i: (i % src_windows, 0)),
                      pl.BlockSpec((1, SC_GATHER_WINDOW), lambda i: (0, i))],
            out_specs=[],
            core_axis_name=("c", "s"),
            dimension_semantics=(pltpu.PARALLEL,),
        )(src_hbm, idx_hbm)

    return scatter(src, idx)


def _expert_kernel(te_ref, nv_ref, run_ref, xs_ref, wg_hbm, wu_hbm, wd_hbm, o_ref,
                   wg_f, wu_f, wd_f, wg_s, wu_s, wd_s, sems, *, slab0):
    i = pl.program_id(0)

    def weight_copies(expert, slot):
        return [pltpu.make_async_copy(hbm.at[slab0 + expert], buf.at[slot], sems.at[slot, k])
                for k, (hbm, buf) in enumerate(((wg_hbm, wg_f), (wu_hbm, wu_f), (wd_hbm, wd_f)))]

    @pl.when(i == 0)
    def _():
        for c in weight_copies(te_ref[0], 0):
            c.start()

    @pl.when(run_ref[0, i] == 1)
    def _():
        slot = run_ref[1, i]
        for c in weight_copies(te_ref[i], slot):
            c.wait()

        @pl.when(run_ref[2, i] >= 0)
        def _():
            for c in weight_copies(run_ref[2, i], 1 - slot):
                c.start()

        wg_s[...] = wg_f[slot].astype(BF16)
        wu_s[...] = wu_f[slot].astype(BF16)
        wd_s[...] = wd_f[slot].astype(BF16)

    @pl.when(i < nv_ref[0])
    def _():
        chunks = [c.astype(BF16) for c in _unpack_pieces([xs_ref[s] for s in range(N_PIECES)])]
        cw = chunks[0].shape[1]

        def up(w_s):
            acc = jnp.dot(chunks[0], w_s[:cw, :], preferred_element_type=F32)
            for j in range(1, len(chunks)):
                acc = acc + jnp.dot(chunks[j], w_s[j * cw:(j + 1) * cw, :], preferred_element_type=F32)
            return acc

        g = up(wg_s)
        hmid = (g * jax.nn.sigmoid(g) * up(wu_s)).astype(BF16)
        _store_pieces(o_ref, _pack_halves(jnp.dot(hmid, wd_s[...], preferred_element_type=F32)))

    @pl.when(i >= nv_ref[0])
    def _():
        o_ref[...] = jnp.zeros_like(o_ref)


def _expert_mlp(tile_expert, n_valid, xs, wg, wu, wd, layer, tm):
    p = xs.shape[1]
    n_tiles = p // tm
    d, de = wg.shape[1], wg.shape[2]
    tiles = jnp.arange(n_tiles, dtype=jnp.int32)
    prev = jnp.concatenate([tile_expert[:1], tile_expert[:-1]])
    is_start = (tiles == 0) | (tile_expert != prev)
    slot = (jnp.cumsum(is_start.astype(jnp.int32)) - 1) % 2
    start_at = jnp.where(is_start, tiles, n_tiles)
    next_start = jnp.concatenate([lax.cummin(start_at[::-1])[::-1][1:], jnp.full((1,), n_tiles, jnp.int32)])
    next_expert = jnp.where(next_start < n_tiles, tile_expert[jnp.minimum(next_start, n_tiles - 1)], -1)
    runs = jnp.stack([is_start.astype(jnp.int32), slot, next_expert]).astype(jnp.int32)

    xmap = lambda i, te, nv, run: (0, jnp.minimum(i, nv[0] - 1), 0)
    return pl.pallas_call(
        functools.partial(_expert_kernel, slab0=layer * N_EXPERTS),
        out_shape=jax.ShapeDtypeStruct(xs.shape, jnp.uint32),
        grid_spec=pltpu.PrefetchScalarGridSpec(
            num_scalar_prefetch=3,
            grid=(n_tiles,),
            in_specs=[pl.BlockSpec((N_PIECES, tm, PIECE_W), xmap),
                      pl.BlockSpec(memory_space=pl.ANY),
                      pl.BlockSpec(memory_space=pl.ANY),
                      pl.BlockSpec(memory_space=pl.ANY)],
            out_specs=pl.BlockSpec((N_PIECES, tm, PIECE_W), lambda i, te, nv, run: (0, i, 0)),
            scratch_shapes=[pltpu.VMEM((2, d, de), F32), pltpu.VMEM((2, d, de), F32), pltpu.VMEM((2, de, d), F32),
                            pltpu.VMEM((d, de), BF16), pltpu.VMEM((d, de), BF16), pltpu.VMEM((de, d), BF16),
                            pltpu.SemaphoreType.DMA((2, 3))]),
        compiler_params=_cparams(("arbitrary",)),
        name="expert_mlp",
    )(tile_expert, n_valid, runs, xs, wg, wu, wd)


def _combine_kernel(y_ref, route_ref, x_ref, g_ref, b_ref, o_ref, *, alpha):
    o_ref[...] = _combine_rows(y_ref, route_ref, x_ref, g_ref, b_ref, alpha)


def _combine(yg, route, x1, g2, b2, alpha, tm):
    n_rows, d = x1.shape
    row = lambda w: pl.BlockSpec((tm, w), lambda i: (i, 0))
    vec = pl.BlockSpec((1, d), lambda i: (0, 0))
    return pl.pallas_call(
        functools.partial(_combine_kernel, alpha=alpha),
        out_shape=jax.ShapeDtypeStruct((n_rows, d), F32),
        grid=(n_rows // tm,),
        in_specs=[pl.BlockSpec((N_PIECES, 2, tm, PIECE_W), lambda i: (0, 0, i, 0)),
                  row(ROUTE_LANES), row(d), vec, vec],
        out_specs=pl.BlockSpec((tm, d), lambda i: (i, 0)),
        compiler_params=_cparams(("parallel",)),
        name="combine_ln2",
    )(yg, route, x1, g2.reshape(1, d), b2.reshape(1, d))


def _rope_tables(seq):
    half = QK_DIM // 2
    inv_freq = 1.0 / (ROPE_BASE ** (jnp.arange(half, dtype=F32) / half))
    ang = jnp.arange(seq, dtype=F32)[:, None] * inv_freq[None, :]
    cos, sin = jnp.cos(ang), jnp.sin(ang)
    return jnp.concatenate([cos, cos], axis=1), jnp.concatenate([-sin, sin], axis=1)


def _channel_dft_matrix():
    n = np.arange(FG_DIM)
    ang = 2.0 * np.pi * np.outer(n, n) / FG_DIM
    eye = np.eye(N_FGROUPS)
    bd = np.concatenate([np.kron(eye, np.cos(ang)), np.kron(eye, np.sin(ang))], axis=1) / math.sqrt(FG_DIM)
    return jnp.asarray(bd, dtype=BF16)


def _dft_matrix_kernel(ch_ref, sh_ref, cl_ref, sl_ref, c_ref, s_ref):
    cl = cl_ref[...]
    sl = sl_ref[...]
    for blk in range(ch_ref.shape[1]):
        ch = ch_ref[:, blk:blk + 1]
        sh = sh_ref[:, blk:blk + 1]
        cols = slice(blk * LANE, (blk + 1) * LANE)
        c_ref[:, cols] = (ch * cl - sh * sl).astype(BF16)
        s_ref[:, cols] = (sh * cl + ch * sl).astype(BF16)


def _seq_dft_matrices(seq):
    k = jnp.arange(seq, dtype=jnp.int32)[:, None]
    hi = jnp.arange(seq // LANE, dtype=jnp.int32)[None, :] * LANE
    lo = jnp.arange(LANE, dtype=jnp.int32)[None, :]
    w = 2.0 * math.pi / seq
    a_hi = ((k * hi) % seq).astype(F32) * w
    a_lo = ((k * lo) % seq).astype(F32) * w
    scale = seq ** -0.5
    tm = min(256, seq)
    nb = seq // LANE
    small = lambda width: pl.BlockSpec((tm, width), lambda i: (i, 0))
    return pl.pallas_call(
        _dft_matrix_kernel,
        out_shape=(jax.ShapeDtypeStruct((seq, seq), BF16), jax.ShapeDtypeStruct((seq, seq), BF16)),
        grid=(seq // tm,),
        in_specs=[small(nb), small(nb), small(LANE), small(LANE)],
        out_specs=(small(seq), small(seq)),
        compiler_params=_cparams(("parallel",)),
        name="dft_matrices",
    )(jnp.cos(a_hi) * scale, jnp.sin(a_hi) * scale, jnp.cos(a_lo), jnp.sin(a_lo))


def _pad_heads(w, axis):
    shp = list(w.shape)
    shp[axis:axis + 1] = [N_HEADS, V_DIM]
    w = w.reshape(shp)
    pad = [(0, 0)] * w.ndim
    pad[axis + 1] = (0, V_PAD - V_DIM)
    w = jnp.pad(w, pad)
    shp[axis:axis + 2] = [N_HEADS * V_PAD]
    return w.reshape(shp)


def _pad_in_weight(w_in):
    qk_w = 2 * N_HEADS * QK_DIM
    vw = N_HEADS * V_DIM
    qk = w_in[:, :qk_w]
    v = _pad_heads(w_in[:, qk_w:qk_w + vw], 1)
    g = _pad_heads(w_in[:, qk_w + vw:qk_w + 2 * vw], 1)
    rest = w_in[:, qk_w + 2 * vw:]
    return jnp.concatenate([qk, v, g, rest], axis=1).astype(BF16)


def _routing_plan(route, tile_counts, tm_tok, n_tiles, moe_tile):
    t = route.shape[0]
    experts = jnp.arange(N_EXPERTS, dtype=jnp.int32)
    cnt = tile_counts[:, 0, :N_EXPERTS].astype(jnp.int32)
    before = jnp.cumsum(cnt, axis=0) - cnt
    counts = jnp.sum(cnt, axis=0)
    tiles_per = (counts + moe_tile - 1) // moe_tile
    tile_end = jnp.cumsum(tiles_per)
    start_row = (tile_end - tiles_per) * moe_tile
    n_valid = tile_end[-1]

    base = (start_row[None, :] + before).astype(F32)
    base_tok = jnp.broadcast_to(base[:, None, :], (t // tm_tok, tm_tok, N_EXPERTS)).reshape(t, N_EXPERTS)
    e = route[:, 0:2]
    rank = route[:, 4:6]
    sel = e[:, :, None] == experts.astype(F32)[None, None, :]
    pos = (jnp.sum(jnp.where(sel, base_tok[:, None, :], 0.0), axis=-1) + rank).astype(jnp.int32)

    n_pad = n_tiles * moe_tile - 2 * t
    pads = tiles_per * moe_tile - counts
    pad_end = jnp.cumsum(pads)
    p = jnp.arange(n_pad, dtype=jnp.int32)
    pe = jnp.minimum(jnp.sum((p[:, None] >= pad_end[None, :]).astype(jnp.int32), axis=1), N_EXPERTS - 1)
    in_expert = start_row[pe] + counts[pe] + (p - (pad_end[pe] - pads[pe]))
    pad_pos = jnp.where(p < pad_end[-1], in_expert, n_valid * moe_tile + (p - pad_end[-1]))

    tile_ids = jnp.arange(n_tiles, dtype=jnp.int32)
    tile_expert = jnp.minimum(jnp.sum((tile_ids[:, None] >= tile_end[None, :]).astype(jnp.int32), axis=1),
                              N_EXPERTS - 1)
    last_e = tile_expert[jnp.maximum(n_valid - 1, 0)]
    tile_expert = jnp.where(tile_ids < n_valid, tile_expert, last_e).astype(jnp.int32)
    return pos, pad_pos.astype(jnp.int32), tile_expert, n_valid.reshape(1).astype(jnp.int32)


def _largest_tile(cands, *dims):
    for c in cands:
        if all(d % c == 0 for d in dims):
            return c
    raise ValueError(f"no tile in {cands} divides {dims}")


def kernel(x_prompt, x_sample, ln_in_g, ln_in_b, w_in, ret_decay_fwd, ret_decay_bwd, ret_gn_g, ret_gn_b, w_ret_up, w_four_up, w_out, ln1_g, ln1_b, w_route_group, b_route_group, w_route_expert, b_route_expert, w_expert_gate, w_expert_up, w_expert_down, ln2_g, ln2_b):
    bp, sp, d = x_prompt.shape
    bs, ss, _ = x_sample.shape
    depth = w_in.shape[0]
    alpha = float((2 * depth) ** 0.25)

    cos_t, sin_t = _rope_tables(max(sp, ss))
    bd = _channel_dft_matrix()
    dft = {s: _seq_dft_matrices(s) for s in sorted({sp, ss})}
    wg_all = w_expert_gate.reshape((depth * N_EXPERTS,) + w_expert_gate.shape[2:])
    wu_all = w_expert_up.reshape((depth * N_EXPERTS,) + w_expert_up.shape[2:])
    wd_all = w_expert_down.reshape((depth * N_EXPERTS,) + w_expert_down.shape[2:])

    batches = []
    for xin in (x_prompt, x_sample):
        nb, seq, _ = xin.shape
        t = nb * seq
        assert seq % CHUNK == 0 and t % SC_GATHER_WINDOW == 0
        moe_tile = MOE_TILE if 2 * t >= 4 * MOE_TILE * N_EXPERTS else MOE_TILE // 2
        assert (2 * t) % moe_tile == 0
        batches.append(dict(nb=nb, seq=seq, t=t, moe_tile=moe_tile, n_tiles=(2 * t) // moe_tile + N_EXPERTS,
                            tm_in=_largest_tile((512, 256, 128), seq),
                            tm_tok=_largest_tile((TOKEN_TILE, 128), t),
                            tm_merge=_largest_tile((MERGE_TILE, RANK_TILE), t),
                            prologue=(xin.reshape(t, d),)))

    for l in range(depth):
        w_pad = _pad_in_weight(w_in[l])
        dec = jnp.stack([ret_decay_fwd[l], ret_decay_bwd[l]]).astype(F32)
        gn_g = _pad_heads(ret_gn_g[l], 0).reshape(1, -1).astype(F32)
        gn_b = _pad_heads(ret_gn_b[l], 0).reshape(1, -1).astype(F32)
        wr = _pad_heads(w_ret_up[l], 0).astype(BF16)
        wf = w_four_up[l].astype(BF16)
        wo = w_out[l].astype(BF16)
        w_rt = jnp.concatenate([w_route_group[l], w_route_expert[l]], axis=1)
        w_rt = jnp.pad(w_rt, ((0, 0), (0, ROUTE_LANES // 2 - w_rt.shape[1])))
        wrh = w_rt.astype(BF16)
        wr2 = jnp.concatenate([wrh, (w_rt - wrh.astype(F32)).astype(BF16)], axis=1)
        rbias = jnp.concatenate([b_route_group[l], b_route_expert[l]])
        rbias = jnp.pad(rbias, (0, ROUTE_LANES - rbias.shape[0])).reshape(1, ROUTE_LANES).astype(F32)
        pro_g, pro_b = (ln_in_g, ln_in_b) if l == 0 else (ln2_g[l - 1], ln2_b[l - 1])

        for bt in batches:
            nb, seq, t, moe_tile, n_tiles = bt["nb"], bt["seq"], bt["t"], bt["moe_tile"], bt["n_tiles"]
            x, qk, v, sg, z, gates = _inproj(bt["prologue"], pro_g, pro_b, w_pad, cos_t, sin_t, bd,
                                             bt["tm_in"], seq, alpha)
            r = _retention(dec, qk, v, sg, gn_g, gn_b, nb, seq, 0)
            fo = _seq_dft(dft[seq][0], dft[seq][1], z, nb, seq, 0)
            x1, x1p, route, tile_counts = _merge(r, fo, gates, x, wr, wf, wo, ln1_g[l], ln1_b[l], wr2, rbias,
                                                 alpha, bt["tm_merge"])

            pos, pad_pos, tile_expert, n_valid = _routing_plan(route, tile_counts, RANK_TILE, n_tiles, moe_tile)
            n_sorted = n_tiles * moe_tile
            piece_off = jnp.arange(N_PIECES, dtype=jnp.int32)[:, None] * n_sorted
            dst = jnp.concatenate([(piece_off + seg[None, :]).reshape(-1)
                                   for seg in (pos[:, 0], pos[:, 1], pad_pos)])
            xs = _sc_scatter_rows(x1p.reshape(N_PIECES * t, PIECE_W), dst, N_PIECES * n_sorted)
            ys = _expert_mlp(tile_expert, n_valid, xs.reshape(N_PIECES, n_sorted, PIECE_W),
                             wg_all, wu_all, wd_all, l, moe_tile)
            src = (piece_off[:, :, None] + pos.T[None, :, :]).reshape(-1)
            yg = _sc_gather_rows(ys.reshape(N_PIECES * n_sorted, PIECE_W), src).reshape(N_PIECES, 2, t, PIECE_W)
            bt["prologue"] = (yg, route, x1)

    outs = [_combine(*bt["prologue"], ln2_g[depth - 1], ln2_b[depth - 1], alpha, bt["tm_tok"]) for bt in batches]
    return outs[0].reshape(bp, sp, d), outs[1].reshape(bs, ss, d)
```

```python
import functools
import math

import jax
import jax.numpy as jnp
import numpy as np
from jax import lax
from jax.experimental import pallas as pl
from jax.experimental.pallas import tpu as pltpu
from jax.experimental.pallas import tpu_sc as plsc

N_HEADS = 4
QK_DIM = 128
V_DIM = 192
V_PAD = 256
CHUNK = 128
ROPE_BASE = 10000.0
N_FGROUPS = 4
FG_DIM = 64
F_WIDTH = N_FGROUPS * FG_DIM
N_GROUPS = 4
EXPERTS_PER_GROUP = 8
N_EXPERTS = N_GROUPS * EXPERTS_PER_GROUP
LN_EPS = 1e-5
ROUTE_LANES = 128

LANE = 128
VMEM_LIMIT_BYTES = 56 * 1024 * 1024
MOE_TILE = 512
GATHER_TILE = 256
SC_GATHER_WINDOW = 128
N_PIECES = 2
PIECE_W = 256
TOKEN_TILE = 256
MERGE_TILE = 512
RANK_TILE = 256
RET_UNROLL = 8

F32 = jnp.float32
BF16 = jnp.bfloat16


def _cparams(sem):
    return pltpu.CompilerParams(dimension_semantics=sem, vmem_limit_bytes=VMEM_LIMIT_BYTES)


def _layer_norm_rows(y, g, b):
    mu = jnp.mean(y, axis=-1, keepdims=True)
    d = y - mu
    var = jnp.mean(d * d, axis=-1, keepdims=True)
    return d * lax.rsqrt(var + LN_EPS) * g + b


def _pack_halves(y):
    n = y.shape[1] // 2
    lo = pltpu.bitcast(y[:, :n].astype(BF16).astype(F32), jnp.uint32)
    hi = pltpu.bitcast(y[:, n:].astype(BF16).astype(F32), jnp.uint32)
    return (hi & jnp.uint32(0xFFFF0000)) | (lo >> 16)


def _unpack_halves(w):
    lo = pltpu.bitcast(w << 16, F32)
    hi = pltpu.bitcast(w & jnp.uint32(0xFFFF0000), F32)
    return lo, hi


def _store_pieces(ref, packed):
    for s in range(N_PIECES):
        ref[s] = packed[:, s * PIECE_W:(s + 1) * PIECE_W]


def _unpack_pieces(pieces):
    halves = [_unpack_halves(p) for p in pieces]
    return [h[0] for h in halves] + [h[1] for h in halves]


def _combine_rows(y_ref, route_ref, x_ref, g_ref, b_ref, alpha):
    c1 = route_ref[:, 2:3]
    c2 = route_ref[:, 3:4]
    first = _unpack_pieces([y_ref[s, 0] for s in range(N_PIECES)])
    second = _unpack_pieces([y_ref[s, 1] for s in range(N_PIECES)])
    ff = jnp.concatenate([c1 * a + c2 * b for a, b in zip(first, second)], axis=1)
    return _layer_norm_rows(alpha * x_ref[...] + ff, g_ref[...], b_ref[...])


def _inproj_first_kernel(xraw_ref, g_ref, b_ref, *rest, d_model):
    _project(_layer_norm_rows(xraw_ref[...], g_ref[...], b_ref[...]), *rest, d_model=d_model)


def _inproj_next_kernel(y_ref, route_ref, x1_ref, g_ref, b_ref, *rest, alpha, d_model):
    _project(_combine_rows(y_ref, route_ref, x1_ref, g_ref, b_ref, alpha), *rest, d_model=d_model)


def _project(x_f32, w_ref, cos_ref, sin_ref, bd_ref,
             x_out_ref, qk_ref, v_ref, sg_ref, z_ref, gates_ref, *, d_model):
    x_out_ref[...] = x_f32
    x = x_f32.astype(BF16)
    cos = cos_ref[...]
    sin = sin_ref[...]
    qk_w = 2 * N_HEADS * QK_DIM
    v_w = N_HEADS * V_PAD
    off_v = qk_w
    off_g = off_v + v_w
    off_f = off_g + v_w
    off_gr = off_f + F_WIDTH
    cw = 256

    def proj(c0):
        return jnp.dot(x, w_ref[:, c0:c0 + cw], preferred_element_type=F32)

    for c0 in range(0, qk_w, cw):
        acc = proj(c0)
        scale = QK_DIM ** -0.5 if c0 < N_HEADS * QK_DIM else 1.0
        parts = []
        for hh in range(cw // QK_DIM):
            a = acc[:, hh * QK_DIM:(hh + 1) * QK_DIM]
            rot = a * cos + pltpu.roll(a, QK_DIM // 2, axis=1) * sin
            if scale != 1.0:
                rot = rot * scale
            parts.append(rot.astype(BF16))
        qk_ref[:, c0:c0 + cw] = jnp.concatenate(parts, axis=1)
    for c0 in range(0, v_w, cw):
        v_ref[:, c0:c0 + cw] = proj(off_v + c0).astype(BF16)
    for c0 in range(0, v_w, cw):
        a = proj(off_g + c0)
        sg_ref[:, c0:c0 + cw] = (a * jax.nn.sigmoid(a)).astype(BF16)
    f = proj(off_f).astype(BF16)
    z_ref[...] = jnp.dot(f, bd_ref[...], preferred_element_type=F32).astype(BF16)
    for c0 in range(0, 2 * d_model, cw):
        gates_ref[:, c0:c0 + cw] = jax.nn.sigmoid(proj(off_gr + c0)).astype(BF16)


def _inproj(prologue, g, b, w_pad, cos_t, sin_t, bd, tm, seq, alpha):
    first = len(prologue) == 1
    t, d = prologue[-1].shape
    n_cols = w_pad.shape[1]
    pos_map = lambda i: (i % (seq // tm), 0)
    row = lambda w: pl.BlockSpec((tm, w), lambda i: (i, 0))
    vec = pl.BlockSpec((1, d), lambda i: (0, 0))
    outs = (jax.ShapeDtypeStruct((t, d), F32),
            jax.ShapeDtypeStruct((t, 2 * N_HEADS * QK_DIM), BF16),
            jax.ShapeDtypeStruct((t, N_HEADS * V_PAD), BF16),
            jax.ShapeDtypeStruct((t, N_HEADS * V_PAD), BF16),
            jax.ShapeDtypeStruct((t, 2 * F_WIDTH), BF16),
            jax.ShapeDtypeStruct((t, 2 * d), BF16))
    if first:
        body = functools.partial(_inproj_first_kernel, d_model=d)
        pro_specs = [row(d)]
    else:
        body = functools.partial(_inproj_next_kernel, alpha=alpha, d_model=d)
        pro_specs = [pl.BlockSpec((N_PIECES, 2, tm, PIECE_W), lambda i: (0, 0, i, 0)), row(ROUTE_LANES), row(d)]
    return pl.pallas_call(
        body,
        out_shape=outs,
        grid=(t // tm,),
        in_specs=pro_specs + [vec, vec,
                              pl.BlockSpec((d, n_cols), lambda i: (0, 0)),
                              pl.BlockSpec((tm, QK_DIM), pos_map),
                              pl.BlockSpec((tm, QK_DIM), pos_map),
                              pl.BlockSpec((F_WIDTH, 2 * F_WIDTH), lambda i: (0, 0))],
        out_specs=tuple(row(o.shape[1]) for o in outs),
        compiler_params=_cparams(("parallel",)),
        name="inproj",
    )(*prologue, g.reshape(1, d), b.reshape(1, d), w_pad, cos_t, sin_t, bd)


def _log_sigmoid(x):
    return jnp.minimum(x, 0.0) - jnp.log1p(jnp.exp(-jnp.abs(x)))


def _retention_kernel(dec_ref, q_ref, k_ref, v_ref, sg_ref, gng_ref, gnb_ref, o_ref,
                      r_all, rf_s, rb_s, *, n_chunks):
    h = pl.program_id(1)
    c = CHUNK
    lf = _log_sigmoid(jnp.full((1, c), dec_ref[0, h], F32))
    lb = _log_sigmoid(jnp.full((1, c), dec_ref[1, h], F32))
    lf_w = _log_sigmoid(jnp.full((1, V_PAD), dec_ref[0, h], F32))
    lb_w = _log_sigmoid(jnp.full((1, V_PAD), dec_ref[1, h], F32))
    row = lax.broadcasted_iota(jnp.int32, (c, c), 0).astype(F32)
    col = lax.broadcasted_iota(jnp.int32, (c, c), 1).astype(F32)
    diff = row - col
    dmat = jnp.where(diff >= 0.0, jnp.exp(lf * jnp.maximum(diff, 0.0)),
                     jnp.exp(lb * jnp.maximum(-diff, 0.0)))
    qd_f = jnp.exp(lf * (row + 1.0))
    kd_f = jnp.exp(lf * (c - 1.0 - row))
    qd_b = jnp.exp(lb * (c - row))
    kd_b = jnp.exp(lb * row)
    cf = jnp.exp(lf_w * float(c))
    cb = jnp.exp(lb_w * float(c))
    gn_g = gng_ref[...]
    gn_b = gnb_ref[...]
    tdot = (((0,), (0,)), ((), ()))

    rf_s[...] = jnp.zeros_like(rf_s)
    rb_s[...] = jnp.zeros_like(rb_s)

    def state_step(t, carry):
        for (ci, kd, cdec, r_s, half) in ((t, kd_f, cf, rf_s, 0), (n_chunks - 1 - t, kd_b, cb, rb_s, 1)):
            r0 = pl.multiple_of(ci * c, c)
            kk = (k_ref[pl.ds(r0, c), :].astype(F32) * kd).astype(BF16)
            kv = lax.dot_general(kk, v_ref[pl.ds(r0, c), :], tdot, preferred_element_type=F32)
            r = r_s[...]
            r_all[ci, half * c:(half + 1) * c, :] = r.astype(BF16)
            r_s[...] = cdec * r + kv
        return carry

    lax.fori_loop(0, n_chunks, state_step, 0, unroll=RET_UNROLL)

    def out_step(ci, carry):
        r0 = pl.multiple_of(ci * c, c)
        q = q_ref[pl.ds(r0, c), :]
        qf = q.astype(F32)
        s = lax.dot_general(q, k_ref[pl.ds(r0, c), :], (((1,), (1,)), ((), ())), preferred_element_type=F32)
        p = (s * dmat).astype(BF16)
        o = jnp.dot(p, v_ref[pl.ds(r0, c), :], preferred_element_type=F32)
        qq = jnp.concatenate([(qf * qd_f).astype(BF16), (qf * qd_b).astype(BF16)], axis=1)
        o = o + jnp.dot(qq, r_all[ci], preferred_element_type=F32)
        mu = jnp.sum(o, axis=-1, keepdims=True) * (1.0 / V_DIM)
        d = o - mu
        var = (jnp.sum(d * d, axis=-1, keepdims=True) - float(V_PAD - V_DIM) * mu * mu) * (1.0 / V_DIM)
        y = d * lax.rsqrt(var + LN_EPS) * gn_g + gn_b
        o_ref[pl.ds(r0, c), :] = (y * sg_ref[pl.ds(r0, c), :].astype(F32)).astype(BF16)
        return carry

    lax.fori_loop(0, n_chunks, out_step, 0, unroll=RET_UNROLL)


def _retention(dec, qk, v, sg, gn_g, gn_b, n_batch, seq, row_block0):
    n_chunks = seq // CHUNK
    rb = lambda b: b + row_block0
    return pl.pallas_call(
        functools.partial(_retention_kernel, n_chunks=n_chunks),
        out_shape=jax.ShapeDtypeStruct((n_batch * seq, N_HEADS * V_PAD), BF16),
        grid_spec=pltpu.PrefetchScalarGridSpec(
            num_scalar_prefetch=0,
            grid=(n_batch, N_HEADS),
            in_specs=[pl.BlockSpec(memory_space=pltpu.SMEM),
                      pl.BlockSpec((seq, QK_DIM), lambda b, h: (rb(b), h)),
                      pl.BlockSpec((seq, QK_DIM), lambda b, h: (rb(b), N_HEADS + h)),
                      pl.BlockSpec((seq, V_PAD), lambda b, h: (rb(b), h)),
                      pl.BlockSpec((seq, V_PAD), lambda b, h: (rb(b), h)),
                      pl.BlockSpec((1, V_PAD), lambda b, h: (0, h)),
                      pl.BlockSpec((1, V_PAD), lambda b, h: (0, h))],
            out_specs=pl.BlockSpec((seq, V_PAD), lambda b, h: (b, h)),
            scratch_shapes=[pltpu.VMEM((n_chunks, 2 * CHUNK, V_PAD), BF16),
                            pltpu.VMEM((CHUNK, V_PAD), F32),
                            pltpu.VMEM((CHUNK, V_PAD), F32)]),
        compiler_params=_cparams(("parallel", "arbitrary")),
        name="retention",
    )(dec, qk, qk, v, sg, gn_g, gn_b)


def _seqdft_kernel(c_ref, s_ref, zc_ref, zs_ref, rc_ref, rs_ref, zmid_ref, o_ref, acc_ref, *, scale):
    kk = pl.program_id(1)
    b = pl.program_id(2)
    fold_c = (zc_ref[...].astype(F32) + rc_ref[...].astype(F32)).astype(BF16)
    fold_s = (zs_ref[...].astype(F32) - rs_ref[...].astype(F32)).astype(BF16)
    part = (jnp.dot(c_ref[...], fold_c, preferred_element_type=F32)
            - jnp.dot(s_ref[...], fold_s, preferred_element_type=F32))

    @pl.when(kk == 0)
    def _():
        acc_ref[b] = part

    @pl.when(kk > 0)
    def _():
        acc_ref[b] += part

    @pl.when(kk == pl.num_programs(1) - 1)
    def _():
        tm = acc_ref.shape[1]
        odd = lax.broadcasted_iota(jnp.int32, (tm, F_WIDTH), 0) % 2 == 1
        mid = zmid_ref[b, 0:1, :].astype(F32) * scale
        o_ref[b] = (acc_ref[b] + jnp.where(odd, -mid, mid)).astype(o_ref.dtype)


def _seq_dft(cmat, smat, z, n_batch, seq):
    half = seq // 2
    tm = min(1024, seq)
    tk = min(1024, half)
    assert tm % 2 == 0
    z3 = z.reshape(n_batch, seq, z.shape[1])
    zrev = jnp.concatenate([jnp.zeros((n_batch, 1, z.shape[1]), z.dtype), jnp.flip(z3[:, half + 1:], axis=1)], axis=1)
    zmid = jnp.broadcast_to(z3[:, half:half + 1, :F_WIDTH], (n_batch, 8, F_WIDTH))
    zblock = lambda part: pl.BlockSpec((None, tk, F_WIDTH), lambda i, k, b: (b, k, part))
    out = pl.pallas_call(
        functools.partial(_seqdft_kernel, scale=seq ** -0.5),
        out_shape=jax.ShapeDtypeStruct((n_batch, seq, F_WIDTH), BF16),
        grid=(seq // tm, half // tk, n_batch),
        in_specs=[pl.BlockSpec((tm, tk), lambda i, k, b: (i, k)),
                  pl.BlockSpec((tm, tk), lambda i, k, b: (i, k)),
                  zblock(0), zblock(1), zblock(0), zblock(1),
                  pl.BlockSpec((n_batch, 8, F_WIDTH), lambda i, k, b: (0, 0, 0))],
        out_specs=pl.BlockSpec((n_batch, tm, F_WIDTH), lambda i, k, b: (0, i, 0)),
        scratch_shapes=[pltpu.VMEM((n_batch, tm, F_WIDTH), F32)],
        compiler_params=_cparams(("parallel", "arbitrary", "arbitrary")),
        name="seq_dft",
    )(cmat, smat, z3, z3, zrev, zrev, zmid)
    return out.reshape(n_batch * seq, F_WIDTH)


def _merge_kernel(r_ref, fo_ref, gates_ref, x_ref, wr_ref, wf_ref, wo_ref, g1_ref, b1_ref,
                  wr2_ref, rb_ref, x1_ref, x1p_ref, route_ref, cnt_ref, *, alpha, d_model):
    for j in range(x_ref.shape[0] // RANK_TILE):
        rows = pl.ds(j * RANK_TILE, RANK_TILE)
        _merge_subtile(rows, j, r_ref, fo_ref, gates_ref, x_ref, wr_ref, wf_ref, wo_ref,
                       g1_ref, b1_ref, wr2_ref, rb_ref, x1_ref, x1p_ref, route_ref, cnt_ref,
                       alpha=alpha, d_model=d_model)


def _merge_subtile(rows, j, r_ref, fo_ref, gates_ref, x_ref, wr_ref, wf_ref, wo_ref,
                   g1_ref, b1_ref, wr2_ref, rb_ref, x1_ref, x1p_ref, route_ref, cnt_ref, *, alpha, d_model):
    a = jnp.dot(r_ref[rows, :], wr_ref[...], preferred_element_type=F32)
    b = jnp.dot(fo_ref[rows, :], wf_ref[...], preferred_element_type=F32)
    merged = (gates_ref[rows, :d_model].astype(F32) * a + gates_ref[rows, d_model:].astype(F32) * b)
    mix = jnp.dot(merged.astype(BF16), wo_ref[...], preferred_element_type=F32)
    x1 = _layer_norm_rows(alpha * x_ref[rows, :] + mix, g1_ref[...], b1_ref[...])
    x1_ref[rows, :] = x1
    packed = _pack_halves(x1)
    for s in range(N_PIECES):
        x1p_ref[s, rows, :] = packed[:, s * PIECE_W:(s + 1) * PIECE_W]

    xh = x1.astype(BF16)
    xl = (x1 - xh.astype(F32)).astype(BF16)
    s2 = (jnp.dot(xh, wr2_ref[...], preferred_element_type=F32)
          + jnp.dot(xl, wr2_ref[...], preferred_element_type=F32))
    logits = s2 + pltpu.roll(s2, ROUTE_LANES // 2, axis=1) + rb_ref[...]
    tm = logits.shape[0]
    lane = lax.broadcasted_iota(jnp.int32, (tm, ROUTE_LANES), 1)
    neg = jnp.float32(-jnp.inf)
    big = jnp.int32(ROUTE_LANES)

    def top1(mask):
        vals = jnp.where(mask, logits, neg)
        m = jnp.max(vals, axis=-1, keepdims=True)
        idx = jnp.min(jnp.where(mask & (vals == m), lane, big), axis=-1, keepdims=True)
        return m, idx

    gmask = lane < N_GROUPS
    gm, gi = top1(gmask)
    gsum = jnp.sum(jnp.where(gmask, jnp.exp(logits - gm), 0.0), axis=-1, keepdims=True)
    g_gate = 1.0 / gsum
    lo = N_GROUPS + gi * EXPERTS_PER_GROUP
    emask = (lane >= lo) & (lane < lo + EXPERTS_PER_GROUP)
    m1, i1 = top1(emask)
    m2, i2 = top1(emask & (lane != i1))
    e2 = jnp.exp(m2 - m1)
    w1 = 1.0 / (1.0 + e2)
    w2 = e2 / (1.0 + e2)
    oh1 = (lane == i1 - N_GROUPS).astype(F32)
    oh2 = (lane == i2 - N_GROUPS).astype(F32)
    earlier = (lax.broadcasted_iota(jnp.int32, (tm, tm), 1)
               < lax.broadcasted_iota(jnp.int32, (tm, tm), 0)).astype(BF16)
    tot1 = jnp.sum(oh1, axis=0, keepdims=True)
    tot2 = jnp.sum(oh2, axis=0, keepdims=True)
    pre1 = jnp.dot(earlier, oh1.astype(BF16), preferred_element_type=F32)
    pre2 = jnp.dot(earlier, oh2.astype(BF16), preferred_element_type=F32) + tot1
    r1 = jnp.sum(pre1 * oh1, axis=-1, keepdims=True)
    r2 = jnp.sum(pre2 * oh2, axis=-1, keepdims=True)
    cnt_ref[j] = tot1 + tot2

    rec = jnp.where(lane == 0, (i1 - N_GROUPS).astype(F32), 0.0)
    rec = jnp.where(lane == 1, (i2 - N_GROUPS).astype(F32), rec)
    rec = jnp.where(lane == 2, g_gate * w1, rec)
    rec = jnp.where(lane == 3, g_gate * w2, rec)
    rec = jnp.where(lane == 4, r1, rec)
    rec = jnp.where(lane == 5, r2, rec)
    route_ref[rows, :] = rec


def _merge(r, fo, gates, x, wr, wf, wo, g1, b1, wr2, rbias, alpha, tm):
    t, d = x.shape
    row = lambda w: pl.BlockSpec((tm, w), lambda i: (i, 0))
    full = lambda a: pl.BlockSpec(a.shape, lambda i: (0, 0))
    g1 = g1.reshape(1, d)
    b1 = b1.reshape(1, d)
    return pl.pallas_call(
        functools.partial(_merge_kernel, alpha=alpha, d_model=d),
        out_shape=(jax.ShapeDtypeStruct((t, d), F32),
                   jax.ShapeDtypeStruct((N_PIECES, t, PIECE_W), jnp.uint32),
                   jax.ShapeDtypeStruct((t, ROUTE_LANES), F32),
                   jax.ShapeDtypeStruct((t // RANK_TILE, 1, ROUTE_LANES), F32)),
        grid=(t // tm,),
        in_specs=[row(r.shape[1]), row(fo.shape[1]), row(gates.shape[1]), row(d),
                  full(wr), full(wf), full(wo), full(g1), full(b1), full(wr2), full(rbias)],
        out_specs=(row(d), pl.BlockSpec((N_PIECES, tm, PIECE_W), lambda i: (0, i, 0)), row(ROUTE_LANES),
                   pl.BlockSpec((tm // RANK_TILE, 1, ROUTE_LANES), lambda i: (i, 0, 0))),
        compiler_params=_cparams(("parallel",)),
        name="merge_router",
    )(r, fo, gates, x, wr, wf, wo, g1, b1, wr2, rbias)


def _sc_mesh():
    return plsc.VectorSubcoreMesh(core_axis_name="c", subcore_axis_name="s")


def _sc_gather_rows(table, idx):
    n, w = idx.shape[0], table.shape[1]
    idx = idx.reshape(1, n)

    @functools.partial(pl.kernel, out_type=jax.ShapeDtypeStruct((n, w), table.dtype), mesh=_sc_mesh(),
                       scratch_types=[])
    def gather(tab_hbm, idx_hbm, out_hbm):
        def body(idx_vmem, out_vmem):
            pltpu.sync_copy(tab_hbm.at[idx_vmem.at[0]], out_vmem)

        pltpu.emit_pipeline(
            body,
            grid=(n // SC_GATHER_WINDOW,),
            in_specs=[pl.BlockSpec((1, SC_GATHER_WINDOW), lambda i: (0, i))],
            out_specs=[pl.BlockSpec((SC_GATHER_WINDOW, w), lambda i: (i, 0))],
            core_axis_name=("c", "s"),
            dimension_semantics=(pltpu.PARALLEL,),
        )(idx_hbm, out_hbm)

    return gather(table, idx)


def _sc_scatter_rows(src, idx, n_out):
    assert idx.shape[0] == n_out
    n, w = n_out, src.shape[1]
    idx = idx.reshape(1, n)
    src_windows = src.shape[0] // SC_GATHER_WINDOW

    @functools.partial(pl.kernel, out_type=jax.ShapeDtypeStruct((n, w), src.dtype), mesh=_sc_mesh(),
                       scratch_types=[])
    def scatter(src_hbm, idx_hbm, out_hbm):
        def body(src_vmem, idx_vmem):
            pltpu.sync_copy(src_vmem, out_hbm.at[idx_vmem.at[0]])

        pltpu.emit_pipeline(
            body,
            grid=(n // SC_GATHER_WINDOW,),
            in_specs=[pl.BlockSpec((SC_GATHER_WINDOW, w), lambda i: (i % src_windows, 0)),
                      pl.BlockSpec((1, SC_GATHER_WINDOW), lambda i: (0, i))],
            out_specs=[],
            core_axis_name=("c", "s"),
            dimension_semantics=(pltpu.PARALLEL,),
        )(src_hbm, idx_hbm)

    return scatter(src, idx)


def _expert_kernel(te_ref, nv_ref, run_ref, xs_ref, wg_hbm, wu_hbm, wd_hbm, o_ref,
                   wg_f, wu_f, wd_f, wg_s, wu_s, wd_s, sems, *, slab0):
    i = pl.program_id(0)

    def weight_copies(expert, slot):
        return [pltpu.make_async_copy(hbm.at[slab0 + expert], buf.at[slot], sems.at[slot, k])
                for k, (hbm, buf) in enumerate(((wg_hbm, wg_f), (wu_hbm, wu_f), (wd_hbm, wd_f)))]

    @pl.when(i == 0)
    def _():
        for c in weight_copies(te_ref[0], 0):
            c.start()

    @pl.when(run_ref[0, i] == 1)
    def _():
        slot = run_ref[1, i]
        for c in weight_copies(te_ref[i], slot):
            c.wait()

        @pl.when(run_ref[2, i] >= 0)
        def _():
            for c in weight_copies(run_ref[2, i], 1 - slot):
                c.start()

        wg_s[...] = wg_f[slot].astype(BF16)
        wu_s[...] = wu_f[slot].astype(BF16)
        wd_s[...] = wd_f[slot].astype(BF16)

    @pl.when(i < nv_ref[0])
    def _():
        chunks = [c.astype(BF16) for c in _unpack_pieces([xs_ref[s] for s in range(N_PIECES)])]
        cw = chunks[0].shape[1]

        def up(w_s):
            acc = jnp.dot(chunks[0], w_s[:cw, :], preferred_element_type=F32)
            for j in range(1, len(chunks)):
                acc = acc + jnp.dot(chunks[j], w_s[j * cw:(j + 1) * cw, :], preferred_element_type=F32)
            return acc

        g = up(wg_s)
        hmid = (g * jax.nn.sigmoid(g) * up(wu_s)).astype(BF16)
        _store_pieces(o_ref, _pack_halves(jnp.dot(hmid, wd_s[...], preferred_element_type=F32)))

    @pl.when(i >= nv_ref[0])
    def _():
        o_ref[...] = jnp.zeros_like(o_ref)


def _expert_mlp(tile_expert, n_valid, xs, wg, wu, wd, layer, tm):
    p = xs.shape[1]
    n_tiles = p // tm
    d, de = wg.shape[1], wg.shape[2]
    tiles = jnp.arange(n_tiles, dtype=jnp.int32)
    prev = jnp.concatenate([tile_expert[:1], tile_expert[:-1]])
    is_start = (tiles == 0) | (tile_expert != prev)
    slot = (jnp.cumsum(is_start.astype(jnp.int32)) - 1) % 2
    start_at = jnp.where(is_start, tiles, n_tiles)
    next_start = jnp.concatenate([lax.cummin(start_at[::-1])[::-1][1:], jnp.full((1,), n_tiles, jnp.int32)])
    next_expert = jnp.where(next_start < n_tiles, tile_expert[jnp.minimum(next_start, n_tiles - 1)], -1)
    runs = jnp.stack([is_start.astype(jnp.int32), slot, next_expert]).astype(jnp.int32)

    xmap = lambda i, te, nv, run: (0, jnp.minimum(i, nv[0] - 1), 0)
    return pl.pallas_call(
        functools.partial(_expert_kernel, slab0=layer * N_EXPERTS),
        out_shape=jax.ShapeDtypeStruct(xs.shape, jnp.uint32),
        grid_spec=pltpu.PrefetchScalarGridSpec(
            num_scalar_prefetch=3,
            grid=(n_tiles,),
            in_specs=[pl.BlockSpec((N_PIECES, tm, PIECE_W), xmap),
                      pl.BlockSpec(memory_space=pl.ANY),
                      pl.BlockSpec(memory_space=pl.ANY),
                      pl.BlockSpec(memory_space=pl.ANY)],
            out_specs=pl.BlockSpec((N_PIECES, tm, PIECE_W), lambda i, te, nv, run: (0, i, 0)),
            scratch_shapes=[pltpu.VMEM((2, d, de), F32), pltpu.VMEM((2, d, de), F32), pltpu.VMEM((2, de, d), F32),
                            pltpu.VMEM((d, de), BF16), pltpu.VMEM((d, de), BF16), pltpu.VMEM((de, d), BF16),
                            pltpu.SemaphoreType.DMA((2, 3))]),
        compiler_params=_cparams(("arbitrary",)),
        name="expert_mlp",
    )(tile_expert, n_valid, runs, xs, wg, wu, wd)


def _combine_kernel(y_ref, route_ref, x_ref, g_ref, b_ref, o_ref, *, alpha):
    o_ref[...] = _combine_rows(y_ref, route_ref, x_ref, g_ref, b_ref, alpha)


def _combine(yg, route, x1, g2, b2, alpha, tm):
    n_rows, d = x1.shape
    row = lambda w: pl.BlockSpec((tm, w), lambda i: (i, 0))
    vec = pl.BlockSpec((1, d), lambda i: (0, 0))
    return pl.pallas_call(
        functools.partial(_combine_kernel, alpha=alpha),
        out_shape=jax.ShapeDtypeStruct((n_rows, d), F32),
        grid=(n_rows // tm,),
        in_specs=[pl.BlockSpec((N_PIECES, 2, tm, PIECE_W), lambda i: (0, 0, i, 0)),
                  row(ROUTE_LANES), row(d), vec, vec],
        out_specs=pl.BlockSpec((tm, d), lambda i: (i, 0)),
        compiler_params=_cparams(("parallel",)),
        name="combine_ln2",
    )(yg, route, x1, g2.reshape(1, d), b2.reshape(1, d))


def _rope_tables(seq):
    half = QK_DIM // 2
    inv_freq = 1.0 / (ROPE_BASE ** (jnp.arange(half, dtype=F32) / half))
    ang = jnp.arange(seq, dtype=F32)[:, None] * inv_freq[None, :]
    cos, sin = jnp.cos(ang), jnp.sin(ang)
    return jnp.concatenate([cos, cos], axis=1), jnp.concatenate([-sin, sin], axis=1)


def _channel_dft_matrix():
    n = np.arange(FG_DIM)
    ang = 2.0 * np.pi * np.outer(n, n) / FG_DIM
    eye = np.eye(N_FGROUPS)
    bd = np.concatenate([np.kron(eye, np.cos(ang)), np.kron(eye, np.sin(ang))], axis=1) / math.sqrt(FG_DIM)
    return jnp.asarray(bd, dtype=BF16)


def _dft_matrix_kernel(ch_ref, sh_ref, cl_ref, sl_ref, c_ref, s_ref):
    cl = cl_ref[...]
    sl = sl_ref[...]
    for blk in range(ch_ref.shape[1]):
        ch = ch_ref[:, blk:blk + 1]
        sh = sh_ref[:, blk:blk + 1]
        cols = slice(blk * LANE, (blk + 1) * LANE)
        c_ref[:, cols] = (ch * cl - sh * sl).astype(BF16)
        s_ref[:, cols] = (sh * cl + ch * sl).astype(BF16)


def _seq_dft_matrices(seq):
    half = seq // 2
    assert half % LANE == 0
    k = jnp.arange(seq, dtype=jnp.int32)[:, None]
    hi = jnp.arange(half // LANE, dtype=jnp.int32)[None, :] * LANE
    lo = jnp.arange(LANE, dtype=jnp.int32)[None, :]
    w = 2.0 * math.pi / seq
    a_hi = ((k * hi) % seq).astype(F32) * w
    a_lo = ((k * lo) % seq).astype(F32) * w
    scale = seq ** -0.5
    tm = min(256, seq)
    nb = half // LANE
    small = lambda width: pl.BlockSpec((tm, width), lambda i: (i, 0))
    return pl.pallas_call(
        _dft_matrix_kernel,
        out_shape=(jax.ShapeDtypeStruct((seq, half), BF16), jax.ShapeDtypeStruct((seq, half), BF16)),
        grid=(seq // tm,),
        in_specs=[small(nb), small(nb), small(LANE), small(LANE)],
        out_specs=(small(half), small(half)),
        compiler_params=_cparams(("parallel",)),
        name="dft_matrices",
    )(jnp.cos(a_hi) * scale, jnp.sin(a_hi) * scale, jnp.cos(a_lo), jnp.sin(a_lo))


def _pad_heads(w, axis):
    shp = list(w.shape)
    shp[axis:axis + 1] = [N_HEADS, V_DIM]
    w = w.reshape(shp)
    pad = [(0, 0)] * w.ndim
    pad[axis + 1] = (0, V_PAD - V_DIM)
    w = jnp.pad(w, pad)
    shp[axis:axis + 2] = [N_HEADS * V_PAD]
    return w.reshape(shp)


def _pad_in_weight(w_in):
    qk_w = 2 * N_HEADS * QK_DIM
    vw = N_HEADS * V_DIM
    qk = w_in[:, :qk_w]
    v = _pad_heads(w_in[:, qk_w:qk_w + vw], 1)
    g = _pad_heads(w_in[:, qk_w + vw:qk_w + 2 * vw], 1)
    rest = w_in[:, qk_w + 2 * vw:]
    return jnp.concatenate([qk, v, g, rest], axis=1).astype(BF16)


def _routing_plan(route, tile_counts, tm_tok, n_tiles, moe_tile):
    t = route.shape[0]
    experts = jnp.arange(N_EXPERTS, dtype=jnp.int32)
    cnt = tile_counts[:, 0, :N_EXPERTS].astype(jnp.int32)
    before = jnp.cumsum(cnt, axis=0) - cnt
    counts = jnp.sum(cnt, axis=0)
    tiles_per = (counts + moe_tile - 1) // moe_tile
    tile_end = jnp.cumsum(tiles_per)
    start_row = (tile_end - tiles_per) * moe_tile
    n_valid = tile_end[-1]

    base = (start_row[None, :] + before).astype(F32)
    base_tok = jnp.broadcast_to(base[:, None, :], (t // tm_tok, tm_tok, N_EXPERTS)).reshape(t, N_EXPERTS)
    e = route[:, 0:2]
    rank = route[:, 4:6]
    sel = e[:, :, None] == experts.astype(F32)[None, None, :]
    pos = (jnp.sum(jnp.where(sel, base_tok[:, None, :], 0.0), axis=-1) + rank).astype(jnp.int32)

    n_pad = n_tiles * moe_tile - 2 * t
    pads = tiles_per * moe_tile - counts
    pad_end = jnp.cumsum(pads)
    p = jnp.arange(n_pad, dtype=jnp.int32)
    pe = jnp.minimum(jnp.sum((p[:, None] >= pad_end[None, :]).astype(jnp.int32), axis=1), N_EXPERTS - 1)
    in_expert = start_row[pe] + counts[pe] + (p - (pad_end[pe] - pads[pe]))
    pad_pos = jnp.where(p < pad_end[-1], in_expert, n_valid * moe_tile + (p - pad_end[-1]))

    tile_ids = jnp.arange(n_tiles, dtype=jnp.int32)
    tile_expert = jnp.minimum(jnp.sum((tile_ids[:, None] >= tile_end[None, :]).astype(jnp.int32), axis=1),
                              N_EXPERTS - 1)
    last_e = tile_expert[jnp.maximum(n_valid - 1, 0)]
    tile_expert = jnp.where(tile_ids < n_valid, tile_expert, last_e).astype(jnp.int32)
    return pos, pad_pos.astype(jnp.int32), tile_expert, n_valid.reshape(1).astype(jnp.int32)


def _largest_tile(cands, *dims):
    for c in cands:
        if all(d % c == 0 for d in dims):
            return c
    raise ValueError(f"no tile in {cands} divides {dims}")


def kernel(x_prompt, x_sample, ln_in_g, ln_in_b, w_in, ret_decay_fwd, ret_decay_bwd, ret_gn_g, ret_gn_b, w_ret_up, w_four_up, w_out, ln1_g, ln1_b, w_route_group, b_route_group, w_route_expert, b_route_expert, w_expert_gate, w_expert_up, w_expert_down, ln2_g, ln2_b):
    bp, sp, d = x_prompt.shape
    bs, ss, _ = x_sample.shape
    depth = w_in.shape[0]
    alpha = float((2 * depth) ** 0.25)

    cos_t, sin_t = _rope_tables(max(sp, ss))
    bd = _channel_dft_matrix()
    dft = {s: _seq_dft_matrices(s) for s in sorted({sp, ss})}
    wg_all = w_expert_gate.reshape((depth * N_EXPERTS,) + w_expert_gate.shape[2:])
    wu_all = w_expert_up.reshape((depth * N_EXPERTS,) + w_expert_up.shape[2:])
    wd_all = w_expert_down.reshape((depth * N_EXPERTS,) + w_expert_down.shape[2:])

    batches = []
    for xin in (x_prompt, x_sample):
        nb, seq, _ = xin.shape
        t = nb * seq
        assert seq % CHUNK == 0 and t % SC_GATHER_WINDOW == 0
        moe_tile = MOE_TILE if 2 * t >= 4 * MOE_TILE * N_EXPERTS else MOE_TILE // 2
        assert (2 * t) % moe_tile == 0
        batches.append(dict(nb=nb, seq=seq, t=t, moe_tile=moe_tile, n_tiles=(2 * t) // moe_tile + N_EXPERTS,
                            tm_in=_largest_tile((512, 256, 128), seq),
                            tm_tok=_largest_tile((TOKEN_TILE, 128), t),
                            tm_merge=_largest_tile((MERGE_TILE, RANK_TILE), t),
                            prologue=(xin.reshape(t, d),)))

    for l in range(depth):
        w_pad = _pad_in_weight(w_in[l])
        dec = jnp.stack([ret_decay_fwd[l], ret_decay_bwd[l]]).astype(F32)
        gn_g = _pad_heads(ret_gn_g[l], 0).reshape(1, -1).astype(F32)
        gn_b = _pad_heads(ret_gn_b[l], 0).reshape(1, -1).astype(F32)
        wr = _pad_heads(w_ret_up[l], 0).astype(BF16)
        wf = w_four_up[l].astype(BF16)
        wo = w_out[l].astype(BF16)
        w_rt = jnp.concatenate([w_route_group[l], w_route_expert[l]], axis=1)
        w_rt = jnp.pad(w_rt, ((0, 0), (0, ROUTE_LANES // 2 - w_rt.shape[1])))
        wrh = w_rt.astype(BF16)
        wr2 = jnp.concatenate([wrh, (w_rt - wrh.astype(F32)).astype(BF16)], axis=1)
        rbias = jnp.concatenate([b_route_group[l], b_route_expert[l]])
        rbias = jnp.pad(rbias, (0, ROUTE_LANES - rbias.shape[0])).reshape(1, ROUTE_LANES).astype(F32)
        pro_g, pro_b = (ln_in_g, ln_in_b) if l == 0 else (ln2_g[l - 1], ln2_b[l - 1])

        for bt in batches:
            nb, seq, t, moe_tile, n_tiles = bt["nb"], bt["seq"], bt["t"], bt["moe_tile"], bt["n_tiles"]
            x, qk, v, sg, z, gates = _inproj(bt["prologue"], pro_g, pro_b, w_pad, cos_t, sin_t, bd,
                                             bt["tm_in"], seq, alpha)
            r = _retention(dec, qk, v, sg, gn_g, gn_b, nb, seq, 0)
            fo = _seq_dft(dft[seq][0], dft[seq][1], z, nb, seq)
            x1, x1p, route, tile_counts = _merge(r, fo, gates, x, wr, wf, wo, ln1_g[l], ln1_b[l], wr2, rbias,
                                                 alpha, bt["tm_merge"])

            pos, pad_pos, tile_expert, n_valid = _routing_plan(route, tile_counts, RANK_TILE, n_tiles, moe_tile)
            n_sorted = n_tiles * moe_tile
            piece_off = jnp.arange(N_PIECES, dtype=jnp.int32)[:, None] * n_sorted
            dst = jnp.concatenate([(piece_off + seg[None, :]).reshape(-1)
                                   for seg in (pos[:, 0], pos[:, 1], pad_pos)])
            xs = _sc_scatter_rows(x1p.reshape(N_PIECES * t, PIECE_W), dst, N_PIECES * n_sorted)
            ys = _expert_mlp(tile_expert, n_valid, xs.reshape(N_PIECES, n_sorted, PIECE_W),
                             wg_all, wu_all, wd_all, l, moe_tile)
            src = (piece_off[:, :, None] + pos.T[None, :, :]).reshape(-1)
            yg = _sc_gather_rows(ys.reshape(N_PIECES * n_sorted, PIECE_W), src).reshape(N_PIECES, 2, t, PIECE_W)
            bt["prologue"] = (yg, route, x1)

    outs = [_combine(*bt["prologue"], ln2_g[depth - 1], ln2_b[depth - 1], alpha, bt["tm_tok"]) for bt in batches]
    return outs[0].reshape(bp, sp, d), outs[1].reshape(bs, ss, d)
```

```python
import functools
import math

import jax
import jax.numpy as jnp
import numpy as np
from jax import lax
from jax.experimental import pallas as pl
from jax.experimental.pallas import tpu as pltpu
from jax.experimental.pallas import tpu_sc as plsc

N_HEADS = 4
QK_DIM = 128
V_DIM = 192
V_PAD = 256
CHUNK = 128
ROPE_BASE = 10000.0
N_FGROUPS = 4
FG_DIM = 64
F_WIDTH = N_FGROUPS * FG_DIM
N_GROUPS = 4
EXPERTS_PER_GROUP = 8
N_EXPERTS = N_GROUPS * EXPERTS_PER_GROUP
LN_EPS = 1e-5
ROUTE_LANES = 128

LANE = 128
VMEM_LIMIT_BYTES = 56 * 1024 * 1024
MOE_TILE = 512
GATHER_TILE = 256
SC_GATHER_WINDOW = 128
N_PIECES = 2
PIECE_W = 256
TOKEN_TILE = 256
MERGE_TILE = 512
RANK_TILE = 256
RET_UNROLL = 8

F32 = jnp.float32
BF16 = jnp.bfloat16


def _cparams(sem):
    return pltpu.CompilerParams(dimension_semantics=sem, vmem_limit_bytes=VMEM_LIMIT_BYTES)


def _layer_norm_rows(y, g, b):
    mu = jnp.mean(y, axis=-1, keepdims=True)
    d = y - mu
    var = jnp.mean(d * d, axis=-1, keepdims=True)
    return d * lax.rsqrt(var + LN_EPS) * g + b


def _pack_halves(y):
    n = y.shape[1] // 2
    lo = pltpu.bitcast(y[:, :n].astype(BF16).astype(F32), jnp.uint32)
    hi = pltpu.bitcast(y[:, n:].astype(BF16).astype(F32), jnp.uint32)
    return (hi & jnp.uint32(0xFFFF0000)) | (lo >> 16)


def _unpack_halves(w):
    lo = pltpu.bitcast(w << 16, F32)
    hi = pltpu.bitcast(w & jnp.uint32(0xFFFF0000), F32)
    return lo, hi


def _store_pieces(ref, packed):
    for s in range(N_PIECES):
        ref[s] = packed[:, s * PIECE_W:(s + 1) * PIECE_W]


def _unpack_pieces(pieces):
    halves = [_unpack_halves(p) for p in pieces]
    return [h[0] for h in halves] + [h[1] for h in halves]


def _combine_rows(y_ref, route_ref, x_ref, g_ref, b_ref, alpha):
    c1 = route_ref[:, 2:3]
    c2 = route_ref[:, 3:4]
    first = _unpack_pieces([y_ref[s, 0] for s in range(N_PIECES)])
    second = _unpack_pieces([y_ref[s, 1] for s in range(N_PIECES)])
    ff = jnp.concatenate([c1 * a + c2 * b for a, b in zip(first, second)], axis=1)
    return _layer_norm_rows(alpha * x_ref[...] + ff, g_ref[...], b_ref[...])


def _inproj_first_kernel(xraw_ref, g_ref, b_ref, *rest, d_model):
    _project(_layer_norm_rows(xraw_ref[...], g_ref[...], b_ref[...]), *rest, d_model=d_model)


def _inproj_next_kernel(y_ref, route_ref, x1_ref, g_ref, b_ref, *rest, alpha, d_model):
    _project(_combine_rows(y_ref, route_ref, x1_ref, g_ref, b_ref, alpha), *rest, d_model=d_model)


def _project(x_f32, w_ref, cos_ref, sin_ref, bd_ref,
             x_out_ref, qk_ref, v_ref, sg_ref, z_ref, gates_ref, *, d_model):
    x_out_ref[...] = x_f32
    x = x_f32.astype(BF16)
    cos = cos_ref[...]
    sin = sin_ref[...]
    qk_w = 2 * N_HEADS * QK_DIM
    v_w = N_HEADS * V_PAD
    off_v = qk_w
    off_g = off_v + v_w
    off_f = off_g + v_w
    off_gr = off_f + F_WIDTH
    cw = 256

    def proj(c0):
        return jnp.dot(x, w_ref[:, c0:c0 + cw], preferred_element_type=F32)

    for c0 in range(0, qk_w, cw):
        acc = proj(c0)
        scale = QK_DIM ** -0.5 if c0 < N_HEADS * QK_DIM else 1.0
        parts = []
        for hh in range(cw // QK_DIM):
            a = acc[:, hh * QK_DIM:(hh + 1) * QK_DIM]
            rot = a * cos + pltpu.roll(a, QK_DIM // 2, axis=1) * sin
            if scale != 1.0:
                rot = rot * scale
            parts.append(rot.astype(BF16))
        qk_ref[:, c0:c0 + cw] = jnp.concatenate(parts, axis=1)
    for c0 in range(0, v_w, cw):
        v_ref[:, c0:c0 + cw] = proj(off_v + c0).astype(BF16)
    for c0 in range(0, v_w, cw):
        a = proj(off_g + c0)
        sg_ref[:, c0:c0 + cw] = (a * jax.nn.sigmoid(a)).astype(BF16)
    f = proj(off_f).astype(BF16)
    z_ref[...] = _pack_halves(jnp.dot(f, bd_ref[...], preferred_element_type=F32))
    for c0 in range(0, 2 * d_model, cw):
        gates_ref[:, c0:c0 + cw] = jax.nn.sigmoid(proj(off_gr + c0)).astype(BF16)


def _inproj(prologue, g, b, w_pad, cos_t, sin_t, bd, tm, seq, alpha):
    first = len(prologue) == 1
    t, d = prologue[-1].shape
    n_cols = w_pad.shape[1]
    pos_map = lambda i: (i % (seq // tm), 0)
    row = lambda w: pl.BlockSpec((tm, w), lambda i: (i, 0))
    vec = pl.BlockSpec((1, d), lambda i: (0, 0))
    outs = (jax.ShapeDtypeStruct((t, d), F32),
            jax.ShapeDtypeStruct((t, 2 * N_HEADS * QK_DIM), BF16),
            jax.ShapeDtypeStruct((t, N_HEADS * V_PAD), BF16),
            jax.ShapeDtypeStruct((t, N_HEADS * V_PAD), BF16),
            jax.ShapeDtypeStruct((t, F_WIDTH), jnp.uint32),
            jax.ShapeDtypeStruct((t, 2 * d), BF16))
    if first:
        body = functools.partial(_inproj_first_kernel, d_model=d)
        pro_specs = [row(d)]
    else:
        body = functools.partial(_inproj_next_kernel, alpha=alpha, d_model=d)
        pro_specs = [pl.BlockSpec((N_PIECES, 2, tm, PIECE_W), lambda i: (0, 0, i, 0)), row(ROUTE_LANES), row(d)]
    return pl.pallas_call(
        body,
        out_shape=outs,
        grid=(t // tm,),
        in_specs=pro_specs + [vec, vec,
                              pl.BlockSpec((d, n_cols), lambda i: (0, 0)),
                              pl.BlockSpec((tm, QK_DIM), pos_map),
                              pl.BlockSpec((tm, QK_DIM), pos_map),
                              pl.BlockSpec((F_WIDTH, 2 * F_WIDTH), lambda i: (0, 0))],
        out_specs=tuple(row(o.shape[1]) for o in outs),
        compiler_params=_cparams(("parallel",)),
        name="inproj",
    )(*prologue, g.reshape(1, d), b.reshape(1, d), w_pad, cos_t, sin_t, bd)


def _log_sigmoid(x):
    return jnp.minimum(x, 0.0) - jnp.log1p(jnp.exp(-jnp.abs(x)))


def _retention_kernel(dec_ref, q_ref, k_ref, v_ref, sg_ref, gng_ref, gnb_ref, o_ref,
                      r_all, rf_s, rb_s, *, n_chunks):
    h = pl.program_id(1)
    c = CHUNK
    lf = _log_sigmoid(jnp.full((1, c), dec_ref[0, h], F32))
    lb = _log_sigmoid(jnp.full((1, c), dec_ref[1, h], F32))
    lf_w = _log_sigmoid(jnp.full((1, V_PAD), dec_ref[0, h], F32))
    lb_w = _log_sigmoid(jnp.full((1, V_PAD), dec_ref[1, h], F32))
    row = lax.broadcasted_iota(jnp.int32, (c, c), 0).astype(F32)
    col = lax.broadcasted_iota(jnp.int32, (c, c), 1).astype(F32)
    diff = row - col
    dmat = jnp.where(diff >= 0.0, jnp.exp(lf * jnp.maximum(diff, 0.0)),
                     jnp.exp(lb * jnp.maximum(-diff, 0.0)))
    qd_f = jnp.exp(lf * (row + 1.0))
    kd_f = jnp.exp(lf * (c - 1.0 - row))
    qd_b = jnp.exp(lb * (c - row))
    kd_b = jnp.exp(lb * row)
    cf = jnp.exp(lf_w * float(c))
    cb = jnp.exp(lb_w * float(c))
    gn_g = gng_ref[...]
    gn_b = gnb_ref[...]
    tdot = (((0,), (0,)), ((), ()))

    rf_s[...] = jnp.zeros_like(rf_s)
    rb_s[...] = jnp.zeros_like(rb_s)

    def state_step(t, carry):
        for (ci, kd, cdec, r_s, half) in ((t, kd_f, cf, rf_s, 0), (n_chunks - 1 - t, kd_b, cb, rb_s, 1)):
            r0 = pl.multiple_of(ci * c, c)
            kk = (k_ref[pl.ds(r0, c), :].astype(F32) * kd).astype(BF16)
            kv = lax.dot_general(kk, v_ref[pl.ds(r0, c), :], tdot, preferred_element_type=F32)
            r = r_s[...]
            r_all[ci, half * c:(half + 1) * c, :] = r.astype(BF16)
            r_s[...] = cdec * r + kv
        return carry

    lax.fori_loop(0, n_chunks, state_step, 0, unroll=RET_UNROLL)

    def out_step(ci, carry):
        r0 = pl.multiple_of(ci * c, c)
        q = q_ref[pl.ds(r0, c), :]
        qf = q.astype(F32)
        s = lax.dot_general(q, k_ref[pl.ds(r0, c), :], (((1,), (1,)), ((), ())), preferred_element_type=F32)
        p = (s * dmat).astype(BF16)
        o = jnp.dot(p, v_ref[pl.ds(r0, c), :], preferred_element_type=F32)
        qq = jnp.concatenate([(qf * qd_f).astype(BF16), (qf * qd_b).astype(BF16)], axis=1)
        o = o + jnp.dot(qq, r_all[ci], preferred_element_type=F32)
        mu = jnp.sum(o, axis=-1, keepdims=True) * (1.0 / V_DIM)
        d = o - mu
        var = (jnp.sum(d * d, axis=-1, keepdims=True) - float(V_PAD - V_DIM) * mu * mu) * (1.0 / V_DIM)
        y = d * lax.rsqrt(var + LN_EPS) * gn_g + gn_b
        o_ref[pl.ds(r0, c), :] = (y * sg_ref[pl.ds(r0, c), :].astype(F32)).astype(BF16)
        return carry

    lax.fori_loop(0, n_chunks, out_step, 0, unroll=RET_UNROLL)


def _retention(dec, qk, v, sg, gn_g, gn_b, n_batch, seq, row_block0):
    n_chunks = seq // CHUNK
    rb = lambda b: b + row_block0
    return pl.pallas_call(
        functools.partial(_retention_kernel, n_chunks=n_chunks),
        out_shape=jax.ShapeDtypeStruct((n_batch * seq, N_HEADS * V_PAD), BF16),
        grid_spec=pltpu.PrefetchScalarGridSpec(
            num_scalar_prefetch=0,
            grid=(n_batch, N_HEADS),
            in_specs=[pl.BlockSpec(memory_space=pltpu.SMEM),
                      pl.BlockSpec((seq, QK_DIM), lambda b, h: (rb(b), h)),
                      pl.BlockSpec((seq, QK_DIM), lambda b, h: (rb(b), N_HEADS + h)),
                      pl.BlockSpec((seq, V_PAD), lambda b, h: (rb(b), h)),
                      pl.BlockSpec((seq, V_PAD), lambda b, h: (rb(b), h)),
                      pl.BlockSpec((1, V_PAD), lambda b, h: (0, h)),
                      pl.BlockSpec((1, V_PAD), lambda b, h: (0, h))],
            out_specs=pl.BlockSpec((seq, V_PAD), lambda b, h: (b, h)),
            scratch_shapes=[pltpu.VMEM((n_chunks, 2 * CHUNK, V_PAD), BF16),
                            pltpu.VMEM((CHUNK, V_PAD), F32),
                            pltpu.VMEM((CHUNK, V_PAD), F32)]),
        compiler_params=_cparams(("parallel", "arbitrary")),
        name="retention",
    )(dec, qk, qk, v, sg, gn_g, gn_b)


def _seqdft_kernel(c_ref, s_ref, z_ref, zrev_ref, zmid_ref, o_ref, acc_ref, *, scale):
    kk = pl.program_id(1)
    b = pl.program_id(2)
    zc, zs = _unpack_halves(z_ref[...])
    rc, rs = _unpack_halves(zrev_ref[...])
    no_partner = (lax.broadcasted_iota(jnp.int32, zc.shape, 0) == 0) & (kk == 0)
    fold_c = (zc + jnp.where(no_partner, 0.0, rc)).astype(BF16)
    fold_s = (zs - jnp.where(no_partner, 0.0, rs)).astype(BF16)
    part = (jnp.dot(c_ref[...], fold_c, preferred_element_type=F32)
            - jnp.dot(s_ref[...], fold_s, preferred_element_type=F32))

    @pl.when(kk == 0)
    def _():
        acc_ref[b] = part

    @pl.when(kk > 0)
    def _():
        acc_ref[b] += part

    @pl.when(kk == pl.num_programs(1) - 1)
    def _():
        tm = acc_ref.shape[1]
        odd = lax.broadcasted_iota(jnp.int32, (tm, F_WIDTH), 0) % 2 == 1
        mid = zmid_ref[b, 0:1, :] * scale
        o_ref[b] = (acc_ref[b] + jnp.where(odd, -mid, mid)).astype(o_ref.dtype)


def _seq_dft(cmat, smat, z, n_batch, seq):
    half = seq // 2
    tm = min(1024, seq)
    tk = min(1024, half)
    assert tm % 2 == 0 and (n_batch * half) % SC_GATHER_WINDOW == 0
    n = jnp.arange(half, dtype=jnp.int32)[None, :]
    partner = (jnp.arange(n_batch, dtype=jnp.int32)[:, None] * seq + (seq - n) % seq).reshape(-1)
    zrev = _sc_gather_rows(z, partner).reshape(n_batch, half, F_WIDTH)
    z3 = z.reshape(n_batch, seq, F_WIDTH)
    zmid = lax.bitcast_convert_type(z3[:, half, :] << 16, F32)
    zmid = jnp.broadcast_to(zmid[:, None, :], (n_batch, 8, F_WIDTH))
    zblock = pl.BlockSpec((None, tk, F_WIDTH), lambda i, k, b: (b, k, 0))
    out = pl.pallas_call(
        functools.partial(_seqdft_kernel, scale=seq ** -0.5),
        out_shape=jax.ShapeDtypeStruct((n_batch, seq, F_WIDTH), BF16),
        grid=(seq // tm, half // tk, n_batch),
        in_specs=[pl.BlockSpec((tm, tk), lambda i, k, b: (i, k)),
                  pl.BlockSpec((tm, tk), lambda i, k, b: (i, k)),
                  zblock, zblock,
                  pl.BlockSpec((n_batch, 8, F_WIDTH), lambda i, k, b: (0, 0, 0))],
        out_specs=pl.BlockSpec((n_batch, tm, F_WIDTH), lambda i, k, b: (0, i, 0)),
        scratch_shapes=[pltpu.VMEM((n_batch, tm, F_WIDTH), F32)],
        compiler_params=_cparams(("parallel", "arbitrary", "arbitrary")),
        name="seq_dft",
    )(cmat, smat, z3, zrev, zmid)
    return out.reshape(n_batch * seq, F_WIDTH)


def _merge_kernel(r_ref, fo_ref, gates_ref, x_ref, wr_ref, wf_ref, wo_ref, g1_ref, b1_ref,
                  wr2_ref, rb_ref, x1_ref, x1p_ref, route_ref, cnt_ref, *, alpha, d_model):
    for j in range(x_ref.shape[0] // RANK_TILE):
        rows = pl.ds(j * RANK_TILE, RANK_TILE)
        _merge_subtile(rows, j, r_ref, fo_ref, gates_ref, x_ref, wr_ref, wf_ref, wo_ref,
                       g1_ref, b1_ref, wr2_ref, rb_ref, x1_ref, x1p_ref, route_ref, cnt_ref,
                       alpha=alpha, d_model=d_model)


def _merge_subtile(rows, j, r_ref, fo_ref, gates_ref, x_ref, wr_ref, wf_ref, wo_ref,
                   g1_ref, b1_ref, wr2_ref, rb_ref, x1_ref, x1p_ref, route_ref, cnt_ref, *, alpha, d_model):
    a = jnp.dot(r_ref[rows, :], wr_ref[...], preferred_element_type=F32)
    b = jnp.dot(fo_ref[rows, :], wf_ref[...], preferred_element_type=F32)
    merged = (gates_ref[rows, :d_model].astype(F32) * a + gates_ref[rows, d_model:].astype(F32) * b)
    mix = jnp.dot(merged.astype(BF16), wo_ref[...], preferred_element_type=F32)
    x1 = _layer_norm_rows(alpha * x_ref[rows, :] + mix, g1_ref[...], b1_ref[...])
    x1_ref[rows, :] = x1
    packed = _pack_halves(x1)
    for s in range(N_PIECES):
        x1p_ref[s, rows, :] = packed[:, s * PIECE_W:(s + 1) * PIECE_W]

    xh = x1.astype(BF16)
    xl = (x1 - xh.astype(F32)).astype(BF16)
    s2 = (jnp.dot(xh, wr2_ref[...], preferred_element_type=F32)
          + jnp.dot(xl, wr2_ref[...], preferred_element_type=F32))
    logits = s2 + pltpu.roll(s2, ROUTE_LANES // 2, axis=1) + rb_ref[...]
    tm = logits.shape[0]
    lane = lax.broadcasted_iota(jnp.int32, (tm, ROUTE_LANES), 1)
    neg = jnp.float32(-jnp.inf)
    big = jnp.int32(ROUTE_LANES)

    def top1(mask):
        vals = jnp.where(mask, logits, neg)
        m = jnp.max(vals, axis=-1, keepdims=True)
        idx = jnp.min(jnp.where(mask & (vals == m), lane, big), axis=-1, keepdims=True)
        return m, idx

    gmask = lane < N_GROUPS
    gm, gi = top1(gmask)
    gsum = jnp.sum(jnp.where(gmask, jnp.exp(logits - gm), 0.0), axis=-1, keepdims=True)
    g_gate = 1.0 / gsum
    lo = N_GROUPS + gi * EXPERTS_PER_GROUP
    emask = (lane >= lo) & (lane < lo + EXPERTS_PER_GROUP)
    m1, i1 = top1(emask)
    m2, i2 = top1(emask & (lane != i1))
    e2 = jnp.exp(m2 - m1)
    w1 = 1.0 / (1.0 + e2)
    w2 = e2 / (1.0 + e2)
    oh1 = (lane == i1 - N_GROUPS).astype(F32)
    oh2 = (lane == i2 - N_GROUPS).astype(F32)
    earlier = (lax.broadcasted_iota(jnp.int32, (tm, tm), 1)
               < lax.broadcasted_iota(jnp.int32, (tm, tm), 0)).astype(BF16)
    tot1 = jnp.sum(oh1, axis=0, keepdims=True)
    tot2 = jnp.sum(oh2, axis=0, keepdims=True)
    pre1 = jnp.dot(earlier, oh1.astype(BF16), preferred_element_type=F32)
    pre2 = jnp.dot(earlier, oh2.astype(BF16), preferred_element_type=F32) + tot1
    r1 = jnp.sum(pre1 * oh1, axis=-1, keepdims=True)
    r2 = jnp.sum(pre2 * oh2, axis=-1, keepdims=True)
    cnt_ref[j] = tot1 + tot2

    rec = jnp.where(lane == 0, (i1 - N_GROUPS).astype(F32), 0.0)
    rec = jnp.where(lane == 1, (i2 - N_GROUPS).astype(F32), rec)
    rec = jnp.where(lane == 2, g_gate * w1, rec)
    rec = jnp.where(lane == 3, g_gate * w2, rec)
    rec = jnp.where(lane == 4, r1, rec)
    rec = jnp.where(lane == 5, r2, rec)
    route_ref[rows, :] = rec


def _merge(r, fo, gates, x, wr, wf, wo, g1, b1, wr2, rbias, alpha, tm):
    t, d = x.shape
    row = lambda w: pl.BlockSpec((tm, w), lambda i: (i, 0))
    full = lambda a: pl.BlockSpec(a.shape, lambda i: (0, 0))
    g1 = g1.reshape(1, d)
    b1 = b1.reshape(1, d)
    return pl.pallas_call(
        functools.partial(_merge_kernel, alpha=alpha, d_model=d),
        out_shape=(jax.ShapeDtypeStruct((t, d), F32),
                   jax.ShapeDtypeStruct((N_PIECES, t, PIECE_W), jnp.uint32),
                   jax.ShapeDtypeStruct((t, ROUTE_LANES), F32),
                   jax.ShapeDtypeStruct((t // RANK_TILE, 1, ROUTE_LANES), F32)),
        grid=(t // tm,),
        in_specs=[row(r.shape[1]), row(fo.shape[1]), row(gates.shape[1]), row(d),
                  full(wr), full(wf), full(wo), full(g1), full(b1), full(wr2), full(rbias)],
        out_specs=(row(d), pl.BlockSpec((N_PIECES, tm, PIECE_W), lambda i: (0, i, 0)), row(ROUTE_LANES),
                   pl.BlockSpec((tm // RANK_TILE, 1, ROUTE_LANES), lambda i: (i, 0, 0))),
        compiler_params=_cparams(("parallel",)),
        name="merge_router",
    )(r, fo, gates, x, wr, wf, wo, g1, b1, wr2, rbias)


def _sc_mesh():
    return plsc.VectorSubcoreMesh(core_axis_name="c", subcore_axis_name="s")


def _sc_gather_rows(table, idx):
    n, w = idx.shape[0], table.shape[1]
    idx = idx.reshape(1, n)

    @functools.partial(pl.kernel, out_type=jax.ShapeDtypeStruct((n, w), table.dtype), mesh=_sc_mesh(),
                       scratch_types=[])
    def gather(tab_hbm, idx_hbm, out_hbm):
        def body(idx_vmem, out_vmem):
            pltpu.sync_copy(tab_hbm.at[idx_vmem.at[0]], out_vmem)

        pltpu.emit_pipeline(
            body,
            grid=(n // SC_GATHER_WINDOW,),
            in_specs=[pl.BlockSpec((1, SC_GATHER_WINDOW), lambda i: (0, i))],
            out_specs=[pl.BlockSpec((SC_GATHER_WINDOW, w), lambda i: (i, 0))],
            core_axis_name=("c", "s"),
            dimension_semantics=(pltpu.PARALLEL,),
        )(idx_hbm, out_hbm)

    return gather(table, idx)


def _sc_scatter_rows(src, idx, n_out):
    assert idx.shape[0] == n_out
    n, w = n_out, src.shape[1]
    idx = idx.reshape(1, n)
    src_windows = src.shape[0] // SC_GATHER_WINDOW

    @functools.partial(pl.kernel, out_type=jax.ShapeDtypeStruct((n, w), src.dtype), mesh=_sc_mesh(),
                       scratch_types=[])
    def scatter(src_hbm, idx_hbm, out_hbm):
        def body(src_vmem, idx_vmem):
            pltpu.sync_copy(src_vmem, out_hbm.at[idx_vmem.at[0]])

        pltpu.emit_pipeline(
            body,
            grid=(n // SC_GATHER_WINDOW,),
            in_specs=[pl.BlockSpec((SC_GATHER_WINDOW, w), lambda i: (i % src_windows, 0)),
                      pl.BlockSpec((1, SC_GATHER_WINDOW), lambda i: (0, i))],
            out_specs=[],
            core_axis_name=("c", "s"),
            dimension_semantics=(pltpu.PARALLEL,),
        )(src_hbm, idx_hbm)

    return scatter(src, idx)


def _expert_kernel(te_ref, nv_ref, run_ref, xs_ref, wg_hbm, wu_hbm, wd_hbm, o_ref,
                   wg_f, wu_f, wd_f, wg_s, wu_s, wd_s, sems, *, slab0):
    i = pl.program_id(0)

    def weight_copies(expert, slot):
        return [pltpu.make_async_copy(hbm.at[slab0 + expert], buf.at[slot], sems.at[slot, k])
                for k, (hbm, buf) in enumerate(((wg_hbm, wg_f), (wu_hbm, wu_f), (wd_hbm, wd_f)))]

    @pl.when(i == 0)
    def _():
        for c in weight_copies(te_ref[0], 0):
            c.start()

    @pl.when(run_ref[0, i] == 1)
    def _():
        slot = run_ref[1, i]
        for c in weight_copies(te_ref[i], slot):
            c.wait()

        @pl.when(run_ref[2, i] >= 0)
        def _():
            for c in weight_copies(run_ref[2, i], 1 - slot):
                c.start()

        wg_s[...] = wg_f[slot].astype(BF16)
        wu_s[...] = wu_f[slot].astype(BF16)
        wd_s[...] = wd_f[slot].astype(BF16)

    @pl.when(i < nv_ref[0])
    def _():
        chunks = [c.astype(BF16) for c in _unpack_pieces([xs_ref[s] for s in range(N_PIECES)])]
        cw = chunks[0].shape[1]

        def up(w_s):
            acc = jnp.dot(chunks[0], w_s[:cw, :], preferred_element_type=F32)
            for j in range(1, len(chunks)):
                acc = acc + jnp.dot(chunks[j], w_s[j * cw:(j + 1) * cw, :], preferred_element_type=F32)
            return acc

        g = up(wg_s)
        hmid = (g * jax.nn.sigmoid(g) * up(wu_s)).astype(BF16)
        _store_pieces(o_ref, _pack_halves(jnp.dot(hmid, wd_s[...], preferred_element_type=F32)))

    @pl.when(i >= nv_ref[0])
    def _():
        o_ref[...] = jnp.zeros_like(o_ref)


def _expert_mlp(tile_expert, n_valid, xs, wg, wu, wd, layer, tm):
    p = xs.shape[1]
    n_tiles = p // tm
    d, de = wg.shape[1], wg.shape[2]
    tiles = jnp.arange(n_tiles, dtype=jnp.int32)
    prev = jnp.concatenate([tile_expert[:1], tile_expert[:-1]])
    is_start = (tiles == 0) | (tile_expert != prev)
    slot = (jnp.cumsum(is_start.astype(jnp.int32)) - 1) % 2
    start_at = jnp.where(is_start, tiles, n_tiles)
    next_start = jnp.concatenate([lax.cummin(start_at[::-1])[::-1][1:], jnp.full((1,), n_tiles, jnp.int32)])
    next_expert = jnp.where(next_start < n_tiles, tile_expert[jnp.minimum(next_start, n_tiles - 1)], -1)
    runs = jnp.stack([is_start.astype(jnp.int32), slot, next_expert]).astype(jnp.int32)

    xmap = lambda i, te, nv, run: (0, jnp.minimum(i, nv[0] - 1), 0)
    return pl.pallas_call(
        functools.partial(_expert_kernel, slab0=layer * N_EXPERTS),
        out_shape=jax.ShapeDtypeStruct(xs.shape, jnp.uint32),
        grid_spec=pltpu.PrefetchScalarGridSpec(
            num_scalar_prefetch=3,
            grid=(n_tiles,),
            in_specs=[pl.BlockSpec((N_PIECES, tm, PIECE_W), xmap),
                      pl.BlockSpec(memory_space=pl.ANY),
                      pl.BlockSpec(memory_space=pl.ANY),
                      pl.BlockSpec(memory_space=pl.ANY)],
            out_specs=pl.BlockSpec((N_PIECES, tm, PIECE_W), lambda i, te, nv, run: (0, i, 0)),
            scratch_shapes=[pltpu.VMEM((2, d, de), F32), pltpu.VMEM((2, d, de), F32), pltpu.VMEM((2, de, d), F32),
                            pltpu.VMEM((d, de), BF16), pltpu.VMEM((d, de), BF16), pltpu.VMEM((de, d), BF16),
                            pltpu.SemaphoreType.DMA((2, 3))]),
        compiler_params=_cparams(("arbitrary",)),
        name="expert_mlp",
    )(tile_expert, n_valid, runs, xs, wg, wu, wd)


def _combine_kernel(y_ref, route_ref, x_ref, g_ref, b_ref, o_ref, *, alpha):
    o_ref[...] = _combine_rows(y_ref, route_ref, x_ref, g_ref, b_ref, alpha)


def _combine(yg, route, x1, g2, b2, alpha, tm):
    n_rows, d = x1.shape
    row = lambda w: pl.BlockSpec((tm, w), lambda i: (i, 0))
    vec = pl.BlockSpec((1, d), lambda i: (0, 0))
    return pl.pallas_call(
        functools.partial(_combine_kernel, alpha=alpha),
        out_shape=jax.ShapeDtypeStruct((n_rows, d), F32),
        grid=(n_rows // tm,),
        in_specs=[pl.BlockSpec((N_PIECES, 2, tm, PIECE_W), lambda i: (0, 0, i, 0)),
                  row(ROUTE_LANES), row(d), vec, vec],
        out_specs=pl.BlockSpec((tm, d), lambda i: (i, 0)),
        compiler_params=_cparams(("parallel",)),
        name="combine_ln2",
    )(yg, route, x1, g2.reshape(1, d), b2.reshape(1, d))


def _rope_tables(seq):
    half = QK_DIM // 2
    inv_freq = 1.0 / (ROPE_BASE ** (jnp.arange(half, dtype=F32) / half))
    ang = jnp.arange(seq, dtype=F32)[:, None] * inv_freq[None, :]
    cos, sin = jnp.cos(ang), jnp.sin(ang)
    return jnp.concatenate([cos, cos], axis=1), jnp.concatenate([-sin, sin], axis=1)


def _channel_dft_matrix():
    n = np.arange(FG_DIM)
    ang = 2.0 * np.pi * np.outer(n, n) / FG_DIM
    eye = np.eye(N_FGROUPS)
    bd = np.concatenate([np.kron(eye, np.cos(ang)), np.kron(eye, np.sin(ang))], axis=1) / math.sqrt(FG_DIM)
    return jnp.asarray(bd, dtype=BF16)


def _dft_matrix_kernel(ch_ref, sh_ref, cl_ref, sl_ref, c_ref, s_ref):
    cl = cl_ref[...]
    sl = sl_ref[...]
    for blk in range(ch_ref.shape[1]):
        ch = ch_ref[:, blk:blk + 1]
        sh = sh_ref[:, blk:blk + 1]
        cols = slice(blk * LANE, (blk + 1) * LANE)
        c_ref[:, cols] = (ch * cl - sh * sl).astype(BF16)
        s_ref[:, cols] = (sh * cl + ch * sl).astype(BF16)


def _seq_dft_matrices(seq):
    half = seq // 2
    assert half % LANE == 0
    k = jnp.arange(seq, dtype=jnp.int32)[:, None]
    hi = jnp.arange(half // LANE, dtype=jnp.int32)[None, :] * LANE
    lo = jnp.arange(LANE, dtype=jnp.int32)[None, :]
    w = 2.0 * math.pi / seq
    a_hi = ((k * hi) % seq).astype(F32) * w
    a_lo = ((k * lo) % seq).astype(F32) * w
    scale = seq ** -0.5
    tm = min(256, seq)
    nb = half // LANE
    small = lambda width: pl.BlockSpec((tm, width), lambda i: (i, 0))
    return pl.pallas_call(
        _dft_matrix_kernel,
        out_shape=(jax.ShapeDtypeStruct((seq, half), BF16), jax.ShapeDtypeStruct((seq, half), BF16)),
        grid=(seq // tm,),
        in_specs=[small(nb), small(nb), small(LANE), small(LANE)],
        out_specs=(small(half), small(half)),
        compiler_params=_cparams(("parallel",)),
        name="dft_matrices",
    )(jnp.cos(a_hi) * scale, jnp.sin(a_hi) * scale, jnp.cos(a_lo), jnp.sin(a_lo))


def _pad_heads(w, axis):
    shp = list(w.shape)
    shp[axis:axis + 1] = [N_HEADS, V_DIM]
    w = w.reshape(shp)
    pad = [(0, 0)] * w.ndim
    pad[axis + 1] = (0, V_PAD - V_DIM)
    w = jnp.pad(w, pad)
    shp[axis:axis + 2] = [N_HEADS * V_PAD]
    return w.reshape(shp)


def _pad_in_weight(w_in):
    qk_w = 2 * N_HEADS * QK_DIM
    vw = N_HEADS * V_DIM
    qk = w_in[:, :qk_w]
    v = _pad_heads(w_in[:, qk_w:qk_w + vw], 1)
    g = _pad_heads(w_in[:, qk_w + vw:qk_w + 2 * vw], 1)
    rest = w_in[:, qk_w + 2 * vw:]
    return jnp.concatenate([qk, v, g, rest], axis=1).astype(BF16)


def _routing_plan(route, tile_counts, tm_tok, n_tiles, moe_tile):
    t = route.shape[0]
    experts = jnp.arange(N_EXPERTS, dtype=jnp.int32)
    cnt = tile_counts[:, 0, :N_EXPERTS].astype(jnp.int32)
    before = jnp.cumsum(cnt, axis=0) - cnt
    counts = jnp.sum(cnt, axis=0)
    tiles_per = (counts + moe_tile - 1) // moe_tile
    tile_end = jnp.cumsum(tiles_per)
    start_row = (tile_end - tiles_per) * moe_tile
    n_valid = tile_end[-1]

    base = (start_row[None, :] + before).astype(F32)
    base_tok = jnp.broadcast_to(base[:, None, :], (t // tm_tok, tm_tok, N_EXPERTS)).reshape(t, N_EXPERTS)
    e = route[:, 0:2]
    rank = route[:, 4:6]
    sel = e[:, :, None] == experts.astype(F32)[None, None, :]
    pos = (jnp.sum(jnp.where(sel, base_tok[:, None, :], 0.0), axis=-1) + rank).astype(jnp.int32)

    n_pad = n_tiles * moe_tile - 2 * t
    pads = tiles_per * moe_tile - counts
    pad_end = jnp.cumsum(pads)
    p = jnp.arange(n_pad, dtype=jnp.int32)
    pe = jnp.minimum(jnp.sum((p[:, None] >= pad_end[None, :]).astype(jnp.int32), axis=1), N_EXPERTS - 1)
    in_expert = start_row[pe] + counts[pe] + (p - (pad_end[pe] - pads[pe]))
    pad_pos = jnp.where(p < pad_end[-1], in_expert, n_valid * moe_tile + (p - pad_end[-1]))

    tile_ids = jnp.arange(n_tiles, dtype=jnp.int32)
    tile_expert = jnp.minimum(jnp.sum((tile_ids[:, None] >= tile_end[None, :]).astype(jnp.int32), axis=1),
                              N_EXPERTS - 1)
    last_e = tile_expert[jnp.maximum(n_valid - 1, 0)]
    tile_expert = jnp.where(tile_ids < n_valid, tile_expert, last_e).astype(jnp.int32)
    return pos, pad_pos.astype(jnp.int32), tile_expert, n_valid.reshape(1).astype(jnp.int32)


def _largest_tile(cands, *dims):
    for c in cands:
        if all(d % c == 0 for d in dims):
            return c
    raise ValueError(f"no tile in {cands} divides {dims}")


def kernel(x_prompt, x_sample, ln_in_g, ln_in_b, w_in, ret_decay_fwd, ret_decay_bwd, ret_gn_g, ret_gn_b, w_ret_up, w_four_up, w_out, ln1_g, ln1_b, w_route_group, b_route_group, w_route_expert, b_route_expert, w_expert_gate, w_expert_up, w_expert_down, ln2_g, ln2_b):
    bp, sp, d = x_prompt.shape
    bs, ss, _ = x_sample.shape
    depth = w_in.shape[0]
    alpha = float((2 * depth) ** 0.25)

    cos_t, sin_t = _rope_tables(max(sp, ss))
    bd = _channel_dft_matrix()
    dft = {s: _seq_dft_matrices(s) for s in sorted({sp, ss})}
    wg_all = w_expert_gate.reshape((depth * N_EXPERTS,) + w_expert_gate.shape[2:])
    wu_all = w_expert_up.reshape((depth * N_EXPERTS,) + w_expert_up.shape[2:])
    wd_all = w_expert_down.reshape((depth * N_EXPERTS,) + w_expert_down.shape[2:])

    batches = []
    for xin in (x_prompt, x_sample):
        nb, seq, _ = xin.shape
        t = nb * seq
        assert seq % CHUNK == 0 and t % SC_GATHER_WINDOW == 0
        moe_tile = MOE_TILE if 2 * t >= 4 * MOE_TILE * N_EXPERTS else MOE_TILE // 2
        assert (2 * t) % moe_tile == 0
        batches.append(dict(nb=nb, seq=seq, t=t, moe_tile=moe_tile, n_tiles=(2 * t) // moe_tile + N_EXPERTS,
                            tm_in=_largest_tile((512, 256, 128), seq),
                            tm_tok=_largest_tile((TOKEN_TILE, 128), t),
                            tm_merge=_largest_tile((MERGE_TILE, RANK_TILE), t),
                            prologue=(xin.reshape(t, d),)))

    for l in range(depth):
        w_pad = _pad_in_weight(w_in[l])
        dec = jnp.stack([ret_decay_fwd[l], ret_decay_bwd[l]]).astype(F32)
        gn_g = _pad_heads(ret_gn_g[l], 0).reshape(1, -1).astype(F32)
        gn_b = _pad_heads(ret_gn_b[l], 0).reshape(1, -1).astype(F32)
        wr = _pad_heads(w_ret_up[l], 0).astype(BF16)
        wf = w_four_up[l].astype(BF16)
        wo = w_out[l].astype(BF16)
        w_rt = jnp.concatenate([w_route_group[l], w_route_expert[l]], axis=1)
        w_rt = jnp.pad(w_rt, ((0, 0), (0, ROUTE_LANES // 2 - w_rt.shape[1])))
        wrh = w_rt.astype(BF16)
        wr2 = jnp.concatenate([wrh, (w_rt - wrh.astype(F32)).astype(BF16)], axis=1)
        rbias = jnp.concatenate([b_route_group[l], b_route_expert[l]])
        rbias = jnp.pad(rbias, (0, ROUTE_LANES - rbias.shape[0])).reshape(1, ROUTE_LANES).astype(F32)
        pro_g, pro_b = (ln_in_g, ln_in_b) if l == 0 else (ln2_g[l - 1], ln2_b[l - 1])

        for bt in batches:
            nb, seq, t, moe_tile, n_tiles = bt["nb"], bt["seq"], bt["t"], bt["moe_tile"], bt["n_tiles"]
            x, qk, v, sg, z, gates = _inproj(bt["prologue"], pro_g, pro_b, w_pad, cos_t, sin_t, bd,
                                             bt["tm_in"], seq, alpha)
            r = _retention(dec, qk, v, sg, gn_g, gn_b, nb, seq, 0)
            fo = _seq_dft(dft[seq][0], dft[seq][1], z, nb, seq)
            x1, x1p, route, tile_counts = _merge(r, fo, gates, x, wr, wf, wo, ln1_g[l], ln1_b[l], wr2, rbias,
                                                 alpha, bt["tm_merge"])

            pos, pad_pos, tile_expert, n_valid = _routing_plan(route, tile_counts, RANK_TILE, n_tiles, moe_tile)
            n_sorted = n_tiles * moe_tile
            piece_off = jnp.arange(N_PIECES, dtype=jnp.int32)[:, None] * n_sorted
            dst = jnp.concatenate([(piece_off + seg[None, :]).reshape(-1)
                                   for seg in (pos[:, 0], pos[:, 1], pad_pos)])
            xs = _sc_scatter_rows(x1p.reshape(N_PIECES * t, PIECE_W), dst, N_PIECES * n_sorted)
            ys = _expert_mlp(tile_expert, n_valid, xs.reshape(N_PIECES, n_sorted, PIECE_W),
                             wg_all, wu_all, wd_all, l, moe_tile)
            src = (piece_off[:, :, None] + pos.T[None, :, :]).reshape(-1)
            yg = _sc_gather_rows(ys.reshape(N_PIECES * n_sorted, PIECE_W), src).reshape(N_PIECES, 2, t, PIECE_W)
            bt["prologue"] = (yg, route, x1)

    outs = [_combine(*bt["prologue"], ln2_g[depth - 1], ln2_b[depth - 1], alpha, bt["tm_tok"]) for bt in batches]
    return outs[0].reshape(bp, sp, d), outs[1].reshape(bs, ss, d)
```

```python
import functools
import math

import jax
import jax.numpy as jnp
import numpy as np
from jax import lax
from jax.experimental import pallas as pl
from jax.experimental.pallas import tpu as pltpu
from jax.experimental.pallas import tpu_sc as plsc

N_HEADS = 4
QK_DIM = 128
V_DIM = 192
V_PAD = 256
CHUNK = 128
ROPE_BASE = 10000.0
N_FGROUPS = 4
FG_DIM = 64
F_WIDTH = N_FGROUPS * FG_DIM
N_GROUPS = 4
EXPERTS_PER_GROUP = 8
N_EXPERTS = N_GROUPS * EXPERTS_PER_GROUP
LN_EPS = 1e-5
ROUTE_LANES = 128

LANE = 128
VMEM_LIMIT_BYTES = 56 * 1024 * 1024
MOE_TILE = 512
GATHER_TILE = 256
SC_GATHER_WINDOW = 128
N_PIECES = 2
PIECE_W = 256
TOKEN_TILE = 256
MERGE_TILE = 512
RANK_TILE = 256
RET_UNROLL = 8

F32 = jnp.float32
BF16 = jnp.bfloat16


def _cparams(sem):
    return pltpu.CompilerParams(dimension_semantics=sem, vmem_limit_bytes=VMEM_LIMIT_BYTES)


def _layer_norm_rows(y, g, b):
    mu = jnp.mean(y, axis=-1, keepdims=True)
    d = y - mu
    var = jnp.mean(d * d, axis=-1, keepdims=True)
    return d * lax.rsqrt(var + LN_EPS) * g + b


def _pack_halves(y):
    n = y.shape[1] // 2
    lo = pltpu.bitcast(y[:, :n].astype(BF16).astype(F32), jnp.uint32)
    hi = pltpu.bitcast(y[:, n:].astype(BF16).astype(F32), jnp.uint32)
    return (hi & jnp.uint32(0xFFFF0000)) | (lo >> 16)


def _unpack_halves(w):
    lo = pltpu.bitcast(w << 16, F32)
    hi = pltpu.bitcast(w & jnp.uint32(0xFFFF0000), F32)
    return lo, hi


def _store_pieces(ref, packed):
    for s in range(N_PIECES):
        ref[s] = packed[:, s * PIECE_W:(s + 1) * PIECE_W]


def _unpack_pieces(pieces):
    halves = [_unpack_halves(p) for p in pieces]
    return [h[0] for h in halves] + [h[1] for h in halves]


def _combine_rows(y_ref, route_ref, x_ref, g_ref, b_ref, alpha):
    c1 = route_ref[:, 2:3]
    c2 = route_ref[:, 3:4]
    first = _unpack_pieces([y_ref[s, 0] for s in range(N_PIECES)])
    second = _unpack_pieces([y_ref[s, 1] for s in range(N_PIECES)])
    ff = jnp.concatenate([c1 * a + c2 * b for a, b in zip(first, second)], axis=1)
    return _layer_norm_rows(alpha * x_ref[...] + ff, g_ref[...], b_ref[...])


def _inproj_first_kernel(xraw_ref, g_ref, b_ref, *rest, d_model):
    _project(_layer_norm_rows(xraw_ref[...], g_ref[...], b_ref[...]), *rest, d_model=d_model)


def _inproj_next_kernel(y_ref, route_ref, x1_ref, g_ref, b_ref, *rest, alpha, d_model):
    _project(_combine_rows(y_ref, route_ref, x1_ref, g_ref, b_ref, alpha), *rest, d_model=d_model)


def _project(x_f32, w_ref, cos_ref, sin_ref, bd_ref,
             x_out_ref, qk_ref, v_ref, sg_ref, z_ref, gates_ref, *, d_model):
    x_out_ref[...] = x_f32
    x = x_f32.astype(BF16)
    cos = cos_ref[...]
    sin = sin_ref[...]
    qk_w = 2 * N_HEADS * QK_DIM
    v_w = N_HEADS * V_PAD
    off_v = qk_w
    off_g = off_v + v_w
    off_f = off_g + v_w
    off_gr = off_f + F_WIDTH
    cw = 256

    def proj(c0):
        return jnp.dot(x, w_ref[:, c0:c0 + cw], preferred_element_type=F32)

    for c0 in range(0, qk_w, cw):
        acc = proj(c0)
        scale = QK_DIM ** -0.5 if c0 < N_HEADS * QK_DIM else 1.0
        parts = []
        for hh in range(cw // QK_DIM):
            a = acc[:, hh * QK_DIM:(hh + 1) * QK_DIM]
            rot = a * cos + pltpu.roll(a, QK_DIM // 2, axis=1) * sin
            if scale != 1.0:
                rot = rot * scale
            parts.append(rot.astype(BF16))
        qk_ref[:, c0:c0 + cw] = jnp.concatenate(parts, axis=1)
    for c0 in range(0, v_w, cw):
        v_ref[:, c0:c0 + cw] = proj(off_v + c0).astype(BF16)
    for c0 in range(0, v_w, cw):
        a = proj(off_g + c0)
        sg_ref[:, c0:c0 + cw] = (a * jax.nn.sigmoid(a)).astype(BF16)
    f = proj(off_f).astype(BF16)
    z_ref[...] = _pack_halves(jnp.dot(f, bd_ref[...], preferred_element_type=F32))
    for c0 in range(0, 2 * d_model, cw):
        gates_ref[:, c0:c0 + cw] = jax.nn.sigmoid(proj(off_gr + c0)).astype(BF16)


def _inproj(prologue, g, b, w_pad, cos_t, sin_t, bd, tm, seq, alpha):
    first = len(prologue) == 1
    t, d = prologue[-1].shape
    n_cols = w_pad.shape[1]
    pos_map = lambda i: (i % (seq // tm), 0)
    row = lambda w: pl.BlockSpec((tm, w), lambda i: (i, 0))
    vec = pl.BlockSpec((1, d), lambda i: (0, 0))
    outs = (jax.ShapeDtypeStruct((t, d), F32),
            jax.ShapeDtypeStruct((t, 2 * N_HEADS * QK_DIM), BF16),
            jax.ShapeDtypeStruct((t, N_HEADS * V_PAD), BF16),
            jax.ShapeDtypeStruct((t, N_HEADS * V_PAD), BF16),
            jax.ShapeDtypeStruct((t, F_WIDTH), jnp.uint32),
            jax.ShapeDtypeStruct((t, 2 * d), BF16))
    if first:
        body = functools.partial(_inproj_first_kernel, d_model=d)
        pro_specs = [row(d)]
    else:
        body = functools.partial(_inproj_next_kernel, alpha=alpha, d_model=d)
        pro_specs = [pl.BlockSpec((N_PIECES, 2, tm, PIECE_W), lambda i: (0, 0, i, 0)), row(ROUTE_LANES), row(d)]
    return pl.pallas_call(
        body,
        out_shape=outs,
        grid=(t // tm,),
        in_specs=pro_specs + [vec, vec,
                              pl.BlockSpec((d, n_cols), lambda i: (0, 0)),
                              pl.BlockSpec((tm, QK_DIM), pos_map),
                              pl.BlockSpec((tm, QK_DIM), pos_map),
                              pl.BlockSpec((F_WIDTH, 2 * F_WIDTH), lambda i: (0, 0))],
        out_specs=tuple(row(o.shape[1]) for o in outs),
        compiler_params=_cparams(("parallel",)),
        name="inproj",
    )(*prologue, g.reshape(1, d), b.reshape(1, d), w_pad, cos_t, sin_t, bd)


def _log_sigmoid(x):
    return jnp.minimum(x, 0.0) - jnp.log1p(jnp.exp(-jnp.abs(x)))


def _retention_kernel(dec_ref, q_ref, k_ref, v_ref, sg_ref, gng_ref, gnb_ref, o_ref,
                      r_all, rf_s, rb_s, *, n_chunks):
    h = pl.program_id(1)
    c = CHUNK
    lf = _log_sigmoid(jnp.full((1, c), dec_ref[0, h], F32))
    lb = _log_sigmoid(jnp.full((1, c), dec_ref[1, h], F32))
    lf_w = _log_sigmoid(jnp.full((1, V_PAD), dec_ref[0, h], F32))
    lb_w = _log_sigmoid(jnp.full((1, V_PAD), dec_ref[1, h], F32))
    row = lax.broadcasted_iota(jnp.int32, (c, c), 0).astype(F32)
    col = lax.broadcasted_iota(jnp.int32, (c, c), 1).astype(F32)
    diff = row - col
    dmat = jnp.where(diff >= 0.0, jnp.exp(lf * jnp.maximum(diff, 0.0)),
                     jnp.exp(lb * jnp.maximum(-diff, 0.0)))
    qd_f = jnp.exp(lf * (row + 1.0))
    kd_f = jnp.exp(lf * (c - 1.0 - row))
    qd_b = jnp.exp(lb * (c - row))
    kd_b = jnp.exp(lb * row)
    cf = jnp.exp(lf_w * float(c))
    cb = jnp.exp(lb_w * float(c))
    gn_g = gng_ref[...]
    gn_b = gnb_ref[...]
    tdot = (((0,), (0,)), ((), ()))

    rf_s[...] = jnp.zeros_like(rf_s)
    rb_s[...] = jnp.zeros_like(rb_s)

    def state_step(t, carry):
        for (ci, kd, cdec, r_s, half) in ((t, kd_f, cf, rf_s, 0), (n_chunks - 1 - t, kd_b, cb, rb_s, 1)):
            r0 = pl.multiple_of(ci * c, c)
            kk = (k_ref[pl.ds(r0, c), :].astype(F32) * kd).astype(BF16)
            kv = lax.dot_general(kk, v_ref[pl.ds(r0, c), :], tdot, preferred_element_type=F32)
            r = r_s[...]
            r_all[ci, half * c:(half + 1) * c, :] = r.astype(BF16)
            r_s[...] = cdec * r + kv
        return carry

    lax.fori_loop(0, n_chunks, state_step, 0, unroll=RET_UNROLL)

    def out_step(ci, carry):
        r0 = pl.multiple_of(ci * c, c)
        q = q_ref[pl.ds(r0, c), :]
        qf = q.astype(F32)
        s = lax.dot_general(q, k_ref[pl.ds(r0, c), :], (((1,), (1,)), ((), ())), preferred_element_type=F32)
        p = (s * dmat).astype(BF16)
        o = jnp.dot(p, v_ref[pl.ds(r0, c), :], preferred_element_type=F32)
        qq = jnp.concatenate([(qf * qd_f).astype(BF16), (qf * qd_b).astype(BF16)], axis=1)
        o = o + jnp.dot(qq, r_all[ci], preferred_element_type=F32)
        mu = jnp.sum(o, axis=-1, keepdims=True) * (1.0 / V_DIM)
        d = o - mu
        var = (jnp.sum(d * d, axis=-1, keepdims=True) - float(V_PAD - V_DIM) * mu * mu) * (1.0 / V_DIM)
        y = d * lax.rsqrt(var + LN_EPS) * gn_g + gn_b
        o_ref[pl.ds(r0, c), :] = (y * sg_ref[pl.ds(r0, c), :].astype(F32)).astype(BF16)
        return carry

    lax.fori_loop(0, n_chunks, out_step, 0, unroll=RET_UNROLL)


def _retention(dec, qk, v, sg, gn_g, gn_b, n_batch, seq, row_block0):
    n_chunks = seq // CHUNK
    rb = lambda b: b + row_block0
    return pl.pallas_call(
        functools.partial(_retention_kernel, n_chunks=n_chunks),
        out_shape=jax.ShapeDtypeStruct((n_batch * seq, N_HEADS * V_PAD), BF16),
        grid_spec=pltpu.PrefetchScalarGridSpec(
            num_scalar_prefetch=0,
            grid=(n_batch, N_HEADS),
            in_specs=[pl.BlockSpec(memory_space=pltpu.SMEM),
                      pl.BlockSpec((seq, QK_DIM), lambda b, h: (rb(b), h)),
                      pl.BlockSpec((seq, QK_DIM), lambda b, h: (rb(b), N_HEADS + h)),
                      pl.BlockSpec((seq, V_PAD), lambda b, h: (rb(b), h)),
                      pl.BlockSpec((seq, V_PAD), lambda b, h: (rb(b), h)),
                      pl.BlockSpec((1, V_PAD), lambda b, h: (0, h)),
                      pl.BlockSpec((1, V_PAD), lambda b, h: (0, h))],
            out_specs=pl.BlockSpec((seq, V_PAD), lambda b, h: (b, h)),
            scratch_shapes=[pltpu.VMEM((n_chunks, 2 * CHUNK, V_PAD), BF16),
                            pltpu.VMEM((CHUNK, V_PAD), F32),
                            pltpu.VMEM((CHUNK, V_PAD), F32)]),
        compiler_params=_cparams(("parallel", "arbitrary")),
        name="retention",
    )(dec, qk, qk, v, sg, gn_g, gn_b)


def _seqdft_kernel(c_ref, s_ref, z_ref, zrev_ref, zmid_ref, o_ref, acc_ref, *, scale):
    kk = pl.program_id(1)
    b = pl.program_id(2)
    zc, zs = _unpack_halves(z_ref[...])
    rc, rs = _unpack_halves(zrev_ref[...])
    no_partner = (lax.broadcasted_iota(jnp.int32, zc.shape, 0) == 0) & (kk == 0)
    fold_c = (zc + jnp.where(no_partner, 0.0, rc)).astype(BF16)
    fold_s = (zs - jnp.where(no_partner, 0.0, rs)).astype(BF16)
    part = (jnp.dot(c_ref[...], fold_c, preferred_element_type=F32)
            - jnp.dot(s_ref[...], fold_s, preferred_element_type=F32))

    @pl.when(kk == 0)
    def _():
        acc_ref[b] = part

    @pl.when(kk > 0)
    def _():
        acc_ref[b] += part

    @pl.when(kk == pl.num_programs(1) - 1)
    def _():
        tm = acc_ref.shape[1]
        odd = lax.broadcasted_iota(jnp.int32, (tm, F_WIDTH), 0) % 2 == 1
        mid = zmid_ref[b, 0:1, :] * scale
        o_ref[b] = (acc_ref[b] + jnp.where(odd, -mid, mid)).astype(o_ref.dtype)


def _seq_partner_rows(z, n_batch, seq):
    half = seq // 2
    assert (n_batch * half) % SC_GATHER_WINDOW == 0
    n = jnp.arange(half, dtype=jnp.int32)[None, :]
    partner = (jnp.arange(n_batch, dtype=jnp.int32)[:, None] * seq + (seq - n) % seq).reshape(-1)
    return _sc_gather_rows(z, partner).reshape(n_batch, half, F_WIDTH)


def _seq_dft(cmat, smat, z, zrev, n_batch, seq):
    half = seq // 2
    tm = min(1024, seq)
    tk = min(1024, half)
    assert tm % 2 == 0
    z3 = z.reshape(n_batch, seq, F_WIDTH)
    zmid = lax.bitcast_convert_type(z3[:, half, :] << 16, F32)
    zmid = jnp.broadcast_to(zmid[:, None, :], (n_batch, 8, F_WIDTH))
    zblock = pl.BlockSpec((None, tk, F_WIDTH), lambda i, k, b: (b, k, 0))
    out = pl.pallas_call(
        functools.partial(_seqdft_kernel, scale=seq ** -0.5),
        out_shape=jax.ShapeDtypeStruct((n_batch, seq, F_WIDTH), BF16),
        grid=(seq // tm, half // tk, n_batch),
        in_specs=[pl.BlockSpec((tm, tk), lambda i, k, b: (i, k)),
                  pl.BlockSpec((tm, tk), lambda i, k, b: (i, k)),
                  zblock, zblock,
                  pl.BlockSpec((n_batch, 8, F_WIDTH), lambda i, k, b: (0, 0, 0))],
        out_specs=pl.BlockSpec((n_batch, tm, F_WIDTH), lambda i, k, b: (0, i, 0)),
        scratch_shapes=[pltpu.VMEM((n_batch, tm, F_WIDTH), F32)],
        compiler_params=_cparams(("parallel", "arbitrary", "arbitrary")),
        name="seq_dft",
    )(cmat, smat, z3, zrev, zmid)
    return out.reshape(n_batch * seq, F_WIDTH)


def _merge_kernel(r_ref, fo_ref, gates_ref, x_ref, wr_ref, wf_ref, wo_ref, g1_ref, b1_ref,
                  wr2_ref, rb_ref, x1_ref, x1p_ref, route_ref, cnt_ref, *, alpha, d_model):
    for j in range(x_ref.shape[0] // RANK_TILE):
        rows = pl.ds(j * RANK_TILE, RANK_TILE)
        _merge_subtile(rows, j, r_ref, fo_ref, gates_ref, x_ref, wr_ref, wf_ref, wo_ref,
                       g1_ref, b1_ref, wr2_ref, rb_ref, x1_ref, x1p_ref, route_ref, cnt_ref,
                       alpha=alpha, d_model=d_model)


def _merge_subtile(rows, j, r_ref, fo_ref, gates_ref, x_ref, wr_ref, wf_ref, wo_ref,
                   g1_ref, b1_ref, wr2_ref, rb_ref, x1_ref, x1p_ref, route_ref, cnt_ref, *, alpha, d_model):
    a = jnp.dot(r_ref[rows, :], wr_ref[...], preferred_element_type=F32)
    b = jnp.dot(fo_ref[rows, :], wf_ref[...], preferred_element_type=F32)
    merged = (gates_ref[rows, :d_model].astype(F32) * a + gates_ref[rows, d_model:].astype(F32) * b)
    mix = jnp.dot(merged.astype(BF16), wo_ref[...], preferred_element_type=F32)
    x1 = _layer_norm_rows(alpha * x_ref[rows, :] + mix, g1_ref[...], b1_ref[...])
    x1_ref[rows, :] = x1
    packed = _pack_halves(x1)
    for s in range(N_PIECES):
        x1p_ref[s, rows, :] = packed[:, s * PIECE_W:(s + 1) * PIECE_W]

    xh = x1.astype(BF16)
    xl = (x1 - xh.astype(F32)).astype(BF16)
    s2 = (jnp.dot(xh, wr2_ref[...], preferred_element_type=F32)
          + jnp.dot(xl, wr2_ref[...], preferred_element_type=F32))
    logits = s2 + pltpu.roll(s2, ROUTE_LANES // 2, axis=1) + rb_ref[...]
    tm = logits.shape[0]
    lane = lax.broadcasted_iota(jnp.int32, (tm, ROUTE_LANES), 1)
    neg = jnp.float32(-jnp.inf)
    big = jnp.int32(ROUTE_LANES)

    def top1(mask):
        vals = jnp.where(mask, logits, neg)
        m = jnp.max(vals, axis=-1, keepdims=True)
        idx = jnp.min(jnp.where(mask & (vals == m), lane, big), axis=-1, keepdims=True)
        return m, idx

    gmask = lane < N_GROUPS
    gm, gi = top1(gmask)
    gsum = jnp.sum(jnp.where(gmask, jnp.exp(logits - gm), 0.0), axis=-1, keepdims=True)
    g_gate = 1.0 / gsum
    lo = N_GROUPS + gi * EXPERTS_PER_GROUP
    emask = (lane >= lo) & (lane < lo + EXPERTS_PER_GROUP)
    m1, i1 = top1(emask)
    m2, i2 = top1(emask & (lane != i1))
    e2 = jnp.exp(m2 - m1)
    w1 = 1.0 / (1.0 + e2)
    w2 = e2 / (1.0 + e2)
    oh1 = (lane == i1 - N_GROUPS).astype(F32)
    oh2 = (lane == i2 - N_GROUPS).astype(F32)
    earlier = (lax.broadcasted_iota(jnp.int32, (tm, tm), 1)
               < lax.broadcasted_iota(jnp.int32, (tm, tm), 0)).astype(BF16)
    tot1 = jnp.sum(oh1, axis=0, keepdims=True)
    tot2 = jnp.sum(oh2, axis=0, keepdims=True)
    pre1 = jnp.dot(earlier, oh1.astype(BF16), preferred_element_type=F32)
    pre2 = jnp.dot(earlier, oh2.astype(BF16), preferred_element_type=F32) + tot1
    r1 = jnp.sum(pre1 * oh1, axis=-1, keepdims=True)
    r2 = jnp.sum(pre2 * oh2, axis=-1, keepdims=True)
    cnt_ref[j] = tot1 + tot2

    rec = jnp.where(lane == 0, (i1 - N_GROUPS).astype(F32), 0.0)
    rec = jnp.where(lane == 1, (i2 - N_GROUPS).astype(F32), rec)
    rec = jnp.where(lane == 2, g_gate * w1, rec)
    rec = jnp.where(lane == 3, g_gate * w2, rec)
    rec = jnp.where(lane == 4, r1, rec)
    rec = jnp.where(lane == 5, r2, rec)
    route_ref[rows, :] = rec


def _merge(r, fo, gates, x, wr, wf, wo, g1, b1, wr2, rbias, alpha, tm):
    t, d = x.shape
    row = lambda w: pl.BlockSpec((tm, w), lambda i: (i, 0))
    full = lambda a: pl.BlockSpec(a.shape, lambda i: (0, 0))
    g1 = g1.reshape(1, d)
    b1 = b1.reshape(1, d)
    return pl.pallas_call(
        functools.partial(_merge_kernel, alpha=alpha, d_model=d),
        out_shape=(jax.ShapeDtypeStruct((t, d), F32),
                   jax.ShapeDtypeStruct((N_PIECES, t, PIECE_W), jnp.uint32),
                   jax.ShapeDtypeStruct((t, ROUTE_LANES), F32),
                   jax.ShapeDtypeStruct((t // RANK_TILE, 1, ROUTE_LANES), F32)),
        grid=(t // tm,),
        in_specs=[row(r.shape[1]), row(fo.shape[1]), row(gates.shape[1]), row(d),
                  full(wr), full(wf), full(wo), full(g1), full(b1), full(wr2), full(rbias)],
        out_specs=(row(d), pl.BlockSpec((N_PIECES, tm, PIECE_W), lambda i: (0, i, 0)), row(ROUTE_LANES),
                   pl.BlockSpec((tm // RANK_TILE, 1, ROUTE_LANES), lambda i: (i, 0, 0))),
        compiler_params=_cparams(("parallel",)),
        name="merge_router",
    )(r, fo, gates, x, wr, wf, wo, g1, b1, wr2, rbias)


def _sc_mesh():
    return plsc.VectorSubcoreMesh(core_axis_name="c", subcore_axis_name="s")


def _sc_gather_rows(table, idx):
    n, w = idx.shape[0], table.shape[1]
    idx = idx.reshape(1, n)

    @functools.partial(pl.kernel, out_type=jax.ShapeDtypeStruct((n, w), table.dtype), mesh=_sc_mesh(),
                       scratch_types=[])
    def gather(tab_hbm, idx_hbm, out_hbm):
        def body(idx_vmem, out_vmem):
            pltpu.sync_copy(tab_hbm.at[idx_vmem.at[0]], out_vmem)

        pltpu.emit_pipeline(
            body,
            grid=(n // SC_GATHER_WINDOW,),
            in_specs=[pl.BlockSpec((1, SC_GATHER_WINDOW), lambda i: (0, i))],
            out_specs=[pl.BlockSpec((SC_GATHER_WINDOW, w), lambda i: (i, 0))],
            core_axis_name=("c", "s"),
            dimension_semantics=(pltpu.PARALLEL,),
        )(idx_hbm, out_hbm)

    return gather(table, idx)


def _sc_scatter_rows(src, idx, n_out):
    assert idx.shape[0] == n_out
    n, w = n_out, src.shape[1]
    idx = idx.reshape(1, n)
    src_windows = src.shape[0] // SC_GATHER_WINDOW

    @functools.partial(pl.kernel, out_type=jax.ShapeDtypeStruct((n, w), src.dtype), mesh=_sc_mesh(),
                       scratch_types=[])
    def scatter(src_hbm, idx_hbm, out_hbm):
        def body(src_vmem, idx_vmem):
            pltpu.sync_copy(src_vmem, out_hbm.at[idx_vmem.at[0]])

        pltpu.emit_pipeline(
            body,
            grid=(n // SC_GATHER_WINDOW,),
            in_specs=[pl.BlockSpec((SC_GATHER_WINDOW, w), lambda i: (i % src_windows, 0)),
                      pl.BlockSpec((1, SC_GATHER_WINDOW), lambda i: (0, i))],
            out_specs=[],
            core_axis_name=("c", "s"),
            dimension_semantics=(pltpu.PARALLEL,),
        )(src_hbm, idx_hbm)

    return scatter(src, idx)


def _expert_kernel(te_ref, nv_ref, run_ref, xs_ref, wg_hbm, wu_hbm, wd_hbm, o_ref,
                   wg_f, wu_f, wd_f, wg_s, wu_s, wd_s, sems, *, slab0):
    i = pl.program_id(0)

    def weight_copies(expert, slot):
        return [pltpu.make_async_copy(hbm.at[slab0 + expert], buf.at[slot], sems.at[slot, k])
                for k, (hbm, buf) in enumerate(((wg_hbm, wg_f), (wu_hbm, wu_f), (wd_hbm, wd_f)))]

    @pl.when(i == 0)
    def _():
        for c in weight_copies(te_ref[0], 0):
            c.start()

    @pl.when(run_ref[0, i] == 1)
    def _():
        slot = run_ref[1, i]
        for c in weight_copies(te_ref[i], slot):
            c.wait()

        @pl.when(run_ref[2, i] >= 0)
        def _():
            for c in weight_copies(run_ref[2, i], 1 - slot):
                c.start()

        wg_s[...] = wg_f[slot].astype(BF16)
        wu_s[...] = wu_f[slot].astype(BF16)
        wd_s[...] = wd_f[slot].astype(BF16)

    @pl.when(i < nv_ref[0])
    def _():
        chunks = [c.astype(BF16) for c in _unpack_pieces([xs_ref[s] for s in range(N_PIECES)])]
        cw = chunks[0].shape[1]

        def up(w_s):
            acc = jnp.dot(chunks[0], w_s[:cw, :], preferred_element_type=F32)
            for j in range(1, len(chunks)):
                acc = acc + jnp.dot(chunks[j], w_s[j * cw:(j + 1) * cw, :], preferred_element_type=F32)
            return acc

        g = up(wg_s)
        hmid = (g * jax.nn.sigmoid(g) * up(wu_s)).astype(BF16)
        _store_pieces(o_ref, _pack_halves(jnp.dot(hmid, wd_s[...], preferred_element_type=F32)))

    @pl.when(i >= nv_ref[0])
    def _():
        o_ref[...] = jnp.zeros_like(o_ref)


def _expert_mlp(tile_expert, n_valid, xs, wg, wu, wd, layer, tm):
    p = xs.shape[1]
    n_tiles = p // tm
    d, de = wg.shape[1], wg.shape[2]
    tiles = jnp.arange(n_tiles, dtype=jnp.int32)
    prev = jnp.concatenate([tile_expert[:1], tile_expert[:-1]])
    is_start = (tiles == 0) | (tile_expert != prev)
    slot = (jnp.cumsum(is_start.astype(jnp.int32)) - 1) % 2
    start_at = jnp.where(is_start, tiles, n_tiles)
    next_start = jnp.concatenate([lax.cummin(start_at[::-1])[::-1][1:], jnp.full((1,), n_tiles, jnp.int32)])
    next_expert = jnp.where(next_start < n_tiles, tile_expert[jnp.minimum(next_start, n_tiles - 1)], -1)
    runs = jnp.stack([is_start.astype(jnp.int32), slot, next_expert]).astype(jnp.int32)

    xmap = lambda i, te, nv, run: (0, jnp.minimum(i, nv[0] - 1), 0)
    return pl.pallas_call(
        functools.partial(_expert_kernel, slab0=layer * N_EXPERTS),
        out_shape=jax.ShapeDtypeStruct(xs.shape, jnp.uint32),
        grid_spec=pltpu.PrefetchScalarGridSpec(
            num_scalar_prefetch=3,
            grid=(n_tiles,),
            in_specs=[pl.BlockSpec((N_PIECES, tm, PIECE_W), xmap),
                      pl.BlockSpec(memory_space=pl.ANY),
                      pl.BlockSpec(memory_space=pl.ANY),
                      pl.BlockSpec(memory_space=pl.ANY)],
            out_specs=pl.BlockSpec((N_PIECES, tm, PIECE_W), lambda i, te, nv, run: (0, i, 0)),
            scratch_shapes=[pltpu.VMEM((2, d, de), F32), pltpu.VMEM((2, d, de), F32), pltpu.VMEM((2, de, d), F32),
                            pltpu.VMEM((d, de), BF16), pltpu.VMEM((d, de), BF16), pltpu.VMEM((de, d), BF16),
                            pltpu.SemaphoreType.DMA((2, 3))]),
        compiler_params=_cparams(("arbitrary",)),
        name="expert_mlp",
    )(tile_expert, n_valid, runs, xs, wg, wu, wd)


def _combine_kernel(y_ref, route_ref, x_ref, g_ref, b_ref, o_ref, *, alpha):
    o_ref[...] = _combine_rows(y_ref, route_ref, x_ref, g_ref, b_ref, alpha)


def _combine(yg, route, x1, g2, b2, alpha, tm):
    n_rows, d = x1.shape
    row = lambda w: pl.BlockSpec((tm, w), lambda i: (i, 0))
    vec = pl.BlockSpec((1, d), lambda i: (0, 0))
    return pl.pallas_call(
        functools.partial(_combine_kernel, alpha=alpha),
        out_shape=jax.ShapeDtypeStruct((n_rows, d), F32),
        grid=(n_rows // tm,),
        in_specs=[pl.BlockSpec((N_PIECES, 2, tm, PIECE_W), lambda i: (0, 0, i, 0)),
                  row(ROUTE_LANES), row(d), vec, vec],
        out_specs=pl.BlockSpec((tm, d), lambda i: (i, 0)),
        compiler_params=_cparams(("parallel",)),
        name="combine_ln2",
    )(yg, route, x1, g2.reshape(1, d), b2.reshape(1, d))


def _rope_tables(seq):
    half = QK_DIM // 2
    inv_freq = 1.0 / (ROPE_BASE ** (jnp.arange(half, dtype=F32) / half))
    ang = jnp.arange(seq, dtype=F32)[:, None] * inv_freq[None, :]
    cos, sin = jnp.cos(ang), jnp.sin(ang)
    return jnp.concatenate([cos, cos], axis=1), jnp.concatenate([-sin, sin], axis=1)


def _channel_dft_matrix():
    n = np.arange(FG_DIM)
    ang = 2.0 * np.pi * np.outer(n, n) / FG_DIM
    eye = np.eye(N_FGROUPS)
    bd = np.concatenate([np.kron(eye, np.cos(ang)), np.kron(eye, np.sin(ang))], axis=1) / math.sqrt(FG_DIM)
    return jnp.asarray(bd, dtype=BF16)


def _dft_matrix_kernel(ch_ref, sh_ref, cl_ref, sl_ref, c_ref, s_ref):
    cl = cl_ref[...]
    sl = sl_ref[...]
    for blk in range(ch_ref.shape[1]):
        ch = ch_ref[:, blk:blk + 1]
        sh = sh_ref[:, blk:blk + 1]
        cols = slice(blk * LANE, (blk + 1) * LANE)
        c_ref[:, cols] = (ch * cl - sh * sl).astype(BF16)
        s_ref[:, cols] = (sh * cl + ch * sl).astype(BF16)


def _seq_dft_matrices(seq):
    half = seq // 2
    assert half % LANE == 0
    k = jnp.arange(seq, dtype=jnp.int32)[:, None]
    hi = jnp.arange(half // LANE, dtype=jnp.int32)[None, :] * LANE
    lo = jnp.arange(LANE, dtype=jnp.int32)[None, :]
    w = 2.0 * math.pi / seq
    a_hi = ((k * hi) % seq).astype(F32) * w
    a_lo = ((k * lo) % seq).astype(F32) * w
    scale = seq ** -0.5
    tm = min(256, seq)
    nb = half // LANE
    small = lambda width: pl.BlockSpec((tm, width), lambda i: (i, 0))
    return pl.pallas_call(
        _dft_matrix_kernel,
        out_shape=(jax.ShapeDtypeStruct((seq, half), BF16), jax.ShapeDtypeStruct((seq, half), BF16)),
        grid=(seq // tm,),
        in_specs=[small(nb), small(nb), small(LANE), small(LANE)],
        out_specs=(small(half), small(half)),
        compiler_params=_cparams(("parallel",)),
        name="dft_matrices",
    )(jnp.cos(a_hi) * scale, jnp.sin(a_hi) * scale, jnp.cos(a_lo), jnp.sin(a_lo))


def _pad_heads(w, axis):
    shp = list(w.shape)
    shp[axis:axis + 1] = [N_HEADS, V_DIM]
    w = w.reshape(shp)
    pad = [(0, 0)] * w.ndim
    pad[axis + 1] = (0, V_PAD - V_DIM)
    w = jnp.pad(w, pad)
    shp[axis:axis + 2] = [N_HEADS * V_PAD]
    return w.reshape(shp)


def _pad_in_weight(w_in):
    qk_w = 2 * N_HEADS * QK_DIM
    vw = N_HEADS * V_DIM
    qk = w_in[:, :qk_w]
    v = _pad_heads(w_in[:, qk_w:qk_w + vw], 1)
    g = _pad_heads(w_in[:, qk_w + vw:qk_w + 2 * vw], 1)
    rest = w_in[:, qk_w + 2 * vw:]
    return jnp.concatenate([qk, v, g, rest], axis=1).astype(BF16)


def _routing_plan(route, tile_counts, tm_tok, n_tiles, moe_tile):
    t = route.shape[0]
    experts = jnp.arange(N_EXPERTS, dtype=jnp.int32)
    cnt = tile_counts[:, 0, :N_EXPERTS].astype(jnp.int32)
    before = jnp.cumsum(cnt, axis=0) - cnt
    counts = jnp.sum(cnt, axis=0)
    tiles_per = (counts + moe_tile - 1) // moe_tile
    tile_end = jnp.cumsum(tiles_per)
    start_row = (tile_end - tiles_per) * moe_tile
    n_valid = tile_end[-1]

    base = (start_row[None, :] + before).astype(F32)
    base_tok = jnp.broadcast_to(base[:, None, :], (t // tm_tok, tm_tok, N_EXPERTS)).reshape(t, N_EXPERTS)
    e = route[:, 0:2]
    rank = route[:, 4:6]
    sel = e[:, :, None] == experts.astype(F32)[None, None, :]
    pos = (jnp.sum(jnp.where(sel, base_tok[:, None, :], 0.0), axis=-1) + rank).astype(jnp.int32)

    n_pad = n_tiles * moe_tile - 2 * t
    pads = tiles_per * moe_tile - counts
    pad_end = jnp.cumsum(pads)
    p = jnp.arange(n_pad, dtype=jnp.int32)
    pe = jnp.minimum(jnp.sum((p[:, None] >= pad_end[None, :]).astype(jnp.int32), axis=1), N_EXPERTS - 1)
    in_expert = start_row[pe] + counts[pe] + (p - (pad_end[pe] - pads[pe]))
    pad_pos = jnp.where(p < pad_end[-1], in_expert, n_valid * moe_tile + (p - pad_end[-1]))

    tile_ids = jnp.arange(n_tiles, dtype=jnp.int32)
    tile_expert = jnp.minimum(jnp.sum((tile_ids[:, None] >= tile_end[None, :]).astype(jnp.int32), axis=1),
                              N_EXPERTS - 1)
    last_e = tile_expert[jnp.maximum(n_valid - 1, 0)]
    tile_expert = jnp.where(tile_ids < n_valid, tile_expert, last_e).astype(jnp.int32)
    return pos, pad_pos.astype(jnp.int32), tile_expert, n_valid.reshape(1).astype(jnp.int32)


def _largest_tile(cands, *dims):
    for c in cands:
        if all(d % c == 0 for d in dims):
            return c
    raise ValueError(f"no tile in {cands} divides {dims}")


def kernel(x_prompt, x_sample, ln_in_g, ln_in_b, w_in, ret_decay_fwd, ret_decay_bwd, ret_gn_g, ret_gn_b, w_ret_up, w_four_up, w_out, ln1_g, ln1_b, w_route_group, b_route_group, w_route_expert, b_route_expert, w_expert_gate, w_expert_up, w_expert_down, ln2_g, ln2_b):
    bp, sp, d = x_prompt.shape
    bs, ss, _ = x_sample.shape
    depth = w_in.shape[0]
    alpha = float((2 * depth) ** 0.25)

    cos_t, sin_t = _rope_tables(max(sp, ss))
    bd = _channel_dft_matrix()
    dft = {s: _seq_dft_matrices(s) for s in sorted({sp, ss})}
    wg_all = w_expert_gate.reshape((depth * N_EXPERTS,) + w_expert_gate.shape[2:])
    wu_all = w_expert_up.reshape((depth * N_EXPERTS,) + w_expert_up.shape[2:])
    wd_all = w_expert_down.reshape((depth * N_EXPERTS,) + w_expert_down.shape[2:])

    batches = []
    for xin in (x_prompt, x_sample):
        nb, seq, _ = xin.shape
        t = nb * seq
        assert seq % CHUNK == 0 and t % SC_GATHER_WINDOW == 0
        moe_tile = MOE_TILE if 2 * t >= 4 * MOE_TILE * N_EXPERTS else MOE_TILE // 2
        assert (2 * t) % moe_tile == 0
        batches.append(dict(nb=nb, seq=seq, t=t, moe_tile=moe_tile, n_tiles=(2 * t) // moe_tile + N_EXPERTS,
                            tm_in=_largest_tile((512, 256, 128), seq),
                            tm_tok=_largest_tile((TOKEN_TILE, 128), t),
                            tm_merge=_largest_tile((MERGE_TILE, RANK_TILE), t),
                            prologue=(xin.reshape(t, d),)))

    for l in range(depth):
        w_pad = _pad_in_weight(w_in[l])
        dec = jnp.stack([ret_decay_fwd[l], ret_decay_bwd[l]]).astype(F32)
        gn_g = _pad_heads(ret_gn_g[l], 0).reshape(1, -1).astype(F32)
        gn_b = _pad_heads(ret_gn_b[l], 0).reshape(1, -1).astype(F32)
        wr = _pad_heads(w_ret_up[l], 0).astype(BF16)
        wf = w_four_up[l].astype(BF16)
        wo = w_out[l].astype(BF16)
        w_rt = jnp.concatenate([w_route_group[l], w_route_expert[l]], axis=1)
        w_rt = jnp.pad(w_rt, ((0, 0), (0, ROUTE_LANES // 2 - w_rt.shape[1])))
        wrh = w_rt.astype(BF16)
        wr2 = jnp.concatenate([wrh, (w_rt - wrh.astype(F32)).astype(BF16)], axis=1)
        rbias = jnp.concatenate([b_route_group[l], b_route_expert[l]])
        rbias = jnp.pad(rbias, (0, ROUTE_LANES - rbias.shape[0])).reshape(1, ROUTE_LANES).astype(F32)
        pro_g, pro_b = (ln_in_g, ln_in_b) if l == 0 else (ln2_g[l - 1], ln2_b[l - 1])

        for bt in batches:
            nb, seq, t, moe_tile, n_tiles = bt["nb"], bt["seq"], bt["t"], bt["moe_tile"], bt["n_tiles"]
            x, qk, v, sg, z, gates = _inproj(bt["prologue"], pro_g, pro_b, w_pad, cos_t, sin_t, bd,
                                             bt["tm_in"], seq, alpha)
            zrev = _seq_partner_rows(z, nb, seq)
            r = _retention(dec, qk, v, sg, gn_g, gn_b, nb, seq, 0)
            fo = _seq_dft(dft[seq][0], dft[seq][1], z, zrev, nb, seq)
            x1, x1p, route, tile_counts = _merge(r, fo, gates, x, wr, wf, wo, ln1_g[l], ln1_b[l], wr2, rbias,
                                                 alpha, bt["tm_merge"])

            pos, pad_pos, tile_expert, n_valid = _routing_plan(route, tile_counts, RANK_TILE, n_tiles, moe_tile)
            n_sorted = n_tiles * moe_tile
            piece_off = jnp.arange(N_PIECES, dtype=jnp.int32)[:, None] * n_sorted
            dst = jnp.concatenate([(piece_off + seg[None, :]).reshape(-1)
                                   for seg in (pos[:, 0], pos[:, 1], pad_pos)])
            xs = _sc_scatter_rows(x1p.reshape(N_PIECES * t, PIECE_W), dst, N_PIECES * n_sorted)
            ys = _expert_mlp(tile_expert, n_valid, xs.reshape(N_PIECES, n_sorted, PIECE_W),
                             wg_all, wu_all, wd_all, l, moe_tile)
            src = (piece_off[:, :, None] + pos.T[None, :, :]).reshape(-1)
            yg = _sc_gather_rows(ys.reshape(N_PIECES * n_sorted, PIECE_W), src).reshape(N_PIECES, 2, t, PIECE_W)
            bt["prologue"] = (yg, route, x1)

    outs = [_combine(*bt["prologue"], ln2_g[depth - 1], ln2_b[depth - 1], alpha, bt["tm_tok"]) for bt in batches]
    return outs[0].reshape(bp, sp, d), outs[1].reshape(bs, ss, d)
```

```python
import functools
import math

import jax
import jax.numpy as jnp
import numpy as np
from jax import lax
from jax.experimental import pallas as pl
from jax.experimental.pallas import tpu as pltpu
from jax.experimental.pallas import tpu_sc as plsc

N_HEADS = 4
QK_DIM = 128
V_DIM = 192
V_PAD = 256
CHUNK = 128
ROPE_BASE = 10000.0
N_FGROUPS = 4
FG_DIM = 64
F_WIDTH = N_FGROUPS * FG_DIM
N_GROUPS = 4
EXPERTS_PER_GROUP = 8
N_EXPERTS = N_GROUPS * EXPERTS_PER_GROUP
LN_EPS = 1e-5
ROUTE_LANES = 128

LANE = 128
VMEM_LIMIT_BYTES = 56 * 1024 * 1024
MOE_TILE = 512
GATHER_TILE = 256
SC_GATHER_WINDOW = 128
N_PIECES = 2
PIECE_W = 256
TOKEN_TILE = 256
MERGE_TILE = 512
RANK_TILE = 256
RET_UNROLL = 8

F32 = jnp.float32
BF16 = jnp.bfloat16


def _cparams(sem):
    return pltpu.CompilerParams(dimension_semantics=sem, vmem_limit_bytes=VMEM_LIMIT_BYTES)


def _layer_norm_rows(y, g, b):
    mu = jnp.mean(y, axis=-1, keepdims=True)
    d = y - mu
    var = jnp.mean(d * d, axis=-1, keepdims=True)
    return d * lax.rsqrt(var + LN_EPS) * g + b


def _pack_halves(y):
    n = y.shape[1] // 2
    lo = pltpu.bitcast(y[:, :n].astype(BF16).astype(F32), jnp.uint32)
    hi = pltpu.bitcast(y[:, n:].astype(BF16).astype(F32), jnp.uint32)
    return (hi & jnp.uint32(0xFFFF0000)) | (lo >> 16)


def _unpack_halves(w):
    lo = pltpu.bitcast(w << 16, F32)
    hi = pltpu.bitcast(w & jnp.uint32(0xFFFF0000), F32)
    return lo, hi


def _store_pieces(ref, packed):
    for s in range(N_PIECES):
        ref[s] = packed[:, s * PIECE_W:(s + 1) * PIECE_W]


def _unpack_pieces(pieces):
    halves = [_unpack_halves(p) for p in pieces]
    return [h[0] for h in halves] + [h[1] for h in halves]


def _combine_rows(y_ref, route_ref, x_ref, g_ref, b_ref, alpha):
    c1 = route_ref[:, 2:3]
    c2 = route_ref[:, 3:4]
    first = _unpack_pieces([y_ref[s, 0] for s in range(N_PIECES)])
    second = _unpack_pieces([y_ref[s, 1] for s in range(N_PIECES)])
    ff = jnp.concatenate([c1 * a + c2 * b for a, b in zip(first, second)], axis=1)
    return _layer_norm_rows(alpha * x_ref[...] + ff, g_ref[...], b_ref[...])


def _inproj_first_kernel(xraw_ref, g_ref, b_ref, *rest, d_model):
    _project(_layer_norm_rows(xraw_ref[...], g_ref[...], b_ref[...]), *rest, d_model=d_model)


def _inproj_next_kernel(y_ref, route_ref, x1_ref, g_ref, b_ref, *rest, alpha, d_model):
    _project(_combine_rows(y_ref, route_ref, x1_ref, g_ref, b_ref, alpha), *rest, d_model=d_model)


def _project(x_f32, w_ref, cos_ref, sin_ref, bd_ref,
             x_out_ref, qk_ref, v_ref, sg_ref, z_ref, gates_ref, *, d_model):
    x_out_ref[...] = x_f32
    x = x_f32.astype(BF16)
    cos = cos_ref[...]
    sin = sin_ref[...]
    qk_w = 2 * N_HEADS * QK_DIM
    v_w = N_HEADS * V_PAD
    off_v = qk_w
    off_g = off_v + v_w
    off_f = off_g + v_w
    off_gr = off_f + F_WIDTH
    cw = 256

    def proj(c0):
        return jnp.dot(x, w_ref[:, c0:c0 + cw], preferred_element_type=F32)

    for c0 in range(0, qk_w, cw):
        acc = proj(c0)
        scale = QK_DIM ** -0.5 if c0 < N_HEADS * QK_DIM else 1.0
        parts = []
        for hh in range(cw // QK_DIM):
            a = acc[:, hh * QK_DIM:(hh + 1) * QK_DIM]
            rot = a * cos + pltpu.roll(a, QK_DIM // 2, axis=1) * sin
            if scale != 1.0:
                rot = rot * scale
            parts.append(rot.astype(BF16))
        qk_ref[:, c0:c0 + cw] = jnp.concatenate(parts, axis=1)
    for c0 in range(0, v_w, cw):
        v_ref[:, c0:c0 + cw] = proj(off_v + c0).astype(BF16)
    for c0 in range(0, v_w, cw):
        a = proj(off_g + c0)
        sg_ref[:, c0:c0 + cw] = (a * jax.nn.sigmoid(a)).astype(BF16)
    f = proj(off_f).astype(BF16)
    z_ref[...] = _pack_halves(jnp.dot(f, bd_ref[...], preferred_element_type=F32))
    for c0 in range(0, 2 * d_model, cw):
        gates_ref[:, c0:c0 + cw] = jax.nn.sigmoid(proj(off_gr + c0)).astype(BF16)


def _layer_block(stacked, layer):
    zeros = (0,) * (stacked.ndim - 1)
    return pl.BlockSpec((None,) + stacked.shape[1:], lambda *_: (layer,) + zeros)


def _inproj(prologue, g, b, pro_layer, w_pad, layer, cos_t, sin_t, bd, tm, seq, alpha):
    first = len(prologue) == 1
    t, d = prologue[-1].shape
    pos_map = lambda i: (i % (seq // tm), 0)
    row = lambda w: pl.BlockSpec((tm, w), lambda i: (i, 0))
    outs = (jax.ShapeDtypeStruct((t, d), F32),
            jax.ShapeDtypeStruct((t, 2 * N_HEADS * QK_DIM), BF16),
            jax.ShapeDtypeStruct((t, N_HEADS * V_PAD), BF16),
            jax.ShapeDtypeStruct((t, N_HEADS * V_PAD), BF16),
            jax.ShapeDtypeStruct((t, F_WIDTH), jnp.uint32),
            jax.ShapeDtypeStruct((t, 2 * d), BF16))
    if first:
        body = functools.partial(_inproj_first_kernel, d_model=d)
        pro_specs = [row(d)]
    else:
        body = functools.partial(_inproj_next_kernel, alpha=alpha, d_model=d)
        pro_specs = [pl.BlockSpec((N_PIECES, 2, tm, PIECE_W), lambda i: (0, 0, i, 0)), row(ROUTE_LANES), row(d)]
    return pl.pallas_call(
        body,
        out_shape=outs,
        grid=(t // tm,),
        in_specs=pro_specs + [_layer_block(g, pro_layer), _layer_block(b, pro_layer),
                              _layer_block(w_pad, layer),
                              pl.BlockSpec((tm, QK_DIM), pos_map),
                              pl.BlockSpec((tm, QK_DIM), pos_map),
                              pl.BlockSpec((F_WIDTH, 2 * F_WIDTH), lambda i: (0, 0))],
        out_specs=tuple(row(o.shape[1]) for o in outs),
        compiler_params=_cparams(("parallel",)),
        name="inproj",
    )(*prologue, g, b, w_pad, cos_t, sin_t, bd)


def _log_sigmoid(x):
    return jnp.minimum(x, 0.0) - jnp.log1p(jnp.exp(-jnp.abs(x)))


def _retention_kernel(dec_ref, q_ref, k_ref, v_ref, sg_ref, gng_ref, gnb_ref, o_ref,
                      r_all, rf_s, rb_s, *, n_chunks, layer):
    h = pl.program_id(1)
    c = CHUNK
    lf = _log_sigmoid(jnp.full((1, c), dec_ref[layer, 0, h], F32))
    lb = _log_sigmoid(jnp.full((1, c), dec_ref[layer, 1, h], F32))
    lf_w = _log_sigmoid(jnp.full((1, V_PAD), dec_ref[layer, 0, h], F32))
    lb_w = _log_sigmoid(jnp.full((1, V_PAD), dec_ref[layer, 1, h], F32))
    row = lax.broadcasted_iota(jnp.int32, (c, c), 0).astype(F32)
    col = lax.broadcasted_iota(jnp.int32, (c, c), 1).astype(F32)
    diff = row - col
    dmat = jnp.where(diff >= 0.0, jnp.exp(lf * jnp.maximum(diff, 0.0)),
                     jnp.exp(lb * jnp.maximum(-diff, 0.0)))
    qd_f = jnp.exp(lf * (row + 1.0))
    kd_f = jnp.exp(lf * (c - 1.0 - row))
    qd_b = jnp.exp(lb * (c - row))
    kd_b = jnp.exp(lb * row)
    cf = jnp.exp(lf_w * float(c))
    cb = jnp.exp(lb_w * float(c))
    gn_g = gng_ref[...]
    gn_b = gnb_ref[...]
    tdot = (((0,), (0,)), ((), ()))

    rf_s[...] = jnp.zeros_like(rf_s)
    rb_s[...] = jnp.zeros_like(rb_s)

    def state_step(t, carry):
        for (ci, kd, cdec, r_s, half) in ((t, kd_f, cf, rf_s, 0), (n_chunks - 1 - t, kd_b, cb, rb_s, 1)):
            r0 = pl.multiple_of(ci * c, c)
            kk = (k_ref[pl.ds(r0, c), :].astype(F32) * kd).astype(BF16)
            kv = lax.dot_general(kk, v_ref[pl.ds(r0, c), :], tdot, preferred_element_type=F32)
            r = r_s[...]
            r_all[ci, half * c:(half + 1) * c, :] = r.astype(BF16)
            r_s[...] = cdec * r + kv
        return carry

    lax.fori_loop(0, n_chunks, state_step, 0, unroll=RET_UNROLL)

    def out_step(ci, carry):
        r0 = pl.multiple_of(ci * c, c)
        q = q_ref[pl.ds(r0, c), :]
        qf = q.astype(F32)
        s = lax.dot_general(q, k_ref[pl.ds(r0, c), :], (((1,), (1,)), ((), ())), preferred_element_type=F32)
        p = (s * dmat).astype(BF16)
        o = jnp.dot(p, v_ref[pl.ds(r0, c), :], preferred_element_type=F32)
        qq = jnp.concatenate([(qf * qd_f).astype(BF16), (qf * qd_b).astype(BF16)], axis=1)
        o = o + jnp.dot(qq, r_all[ci], preferred_element_type=F32)
        mu = jnp.sum(o, axis=-1, keepdims=True) * (1.0 / V_DIM)
        d = o - mu
        var = (jnp.sum(d * d, axis=-1, keepdims=True) - float(V_PAD - V_DIM) * mu * mu) * (1.0 / V_DIM)
        y = d * lax.rsqrt(var + LN_EPS) * gn_g + gn_b
        o_ref[pl.ds(r0, c), :] = (y * sg_ref[pl.ds(r0, c), :].astype(F32)).astype(BF16)
        return carry

    lax.fori_loop(0, n_chunks, out_step, 0, unroll=RET_UNROLL)


def _retention(dec, qk, v, sg, gn_g, gn_b, layer, n_batch, seq):
    n_chunks = seq // CHUNK
    rb = lambda b: b
    return pl.pallas_call(
        functools.partial(_retention_kernel, n_chunks=n_chunks, layer=layer),
        out_shape=jax.ShapeDtypeStruct((n_batch * seq, N_HEADS * V_PAD), BF16),
        grid_spec=pltpu.PrefetchScalarGridSpec(
            num_scalar_prefetch=0,
            grid=(n_batch, N_HEADS),
            in_specs=[pl.BlockSpec(memory_space=pltpu.SMEM),
                      pl.BlockSpec((seq, QK_DIM), lambda b, h: (rb(b), h)),
                      pl.BlockSpec((seq, QK_DIM), lambda b, h: (rb(b), N_HEADS + h)),
                      pl.BlockSpec((seq, V_PAD), lambda b, h: (rb(b), h)),
                      pl.BlockSpec((seq, V_PAD), lambda b, h: (rb(b), h)),
                      pl.BlockSpec((None, 1, V_PAD), lambda b, h: (layer, 0, h)),
                      pl.BlockSpec((None, 1, V_PAD), lambda b, h: (layer, 0, h))],
            out_specs=pl.BlockSpec((seq, V_PAD), lambda b, h: (b, h)),
            scratch_shapes=[pltpu.VMEM((n_chunks, 2 * CHUNK, V_PAD), BF16),
                            pltpu.VMEM((CHUNK, V_PAD), F32),
                            pltpu.VMEM((CHUNK, V_PAD), F32)]),
        compiler_params=_cparams(("parallel", "arbitrary")),
        name="retention",
    )(dec, qk, qk, v, sg, gn_g, gn_b)


def _seqdft_kernel(c_ref, s_ref, z_ref, zrev_ref, zmid_ref, o_ref, acc_ref, *, scale):
    kk = pl.program_id(1)
    b = pl.program_id(2)
    zc, zs = _unpack_halves(z_ref[...])
    rc, rs = _unpack_halves(zrev_ref[...])
    no_partner = (lax.broadcasted_iota(jnp.int32, zc.shape, 0) == 0) & (kk == 0)
    fold_c = (zc + jnp.where(no_partner, 0.0, rc)).astype(BF16)
    fold_s = (zs - jnp.where(no_partner, 0.0, rs)).astype(BF16)
    part = (jnp.dot(c_ref[...], fold_c, preferred_element_type=F32)
            - jnp.dot(s_ref[...], fold_s, preferred_element_type=F32))

    @pl.when(kk == 0)
    def _():
        acc_ref[b] = part

    @pl.when(kk > 0)
    def _():
        acc_ref[b] += part

    @pl.when(kk == pl.num_programs(1) - 1)
    def _():
        tm = acc_ref.shape[1]
        odd = lax.broadcasted_iota(jnp.int32, (tm, F_WIDTH), 0) % 2 == 1
        mid = zmid_ref[b, 0:1, :] * scale
        o_ref[b] = (acc_ref[b] + jnp.where(odd, -mid, mid)).astype(o_ref.dtype)


def _seq_partner_rows(z, n_batch, seq):
    half = seq // 2
    assert (n_batch * half) % SC_GATHER_WINDOW == 0
    n = jnp.arange(half, dtype=jnp.int32)[None, :]
    partner = (jnp.arange(n_batch, dtype=jnp.int32)[:, None] * seq + (seq - n) % seq).reshape(-1)
    return _sc_gather_rows(z, partner).reshape(n_batch, half, F_WIDTH)


def _seq_dft(cmat, smat, z, zrev, n_batch, seq):
    half = seq // 2
    tm = min(1024, seq)
    tk = min(1024, half)
    assert tm % 2 == 0
    z3 = z.reshape(n_batch, seq, F_WIDTH)
    zmid = lax.bitcast_convert_type(z3[:, half, :] << 16, F32)
    zmid = jnp.broadcast_to(zmid[:, None, :], (n_batch, 8, F_WIDTH))
    zblock = pl.BlockSpec((None, tk, F_WIDTH), lambda i, k, b: (b, k, 0))
    out = pl.pallas_call(
        functools.partial(_seqdft_kernel, scale=seq ** -0.5),
        out_shape=jax.ShapeDtypeStruct((n_batch, seq, F_WIDTH), BF16),
        grid=(seq // tm, half // tk, n_batch),
        in_specs=[pl.BlockSpec((tm, tk), lambda i, k, b: (i, k)),
                  pl.BlockSpec((tm, tk), lambda i, k, b: (i, k)),
                  zblock, zblock,
                  pl.BlockSpec((n_batch, 8, F_WIDTH), lambda i, k, b: (0, 0, 0))],
        out_specs=pl.BlockSpec((n_batch, tm, F_WIDTH), lambda i, k, b: (0, i, 0)),
        scratch_shapes=[pltpu.VMEM((n_batch, tm, F_WIDTH), F32)],
        compiler_params=_cparams(("parallel", "arbitrary", "arbitrary")),
        name="seq_dft",
    )(cmat, smat, z3, zrev, zmid)
    return out.reshape(n_batch * seq, F_WIDTH)


def _merge_kernel(r_ref, fo_ref, gates_ref, x_ref, wr_ref, wf_ref, wo_ref, g1_ref, b1_ref,
                  wr2_ref, rb_ref, x1_ref, x1p_ref, route_ref, cnt_ref, *, alpha, d_model):
    for j in range(x_ref.shape[0] // RANK_TILE):
        rows = pl.ds(j * RANK_TILE, RANK_TILE)
        _merge_subtile(rows, j, r_ref, fo_ref, gates_ref, x_ref, wr_ref, wf_ref, wo_ref,
                       g1_ref, b1_ref, wr2_ref, rb_ref, x1_ref, x1p_ref, route_ref, cnt_ref,
                       alpha=alpha, d_model=d_model)


def _merge_subtile(rows, j, r_ref, fo_ref, gates_ref, x_ref, wr_ref, wf_ref, wo_ref,
                   g1_ref, b1_ref, wr2_ref, rb_ref, x1_ref, x1p_ref, route_ref, cnt_ref, *, alpha, d_model):
    a = jnp.dot(r_ref[rows, :], wr_ref[...], preferred_element_type=F32)
    b = jnp.dot(fo_ref[rows, :], wf_ref[...], preferred_element_type=F32)
    merged = (gates_ref[rows, :d_model].astype(F32) * a + gates_ref[rows, d_model:].astype(F32) * b)
    mix = jnp.dot(merged.astype(BF16), wo_ref[...], preferred_element_type=F32)
    x1 = _layer_norm_rows(alpha * x_ref[rows, :] + mix, g1_ref[...], b1_ref[...])
    x1_ref[rows, :] = x1
    packed = _pack_halves(x1)
    for s in range(N_PIECES):
        x1p_ref[s, rows, :] = packed[:, s * PIECE_W:(s + 1) * PIECE_W]

    xh = x1.astype(BF16)
    xl = (x1 - xh.astype(F32)).astype(BF16)
    s2 = (jnp.dot(xh, wr2_ref[...], preferred_element_type=F32)
          + jnp.dot(xl, wr2_ref[...], preferred_element_type=F32))
    logits = s2 + pltpu.roll(s2, ROUTE_LANES // 2, axis=1) + rb_ref[...]
    tm = logits.shape[0]
    lane = lax.broadcasted_iota(jnp.int32, (tm, ROUTE_LANES), 1)
    neg = jnp.float32(-jnp.inf)
    big = jnp.int32(ROUTE_LANES)

    def top1(mask):
        vals = jnp.where(mask, logits, neg)
        m = jnp.max(vals, axis=-1, keepdims=True)
        idx = jnp.min(jnp.where(mask & (vals == m), lane, big), axis=-1, keepdims=True)
        return m, idx

    gmask = lane < N_GROUPS
    gm, gi = top1(gmask)
    gsum = jnp.sum(jnp.where(gmask, jnp.exp(logits - gm), 0.0), axis=-1, keepdims=True)
    g_gate = 1.0 / gsum
    lo = N_GROUPS + gi * EXPERTS_PER_GROUP
    emask = (lane >= lo) & (lane < lo + EXPERTS_PER_GROUP)
    m1, i1 = top1(emask)
    m2, i2 = top1(emask & (lane != i1))
    e2 = jnp.exp(m2 - m1)
    w1 = 1.0 / (1.0 + e2)
    w2 = e2 / (1.0 + e2)
    oh1 = (lane == i1 - N_GROUPS).astype(F32)
    oh2 = (lane == i2 - N_GROUPS).astype(F32)
    earlier = (lax.broadcasted_iota(jnp.int32, (tm, tm), 1)
               < lax.broadcasted_iota(jnp.int32, (tm, tm), 0)).astype(BF16)
    tot1 = jnp.sum(oh1, axis=0, keepdims=True)
    tot2 = jnp.sum(oh2, axis=0, keepdims=True)
    pre1 = jnp.dot(earlier, oh1.astype(BF16), preferred_element_type=F32)
    pre2 = jnp.dot(earlier, oh2.astype(BF16), preferred_element_type=F32) + tot1
    r1 = jnp.sum(pre1 * oh1, axis=-1, keepdims=True)
    r2 = jnp.sum(pre2 * oh2, axis=-1, keepdims=True)
    cnt_ref[j] = tot1 + tot2

    rec = jnp.where(lane == 0, (i1 - N_GROUPS).astype(F32), 0.0)
    rec = jnp.where(lane == 1, (i2 - N_GROUPS).astype(F32), rec)
    rec = jnp.where(lane == 2, g_gate * w1, rec)
    rec = jnp.where(lane == 3, g_gate * w2, rec)
    rec = jnp.where(lane == 4, r1, rec)
    rec = jnp.where(lane == 5, r2, rec)
    route_ref[rows, :] = rec


def _merge(r, fo, gates, x, wr, wf, wo, g1, b1, wr2, rbias, layer, alpha, tm):
    t, d = x.shape
    row = lambda w: pl.BlockSpec((tm, w), lambda i: (i, 0))
    full = lambda a: _layer_block(a, layer)
    return pl.pallas_call(
        functools.partial(_merge_kernel, alpha=alpha, d_model=d),
        out_shape=(jax.ShapeDtypeStruct((t, d), F32),
                   jax.ShapeDtypeStruct((N_PIECES, t, PIECE_W), jnp.uint32),
                   jax.ShapeDtypeStruct((t, ROUTE_LANES), F32),
                   jax.ShapeDtypeStruct((t // RANK_TILE, 1, ROUTE_LANES), F32)),
        grid=(t // tm,),
        in_specs=[row(r.shape[1]), row(fo.shape[1]), row(gates.shape[1]), row(d),
                  full(wr), full(wf), full(wo), full(g1), full(b1), full(wr2), full(rbias)],
        out_specs=(row(d), pl.BlockSpec((N_PIECES, tm, PIECE_W), lambda i: (0, i, 0)), row(ROUTE_LANES),
                   pl.BlockSpec((tm // RANK_TILE, 1, ROUTE_LANES), lambda i: (i, 0, 0))),
        compiler_params=_cparams(("parallel",)),
        name="merge_router",
    )(r, fo, gates, x, wr, wf, wo, g1, b1, wr2, rbias)


def _sc_mesh():
    return plsc.VectorSubcoreMesh(core_axis_name="c", subcore_axis_name="s")


def _sc_gather_rows(table, idx):
    n, w = idx.shape[0], table.shape[1]
    idx = idx.reshape(1, n)

    @functools.partial(pl.kernel, out_type=jax.ShapeDtypeStruct((n, w), table.dtype), mesh=_sc_mesh(),
                       scratch_types=[])
    def gather(tab_hbm, idx_hbm, out_hbm):
        def body(idx_vmem, out_vmem):
            pltpu.sync_copy(tab_hbm.at[idx_vmem.at[0]], out_vmem)

        pltpu.emit_pipeline(
            body,
            grid=(n // SC_GATHER_WINDOW,),
            in_specs=[pl.BlockSpec((1, SC_GATHER_WINDOW), lambda i: (0, i))],
            out_specs=[pl.BlockSpec((SC_GATHER_WINDOW, w), lambda i: (i, 0))],
            core_axis_name=("c", "s"),
            dimension_semantics=(pltpu.PARALLEL,),
        )(idx_hbm, out_hbm)

    return gather(table, idx)


def _sc_scatter_rows(src, idx, n_out):
    assert idx.shape[0] == n_out
    n, w = n_out, src.shape[1]
    idx = idx.reshape(1, n)
    src_windows = src.shape[0] // SC_GATHER_WINDOW

    @functools.partial(pl.kernel, out_type=jax.ShapeDtypeStruct((n, w), src.dtype), mesh=_sc_mesh(),
                       scratch_types=[])
    def scatter(src_hbm, idx_hbm, out_hbm):
        def body(src_vmem, idx_vmem):
            pltpu.sync_copy(src_vmem, out_hbm.at[idx_vmem.at[0]])

        pltpu.emit_pipeline(
            body,
            grid=(n // SC_GATHER_WINDOW,),
            in_specs=[pl.BlockSpec((SC_GATHER_WINDOW, w), lambda i: (i % src_windows, 0)),
                      pl.BlockSpec((1, SC_GATHER_WINDOW), lambda i: (0, i))],
            out_specs=[],
            core_axis_name=("c", "s"),
            dimension_semantics=(pltpu.PARALLEL,),
        )(src_hbm, idx_hbm)

    return scatter(src, idx)


def _expert_kernel(te_ref, nv_ref, run_ref, xs_ref, wg_hbm, wu_hbm, wd_hbm, o_ref,
                   wg_f, wu_f, wd_f, wg_s, wu_s, wd_s, sems, *, slab0):
    i = pl.program_id(0)

    def weight_copies(expert, slot):
        return [pltpu.make_async_copy(hbm.at[slab0 + expert], buf.at[slot], sems.at[slot, k])
                for k, (hbm, buf) in enumerate(((wg_hbm, wg_f), (wu_hbm, wu_f), (wd_hbm, wd_f)))]

    @pl.when(i == 0)
    def _():
        for c in weight_copies(te_ref[0], 0):
            c.start()

    @pl.when(run_ref[0, i] == 1)
    def _():
        slot = run_ref[1, i]
        for c in weight_copies(te_ref[i], slot):
            c.wait()

        @pl.when(run_ref[2, i] >= 0)
        def _():
            for c in weight_copies(run_ref[2, i], 1 - slot):
                c.start()

        wg_s[...] = wg_f[slot].astype(BF16)
        wu_s[...] = wu_f[slot].astype(BF16)
        wd_s[...] = wd_f[slot].astype(BF16)

    @pl.when(i < nv_ref[0])
    def _():
        chunks = [c.astype(BF16) for c in _unpack_pieces([xs_ref[s] for s in range(N_PIECES)])]
        cw = chunks[0].shape[1]

        def up(w_s):
            acc = jnp.dot(chunks[0], w_s[:cw, :], preferred_element_type=F32)
            for j in range(1, len(chunks)):
                acc = acc + jnp.dot(chunks[j], w_s[j * cw:(j + 1) * cw, :], preferred_element_type=F32)
            return acc

        g = up(wg_s)
        hmid = (g * jax.nn.sigmoid(g) * up(wu_s)).astype(BF16)
        _store_pieces(o_ref, _pack_halves(jnp.dot(hmid, wd_s[...], preferred_element_type=F32)))

    @pl.when(i >= nv_ref[0])
    def _():
        o_ref[...] = jnp.zeros_like(o_ref)


def _expert_mlp(tile_expert, n_valid, xs, wg, wu, wd, layer, tm):
    p = xs.shape[1]
    n_tiles = p // tm
    d, de = wg.shape[1], wg.shape[2]
    tiles = jnp.arange(n_tiles, dtype=jnp.int32)
    prev = jnp.concatenate([tile_expert[:1], tile_expert[:-1]])
    is_start = (tiles == 0) | (tile_expert != prev)
    slot = (jnp.cumsum(is_start.astype(jnp.int32)) - 1) % 2
    start_at = jnp.where(is_start, tiles, n_tiles)
    next_start = jnp.concatenate([lax.cummin(start_at[::-1])[::-1][1:], jnp.full((1,), n_tiles, jnp.int32)])
    next_expert = jnp.where(next_start < n_tiles, tile_expert[jnp.minimum(next_start, n_tiles - 1)], -1)
    runs = jnp.stack([is_start.astype(jnp.int32), slot, next_expert]).astype(jnp.int32)

    xmap = lambda i, te, nv, run: (0, jnp.minimum(i, nv[0] - 1), 0)
    return pl.pallas_call(
        functools.partial(_expert_kernel, slab0=layer * N_EXPERTS),
        out_shape=jax.ShapeDtypeStruct(xs.shape, jnp.uint32),
        grid_spec=pltpu.PrefetchScalarGridSpec(
            num_scalar_prefetch=3,
            grid=(n_tiles,),
            in_specs=[pl.BlockSpec((N_PIECES, tm, PIECE_W), xmap),
                      pl.BlockSpec(memory_space=pl.ANY),
                      pl.BlockSpec(memory_space=pl.ANY),
                      pl.BlockSpec(memory_space=pl.ANY)],
            out_specs=pl.BlockSpec((N_PIECES, tm, PIECE_W), lambda i, te, nv, run: (0, i, 0)),
            scratch_shapes=[pltpu.VMEM((2, d, de), F32), pltpu.VMEM((2, d, de), F32), pltpu.VMEM((2, de, d), F32),
                            pltpu.VMEM((d, de), BF16), pltpu.VMEM((d, de), BF16), pltpu.VMEM((de, d), BF16),
                            pltpu.SemaphoreType.DMA((2, 3))]),
        compiler_params=_cparams(("arbitrary",)),
        name="expert_mlp",
    )(tile_expert, n_valid, runs, xs, wg, wu, wd)


def _combine_kernel(y_ref, route_ref, x_ref, g_ref, b_ref, o_ref, *, alpha):
    o_ref[...] = _combine_rows(y_ref, route_ref, x_ref, g_ref, b_ref, alpha)


def _combine(yg, route, x1, g2, b2, layer, alpha, tm):
    n_rows, d = x1.shape
    row = lambda w: pl.BlockSpec((tm, w), lambda i: (i, 0))
    return pl.pallas_call(
        functools.partial(_combine_kernel, alpha=alpha),
        out_shape=jax.ShapeDtypeStruct((n_rows, d), F32),
        grid=(n_rows // tm,),
        in_specs=[pl.BlockSpec((N_PIECES, 2, tm, PIECE_W), lambda i: (0, 0, i, 0)),
                  row(ROUTE_LANES), row(d), _layer_block(g2, layer), _layer_block(b2, layer)],
        out_specs=pl.BlockSpec((tm, d), lambda i: (i, 0)),
        compiler_params=_cparams(("parallel",)),
        name="combine_ln2",
    )(yg, route, x1, g2, b2)


def _rope_tables(seq):
    half = QK_DIM // 2
    inv_freq = 1.0 / (ROPE_BASE ** (jnp.arange(half, dtype=F32) / half))
    ang = jnp.arange(seq, dtype=F32)[:, None] * inv_freq[None, :]
    cos, sin = jnp.cos(ang), jnp.sin(ang)
    return jnp.concatenate([cos, cos], axis=1), jnp.concatenate([-sin, sin], axis=1)


def _channel_dft_matrix():
    n = np.arange(FG_DIM)
    ang = 2.0 * np.pi * np.outer(n, n) / FG_DIM
    eye = np.eye(N_FGROUPS)
    bd = np.concatenate([np.kron(eye, np.cos(ang)), np.kron(eye, np.sin(ang))], axis=1) / math.sqrt(FG_DIM)
    return jnp.asarray(bd, dtype=BF16)


def _dft_matrix_kernel(ch_ref, sh_ref, cl_ref, sl_ref, c_ref, s_ref):
    cl = cl_ref[...]
    sl = sl_ref[...]
    for blk in range(ch_ref.shape[1]):
        ch = ch_ref[:, blk:blk + 1]
        sh = sh_ref[:, blk:blk + 1]
        cols = slice(blk * LANE, (blk + 1) * LANE)
        c_ref[:, cols] = (ch * cl - sh * sl).astype(BF16)
        s_ref[:, cols] = (sh * cl + ch * sl).astype(BF16)


def _seq_dft_matrices(seq):
    half = seq // 2
    assert half % LANE == 0
    k = jnp.arange(seq, dtype=jnp.int32)[:, None]
    hi = jnp.arange(half // LANE, dtype=jnp.int32)[None, :] * LANE
    lo = jnp.arange(LANE, dtype=jnp.int32)[None, :]
    w = 2.0 * math.pi / seq
    a_hi = ((k * hi) % seq).astype(F32) * w
    a_lo = ((k * lo) % seq).astype(F32) * w
    scale = seq ** -0.5
    tm = min(256, seq)
    nb = half // LANE
    small = lambda width: pl.BlockSpec((tm, width), lambda i: (i, 0))
    return pl.pallas_call(
        _dft_matrix_kernel,
        out_shape=(jax.ShapeDtypeStruct((seq, half), BF16), jax.ShapeDtypeStruct((seq, half), BF16)),
        grid=(seq // tm,),
        in_specs=[small(nb), small(nb), small(LANE), small(LANE)],
        out_specs=(small(half), small(half)),
        compiler_params=_cparams(("parallel",)),
        name="dft_matrices",
    )(jnp.cos(a_hi) * scale, jnp.sin(a_hi) * scale, jnp.cos(a_lo), jnp.sin(a_lo))


def _pad_heads(w, axis):
    shp = list(w.shape)
    shp[axis:axis + 1] = [N_HEADS, V_DIM]
    w = w.reshape(shp)
    pad = [(0, 0)] * w.ndim
    pad[axis + 1] = (0, V_PAD - V_DIM)
    w = jnp.pad(w, pad)
    shp[axis:axis + 2] = [N_HEADS * V_PAD]
    return w.reshape(shp)


def _pad_in_weight(w_in):
    qk_w = 2 * N_HEADS * QK_DIM
    vw = N_HEADS * V_DIM
    qk = w_in[..., :qk_w]
    v = _pad_heads(w_in[..., qk_w:qk_w + vw], 2)
    g = _pad_heads(w_in[..., qk_w + vw:qk_w + 2 * vw], 2)
    rest = w_in[..., qk_w + 2 * vw:]
    return jnp.concatenate([qk, v, g, rest], axis=2).astype(BF16)


def _routing_plan(route, tile_counts, tm_tok, n_tiles, moe_tile):
    t = route.shape[0]
    experts = jnp.arange(N_EXPERTS, dtype=jnp.int32)
    cnt = tile_counts[:, 0, :N_EXPERTS].astype(jnp.int32)
    before = jnp.cumsum(cnt, axis=0) - cnt
    counts = jnp.sum(cnt, axis=0)
    tiles_per = (counts + moe_tile - 1) // moe_tile
    tile_end = jnp.cumsum(tiles_per)
    start_row = (tile_end - tiles_per) * moe_tile
    n_valid = tile_end[-1]

    base = (start_row[None, :] + before).astype(F32)
    base_tok = jnp.broadcast_to(base[:, None, :], (t // tm_tok, tm_tok, N_EXPERTS)).reshape(t, N_EXPERTS)
    e = route[:, 0:2]
    rank = route[:, 4:6]
    sel = e[:, :, None] == experts.astype(F32)[None, None, :]
    pos = (jnp.sum(jnp.where(sel, base_tok[:, None, :], 0.0), axis=-1) + rank).astype(jnp.int32)

    n_pad = n_tiles * moe_tile - 2 * t
    pads = tiles_per * moe_tile - counts
    pad_end = jnp.cumsum(pads)
    p = jnp.arange(n_pad, dtype=jnp.int32)
    pe = jnp.minimum(jnp.sum((p[:, None] >= pad_end[None, :]).astype(jnp.int32), axis=1), N_EXPERTS - 1)
    in_expert = start_row[pe] + counts[pe] + (p - (pad_end[pe] - pads[pe]))
    pad_pos = jnp.where(p < pad_end[-1], in_expert, n_valid * moe_tile + (p - pad_end[-1]))

    tile_ids = jnp.arange(n_tiles, dtype=jnp.int32)
    tile_expert = jnp.minimum(jnp.sum((tile_ids[:, None] >= tile_end[None, :]).astype(jnp.int32), axis=1),
                              N_EXPERTS - 1)
    last_e = tile_expert[jnp.maximum(n_valid - 1, 0)]
    tile_expert = jnp.where(tile_ids < n_valid, tile_expert, last_e).astype(jnp.int32)
    return pos, pad_pos.astype(jnp.int32), tile_expert, n_valid.reshape(1).astype(jnp.int32)


def _largest_tile(cands, *dims):
    for c in cands:
        if all(d % c == 0 for d in dims):
            return c
    raise ValueError(f"no tile in {cands} divides {dims}")


def kernel(x_prompt, x_sample, ln_in_g, ln_in_b, w_in, ret_decay_fwd, ret_decay_bwd, ret_gn_g, ret_gn_b, w_ret_up, w_four_up, w_out, ln1_g, ln1_b, w_route_group, b_route_group, w_route_expert, b_route_expert, w_expert_gate, w_expert_up, w_expert_down, ln2_g, ln2_b):
    bp, sp, d = x_prompt.shape
    bs, ss, _ = x_sample.shape
    depth = w_in.shape[0]
    alpha = float((2 * depth) ** 0.25)

    cos_t, sin_t = _rope_tables(max(sp, ss))
    bd = _channel_dft_matrix()
    dft = {s: _seq_dft_matrices(s) for s in sorted({sp, ss})}
    wg_all = w_expert_gate.reshape((depth * N_EXPERTS,) + w_expert_gate.shape[2:])
    wu_all = w_expert_up.reshape((depth * N_EXPERTS,) + w_expert_up.shape[2:])
    wd_all = w_expert_down.reshape((depth * N_EXPERTS,) + w_expert_down.shape[2:])

    batches = []
    for xin in (x_prompt, x_sample):
        nb, seq, _ = xin.shape
        t = nb * seq
        assert seq % CHUNK == 0 and t % SC_GATHER_WINDOW == 0
        moe_tile = MOE_TILE if 2 * t >= 4 * MOE_TILE * N_EXPERTS else MOE_TILE // 2
        assert (2 * t) % moe_tile == 0
        batches.append(dict(nb=nb, seq=seq, t=t, moe_tile=moe_tile, n_tiles=(2 * t) // moe_tile + N_EXPERTS,
                            tm_in=_largest_tile((512, 256, 128), seq),
                            tm_tok=_largest_tile((TOKEN_TILE, 128), t),
                            tm_merge=_largest_tile((MERGE_TILE, RANK_TILE), t),
                            prologue=(xin.reshape(t, d),)))

    vec3 = lambda a: a.reshape(a.shape[0], 1, a.shape[-1]).astype(F32)
    w_pad = _pad_in_weight(w_in)
    dec = jnp.stack([ret_decay_fwd, ret_decay_bwd], axis=1).astype(F32)
    gn_g = vec3(_pad_heads(ret_gn_g, 1))
    gn_b = vec3(_pad_heads(ret_gn_b, 1))
    wr = _pad_heads(w_ret_up, 1).astype(BF16)
    wf = w_four_up.astype(BF16)
    wo = w_out.astype(BF16)
    w_rt = jnp.concatenate([w_route_group, w_route_expert], axis=2)
    w_rt = jnp.pad(w_rt, ((0, 0), (0, 0), (0, ROUTE_LANES // 2 - w_rt.shape[2])))
    wrh = w_rt.astype(BF16)
    wr2 = jnp.concatenate([wrh, (w_rt - wrh.astype(F32)).astype(BF16)], axis=2)
    rbias = jnp.concatenate([b_route_group, b_route_expert], axis=1)
    rbias = vec3(jnp.pad(rbias, ((0, 0), (0, ROUTE_LANES - rbias.shape[1]))))
    ln_in = (vec3(ln_in_g[None]), vec3(ln_in_b[None]))
    ln1 = (vec3(ln1_g), vec3(ln1_b))
    ln2 = (vec3(ln2_g), vec3(ln2_b))

    for l in range(depth):
        pro_g, pro_b, pro_layer = (*ln_in, 0) if l == 0 else (*ln2, l - 1)

        for bt in batches:
            nb, seq, t, moe_tile, n_tiles = bt["nb"], bt["seq"], bt["t"], bt["moe_tile"], bt["n_tiles"]
            x, qk, v, sg, z, gates = _inproj(bt["prologue"], pro_g, pro_b, pro_layer, w_pad, l, cos_t, sin_t, bd,
                                             bt["tm_in"], seq, alpha)
            zrev = _seq_partner_rows(z, nb, seq)
            r = _retention(dec, qk, v, sg, gn_g, gn_b, l, nb, seq)
            fo = _seq_dft(dft[seq][0], dft[seq][1], z, zrev, nb, seq)
            x1, x1p, route, tile_counts = _merge(r, fo, gates, x, wr, wf, wo, *ln1, wr2, rbias, l,
                                                 alpha, bt["tm_merge"])

            pos, pad_pos, tile_expert, n_valid = _routing_plan(route, tile_counts, RANK_TILE, n_tiles, moe_tile)
            n_sorted = n_tiles * moe_tile
            piece_off = jnp.arange(N_PIECES, dtype=jnp.int32)[:, None] * n_sorted
            dst = jnp.concatenate([(piece_off + seg[None, :]).reshape(-1)
                                   for seg in (pos[:, 0], pos[:, 1], pad_pos)])
            xs = _sc_scatter_rows(x1p.reshape(N_PIECES * t, PIECE_W), dst, N_PIECES * n_sorted)
            ys = _expert_mlp(tile_expert, n_valid, xs.reshape(N_PIECES, n_sorted, PIECE_W),
                             wg_all, wu_all, wd_all, l, moe_tile)
            src = (piece_off[:, :, None] + pos.T[None, :, :]).reshape(-1)
            yg = _sc_gather_rows(ys.reshape(N_PIECES * n_sorted, PIECE_W), src).reshape(N_PIECES, 2, t, PIECE_W)
            bt["prologue"] = (yg, route, x1)

    outs = [_combine(*bt["prologue"], *ln2, depth - 1, alpha, bt["tm_tok"]) for bt in batches]
    return outs[0].reshape(bp, sp, d), outs[1].reshape(bs, ss, d)
```

```python
import functools
import math

import jax
import jax.numpy as jnp
import numpy as np
from jax import lax
from jax.experimental import pallas as pl
from jax.experimental.pallas import tpu as pltpu
from jax.experimental.pallas import tpu_sc as plsc

N_HEADS = 4
QK_DIM = 128
V_DIM = 192
V_PAD = 256
CHUNK = 128
ROPE_BASE = 10000.0
N_FGROUPS = 4
FG_DIM = 64
F_WIDTH = N_FGROUPS * FG_DIM
N_GROUPS = 4
EXPERTS_PER_GROUP = 8
N_EXPERTS = N_GROUPS * EXPERTS_PER_GROUP
LN_EPS = 1e-5
ROUTE_LANES = 128

LANE = 128
VMEM_LIMIT_BYTES = 56 * 1024 * 1024
MOE_TILE = 512
GATHER_TILE = 256
SC_GATHER_WINDOW = 128
N_PIECES = 2
PIECE_W = 256
TOKEN_TILE = 256
MERGE_TILE = 512
RANK_TILE = 256
RET_UNROLL = 16

F32 = jnp.float32
BF16 = jnp.bfloat16


def _cparams(sem):
    return pltpu.CompilerParams(dimension_semantics=sem, vmem_limit_bytes=VMEM_LIMIT_BYTES)


def _layer_norm_rows(y, g, b):
    mu = jnp.mean(y, axis=-1, keepdims=True)
    d = y - mu
    var = jnp.mean(d * d, axis=-1, keepdims=True)
    return d * lax.rsqrt(var + LN_EPS) * g + b


def _pack_halves(y):
    n = y.shape[1] // 2
    lo = pltpu.bitcast(y[:, :n].astype(BF16).astype(F32), jnp.uint32)
    hi = pltpu.bitcast(y[:, n:].astype(BF16).astype(F32), jnp.uint32)
    return (hi & jnp.uint32(0xFFFF0000)) | (lo >> 16)


def _unpack_halves(w):
    lo = pltpu.bitcast(w << 16, F32)
    hi = pltpu.bitcast(w & jnp.uint32(0xFFFF0000), F32)
    return lo, hi


def _store_pieces(ref, packed):
    for s in range(N_PIECES):
        ref[s] = packed[:, s * PIECE_W:(s + 1) * PIECE_W]


def _unpack_pieces(pieces):
    halves = [_unpack_halves(p) for p in pieces]
    return [h[0] for h in halves] + [h[1] for h in halves]


def _combine_rows(y_ref, route_ref, x_ref, g_ref, b_ref, alpha):
    c1 = route_ref[:, 2:3]
    c2 = route_ref[:, 3:4]
    first = _unpack_pieces([y_ref[s, 0] for s in range(N_PIECES)])
    second = _unpack_pieces([y_ref[s, 1] for s in range(N_PIECES)])
    ff = jnp.concatenate([c1 * a + c2 * b for a, b in zip(first, second)], axis=1)
    return _layer_norm_rows(alpha * x_ref[...] + ff, g_ref[...], b_ref[...])


def _inproj_first_kernel(xraw_ref, g_ref, b_ref, *rest, d_model):
    _project(_layer_norm_rows(xraw_ref[...], g_ref[...], b_ref[...]), *rest, d_model=d_model)


def _inproj_next_kernel(y_ref, route_ref, x1_ref, g_ref, b_ref, *rest, alpha, d_model):
    _project(_combine_rows(y_ref, route_ref, x1_ref, g_ref, b_ref, alpha), *rest, d_model=d_model)


def _project(x_f32, w_ref, cos_ref, sin_ref, bd_ref,
             x_out_ref, qk_ref, v_ref, sg_ref, z_ref, gates_ref, *, d_model):
    x_out_ref[...] = x_f32
    x = x_f32.astype(BF16)
    cos = cos_ref[...]
    sin = sin_ref[...]
    qk_w = 2 * N_HEADS * QK_DIM
    v_w = N_HEADS * V_PAD
    off_v = qk_w
    off_g = off_v + v_w
    off_f = off_g + v_w
    off_gr = off_f + F_WIDTH
    cw = 256

    def proj(c0):
        return jnp.dot(x, w_ref[:, c0:c0 + cw], preferred_element_type=F32)

    for c0 in range(0, qk_w, cw):
        acc = proj(c0)
        scale = QK_DIM ** -0.5 if c0 < N_HEADS * QK_DIM else 1.0
        parts = []
        for hh in range(cw // QK_DIM):
            a = acc[:, hh * QK_DIM:(hh + 1) * QK_DIM]
            rot = a * cos + pltpu.roll(a, QK_DIM // 2, axis=1) * sin
            if scale != 1.0:
                rot = rot * scale
            parts.append(rot.astype(BF16))
        qk_ref[:, c0:c0 + cw] = jnp.concatenate(parts, axis=1)
    for c0 in range(0, v_w, cw):
        v_ref[:, c0:c0 + cw] = proj(off_v + c0).astype(BF16)
    for c0 in range(0, v_w, cw):
        a = proj(off_g + c0)
        sg_ref[:, c0:c0 + cw] = (a * jax.nn.sigmoid(a)).astype(BF16)
    f = proj(off_f).astype(BF16)
    z_ref[...] = _pack_halves(jnp.dot(f, bd_ref[...], preferred_element_type=F32))
    for c0 in range(0, 2 * d_model, cw):
        gates_ref[:, c0:c0 + cw] = jax.nn.sigmoid(proj(off_gr + c0)).astype(BF16)


def _layer_block(stacked, layer):
    zeros = (0,) * (stacked.ndim - 1)
    return pl.BlockSpec((None,) + stacked.shape[1:], lambda *_: (layer,) + zeros)


def _inproj(prologue, g, b, pro_layer, w_pad, layer, cos_t, sin_t, bd, tm, seq, alpha):
    first = len(prologue) == 1
    t, d = prologue[-1].shape
    pos_map = lambda i: (i % (seq // tm), 0)
    row = lambda w: pl.BlockSpec((tm, w), lambda i: (i, 0))
    outs = (jax.ShapeDtypeStruct((t, d), F32),
            jax.ShapeDtypeStruct((t, 2 * N_HEADS * QK_DIM), BF16),
            jax.ShapeDtypeStruct((t, N_HEADS * V_PAD), BF16),
            jax.ShapeDtypeStruct((t, N_HEADS * V_PAD), BF16),
            jax.ShapeDtypeStruct((t, F_WIDTH), jnp.uint32),
            jax.ShapeDtypeStruct((t, 2 * d), BF16))
    if first:
        body = functools.partial(_inproj_first_kernel, d_model=d)
        pro_specs = [row(d)]
    else:
        body = functools.partial(_inproj_next_kernel, alpha=alpha, d_model=d)
        pro_specs = [pl.BlockSpec((N_PIECES, 2, tm, PIECE_W), lambda i: (0, 0, i, 0)), row(ROUTE_LANES), row(d)]
    return pl.pallas_call(
        body,
        out_shape=outs,
        grid=(t // tm,),
        in_specs=pro_specs + [_layer_block(g, pro_layer), _layer_block(b, pro_layer),
                              _layer_block(w_pad, layer),
                              pl.BlockSpec((tm, QK_DIM), pos_map),
                              pl.BlockSpec((tm, QK_DIM), pos_map),
                              pl.BlockSpec((F_WIDTH, 2 * F_WIDTH), lambda i: (0, 0))],
        out_specs=tuple(row(o.shape[1]) for o in outs),
        compiler_params=_cparams(("parallel",)),
        name="inproj",
    )(*prologue, g, b, w_pad, cos_t, sin_t, bd)


def _log_sigmoid(x):
    return jnp.minimum(x, 0.0) - jnp.log1p(jnp.exp(-jnp.abs(x)))


def _retention_kernel(dec_ref, q_ref, k_ref, v_ref, sg_ref, gng_ref, gnb_ref, o_ref,
                      r_all, rf_s, rb_s, *, n_chunks, layer):
    h = pl.program_id(1)
    c = CHUNK
    lf = _log_sigmoid(jnp.full((1, c), dec_ref[layer, 0, h], F32))
    lb = _log_sigmoid(jnp.full((1, c), dec_ref[layer, 1, h], F32))
    lf_w = _log_sigmoid(jnp.full((1, V_PAD), dec_ref[layer, 0, h], F32))
    lb_w = _log_sigmoid(jnp.full((1, V_PAD), dec_ref[layer, 1, h], F32))
    row = lax.broadcasted_iota(jnp.int32, (c, c), 0).astype(F32)
    col = lax.broadcasted_iota(jnp.int32, (c, c), 1).astype(F32)
    diff = row - col
    dmat = jnp.where(diff >= 0.0, jnp.exp(lf * jnp.maximum(diff, 0.0)),
                     jnp.exp(lb * jnp.maximum(-diff, 0.0)))
    qd_f = jnp.exp(lf * (row + 1.0))
    kd_f = jnp.exp(lf * (c - 1.0 - row))
    qd_b = jnp.exp(lb * (c - row))
    kd_b = jnp.exp(lb * row)
    cf = jnp.exp(lf_w * float(c))
    cb = jnp.exp(lb_w * float(c))
    gn_g = gng_ref[...]
    gn_b = gnb_ref[...]
    tdot = (((0,), (0,)), ((), ()))

    rf_s[...] = jnp.zeros_like(rf_s)
    rb_s[...] = jnp.zeros_like(rb_s)

    def state_step(t, carry):
        for (ci, kd, cdec, r_s, half) in ((t, kd_f, cf, rf_s, 0), (n_chunks - 1 - t, kd_b, cb, rb_s, 1)):
            r0 = pl.multiple_of(ci * c, c)
            kk = (k_ref[pl.ds(r0, c), :].astype(F32) * kd).astype(BF16)
            kv = lax.dot_general(kk, v_ref[pl.ds(r0, c), :], tdot, preferred_element_type=F32)
            r = r_s[...]
            r_all[ci, half * c:(half + 1) * c, :] = r.astype(BF16)
            r_s[...] = cdec * r + kv
        return carry

    lax.fori_loop(0, n_chunks, state_step, 0, unroll=RET_UNROLL)

    def out_step(ci, carry):
        r0 = pl.multiple_of(ci * c, c)
        q = q_ref[pl.ds(r0, c), :]
        qf = q.astype(F32)
        s = lax.dot_general(q, k_ref[pl.ds(r0, c), :], (((1,), (1,)), ((), ())), preferred_element_type=F32)
        p = (s * dmat).astype(BF16)
        o = jnp.dot(p, v_ref[pl.ds(r0, c), :], preferred_element_type=F32)
        qq = jnp.concatenate([(qf * qd_f).astype(BF16), (qf * qd_b).astype(BF16)], axis=1)
        o = o + jnp.dot(qq, r_all[ci], preferred_element_type=F32)
        mu = jnp.sum(o, axis=-1, keepdims=True) * (1.0 / V_DIM)
        d = o - mu
        var = (jnp.sum(d * d, axis=-1, keepdims=True) - float(V_PAD - V_DIM) * mu * mu) * (1.0 / V_DIM)
        y = d * lax.rsqrt(var + LN_EPS) * gn_g + gn_b
        o_ref[pl.ds(r0, c), :] = (y * sg_ref[pl.ds(r0, c), :].astype(F32)).astype(BF16)
        return carry

    lax.fori_loop(0, n_chunks, out_step, 0, unroll=RET_UNROLL)


def _retention(dec, qk, v, sg, gn_g, gn_b, layer, n_batch, seq):
    n_chunks = seq // CHUNK
    rb = lambda b: b
    return pl.pallas_call(
        functools.partial(_retention_kernel, n_chunks=n_chunks, layer=layer),
        out_shape=jax.ShapeDtypeStruct((n_batch * seq, N_HEADS * V_PAD), BF16),
        grid_spec=pltpu.PrefetchScalarGridSpec(
            num_scalar_prefetch=0,
            grid=(n_batch, N_HEADS),
            in_specs=[pl.BlockSpec(memory_space=pltpu.SMEM),
                      pl.BlockSpec((seq, QK_DIM), lambda b, h: (rb(b), h)),
                      pl.BlockSpec((seq, QK_DIM), lambda b, h: (rb(b), N_HEADS + h)),
                      pl.BlockSpec((seq, V_PAD), lambda b, h: (rb(b), h)),
                      pl.BlockSpec((seq, V_PAD), lambda b, h: (rb(b), h)),
                      pl.BlockSpec((None, 1, V_PAD), lambda b, h: (layer, 0, h)),
                      pl.BlockSpec((None, 1, V_PAD), lambda b, h: (layer, 0, h))],
            out_specs=pl.BlockSpec((seq, V_PAD), lambda b, h: (b, h)),
            scratch_shapes=[pltpu.VMEM((n_chunks, 2 * CHUNK, V_PAD), BF16),
                            pltpu.VMEM((CHUNK, V_PAD), F32),
                            pltpu.VMEM((CHUNK, V_PAD), F32)]),
        compiler_params=_cparams(("parallel", "arbitrary")),
        name="retention",
    )(dec, qk, qk, v, sg, gn_g, gn_b)


def _seqdft_kernel(c_ref, s_ref, z_ref, zrev_ref, zmid_ref, o_ref, acc_ref, *, scale):
    kk = pl.program_id(1)
    b = pl.program_id(2)
    zc, zs = _unpack_halves(z_ref[...])
    rc, rs = _unpack_halves(zrev_ref[...])
    no_partner = (lax.broadcasted_iota(jnp.int32, zc.shape, 0) == 0) & (kk == 0)
    fold_c = (zc + jnp.where(no_partner, 0.0, rc)).astype(BF16)
    fold_s = (zs - jnp.where(no_partner, 0.0, rs)).astype(BF16)
    part = (jnp.dot(c_ref[...], fold_c, preferred_element_type=F32)
            - jnp.dot(s_ref[...], fold_s, preferred_element_type=F32))

    @pl.when(kk == 0)
    def _():
        acc_ref[b] = part

    @pl.when(kk > 0)
    def _():
        acc_ref[b] += part

    @pl.when(kk == pl.num_programs(1) - 1)
    def _():
        tm = acc_ref.shape[1]
        odd = lax.broadcasted_iota(jnp.int32, (tm, F_WIDTH), 0) % 2 == 1
        mid = zmid_ref[b, 0:1, :] * scale
        o_ref[b] = (acc_ref[b] + jnp.where(odd, -mid, mid)).astype(o_ref.dtype)


def _seq_partner_rows(z, n_batch, seq):
    half = seq // 2
    assert (n_batch * half) % SC_GATHER_WINDOW == 0
    n = jnp.arange(half, dtype=jnp.int32)[None, :]
    partner = (jnp.arange(n_batch, dtype=jnp.int32)[:, None] * seq + (seq - n) % seq).reshape(-1)
    return _sc_gather_rows(z, partner).reshape(n_batch, half, F_WIDTH)


def _seq_dft(cmat, smat, z, zrev, n_batch, seq):
    half = seq // 2
    tm = min(1024, seq)
    tk = min(1024, half)
    assert tm % 2 == 0
    z3 = z.reshape(n_batch, seq, F_WIDTH)
    zmid = lax.bitcast_convert_type(z3[:, half, :] << 16, F32)
    zmid = jnp.broadcast_to(zmid[:, None, :], (n_batch, 8, F_WIDTH))
    zblock = pl.BlockSpec((None, tk, F_WIDTH), lambda i, k, b: (b, k, 0))
    out = pl.pallas_call(
        functools.partial(_seqdft_kernel, scale=seq ** -0.5),
        out_shape=jax.ShapeDtypeStruct((n_batch, seq, F_WIDTH), BF16),
        grid=(seq // tm, half // tk, n_batch),
        in_specs=[pl.BlockSpec((tm, tk), lambda i, k, b: (i, k)),
                  pl.BlockSpec((tm, tk), lambda i, k, b: (i, k)),
                  zblock, zblock,
                  pl.BlockSpec((n_batch, 8, F_WIDTH), lambda i, k, b: (0, 0, 0))],
        out_specs=pl.BlockSpec((n_batch, tm, F_WIDTH), lambda i, k, b: (0, i, 0)),
        scratch_shapes=[pltpu.VMEM((n_batch, tm, F_WIDTH), F32)],
        compiler_params=_cparams(("parallel", "arbitrary", "arbitrary")),
        name="seq_dft",
    )(cmat, smat, z3, zrev, zmid)
    return out.reshape(n_batch * seq, F_WIDTH)


def _merge_kernel(r_ref, fo_ref, gates_ref, x_ref, wr_ref, wf_ref, wo_ref, g1_ref, b1_ref,
                  wr2_ref, rb_ref, x1_ref, x1p_ref, route_ref, cnt_ref, *, alpha, d_model):
    for j in range(x_ref.shape[0] // RANK_TILE):
        rows = pl.ds(j * RANK_TILE, RANK_TILE)
        _merge_subtile(rows, j, r_ref, fo_ref, gates_ref, x_ref, wr_ref, wf_ref, wo_ref,
                       g1_ref, b1_ref, wr2_ref, rb_ref, x1_ref, x1p_ref, route_ref, cnt_ref,
                       alpha=alpha, d_model=d_model)


def _merge_subtile(rows, j, r_ref, fo_ref, gates_ref, x_ref, wr_ref, wf_ref, wo_ref,
                   g1_ref, b1_ref, wr2_ref, rb_ref, x1_ref, x1p_ref, route_ref, cnt_ref, *, alpha, d_model):
    a = jnp.dot(r_ref[rows, :], wr_ref[...], preferred_element_type=F32)
    b = jnp.dot(fo_ref[rows, :], wf_ref[...], preferred_element_type=F32)
    merged = (gates_ref[rows, :d_model].astype(F32) * a + gates_ref[rows, d_model:].astype(F32) * b)
    mix = jnp.dot(merged.astype(BF16), wo_ref[...], preferred_element_type=F32)
    x1 = _layer_norm_rows(alpha * x_ref[rows, :] + mix, g1_ref[...], b1_ref[...])
    x1_ref[rows, :] = x1
    packed = _pack_halves(x1)
    for s in range(N_PIECES):
        x1p_ref[s, rows, :] = packed[:, s * PIECE_W:(s + 1) * PIECE_W]

    xh = x1.astype(BF16)
    xl = (x1 - xh.astype(F32)).astype(BF16)
    s2 = (jnp.dot(xh, wr2_ref[...], preferred_element_type=F32)
          + jnp.dot(xl, wr2_ref[...], preferred_element_type=F32))
    logits = s2 + pltpu.roll(s2, ROUTE_LANES // 2, axis=1) + rb_ref[...]
    tm = logits.shape[0]
    lane = lax.broadcasted_iota(jnp.int32, (tm, ROUTE_LANES), 1)
    neg = jnp.float32(-jnp.inf)
    big = jnp.int32(ROUTE_LANES)

    def top1(mask):
        vals = jnp.where(mask, logits, neg)
        m = jnp.max(vals, axis=-1, keepdims=True)
        idx = jnp.min(jnp.where(mask & (vals == m), lane, big), axis=-1, keepdims=True)
        return m, idx

    gmask = lane < N_GROUPS
    gm, gi = top1(gmask)
    gsum = jnp.sum(jnp.where(gmask, jnp.exp(logits - gm), 0.0), axis=-1, keepdims=True)
    g_gate = 1.0 / gsum
    lo = N_GROUPS + gi * EXPERTS_PER_GROUP
    emask = (lane >= lo) & (lane < lo + EXPERTS_PER_GROUP)
    m1, i1 = top1(emask)
    m2, i2 = top1(emask & (lane != i1))
    e2 = jnp.exp(m2 - m1)
    w1 = 1.0 / (1.0 + e2)
    w2 = e2 / (1.0 + e2)
    oh1 = (lane == i1 - N_GROUPS).astype(F32)
    oh2 = (lane == i2 - N_GROUPS).astype(F32)
    earlier = (lax.broadcasted_iota(jnp.int32, (tm, tm), 1)
               < lax.broadcasted_iota(jnp.int32, (tm, tm), 0)).astype(BF16)
    tot1 = jnp.sum(oh1, axis=0, keepdims=True)
    tot2 = jnp.sum(oh2, axis=0, keepdims=True)
    pre1 = jnp.dot(earlier, oh1.astype(BF16), preferred_element_type=F32)
    pre2 = jnp.dot(earlier, oh2.astype(BF16), preferred_element_type=F32) + tot1
    r1 = jnp.sum(pre1 * oh1, axis=-1, keepdims=True)
    r2 = jnp.sum(pre2 * oh2, axis=-1, keepdims=True)
    cnt_ref[j] = tot1 + tot2

    rec = jnp.where(lane == 0, (i1 - N_GROUPS).astype(F32), 0.0)
    rec = jnp.where(lane == 1, (i2 - N_GROUPS).astype(F32), rec)
    rec = jnp.where(lane == 2, g_gate * w1, rec)
    rec = jnp.where(lane == 3, g_gate * w2, rec)
    rec = jnp.where(lane == 4, r1, rec)
    rec = jnp.where(lane == 5, r2, rec)
    route_ref[rows, :] = rec


def _merge(r, fo, gates, x, wr, wf, wo, g1, b1, wr2, rbias, layer, alpha, tm):
    t, d = x.shape
    row = lambda w: pl.BlockSpec((tm, w), lambda i: (i, 0))
    full = lambda a: _layer_block(a, layer)
    return pl.pallas_call(
        functools.partial(_merge_kernel, alpha=alpha, d_model=d),
        out_shape=(jax.ShapeDtypeStruct((t, d), F32),
                   jax.ShapeDtypeStruct((N_PIECES, t, PIECE_W), jnp.uint32),
                   jax.ShapeDtypeStruct((t, ROUTE_LANES), F32),
                   jax.ShapeDtypeStruct((t // RANK_TILE, 1, ROUTE_LANES), F32)),
        grid=(t // tm,),
        in_specs=[row(r.shape[1]), row(fo.shape[1]), row(gates.shape[1]), row(d),
                  full(wr), full(wf), full(wo), full(g1), full(b1), full(wr2), full(rbias)],
        out_specs=(row(d), pl.BlockSpec((N_PIECES, tm, PIECE_W), lambda i: (0, i, 0)), row(ROUTE_LANES),
                   pl.BlockSpec((tm // RANK_TILE, 1, ROUTE_LANES), lambda i: (i, 0, 0))),
        compiler_params=_cparams(("parallel",)),
        name="merge_router",
    )(r, fo, gates, x, wr, wf, wo, g1, b1, wr2, rbias)


def _sc_mesh():
    return plsc.VectorSubcoreMesh(core_axis_name="c", subcore_axis_name="s")


def _sc_gather_rows(table, idx):
    n, w = idx.shape[0], table.shape[1]
    idx = idx.reshape(1, n)

    @functools.partial(pl.kernel, out_type=jax.ShapeDtypeStruct((n, w), table.dtype), mesh=_sc_mesh(),
                       scratch_types=[])
    def gather(tab_hbm, idx_hbm, out_hbm):
        def body(idx_vmem, out_vmem):
            pltpu.sync_copy(tab_hbm.at[idx_vmem.at[0]], out_vmem)

        pltpu.emit_pipeline(
            body,
            grid=(n // SC_GATHER_WINDOW,),
            in_specs=[pl.BlockSpec((1, SC_GATHER_WINDOW), lambda i: (0, i))],
            out_specs=[pl.BlockSpec((SC_GATHER_WINDOW, w), lambda i: (i, 0))],
            core_axis_name=("c", "s"),
            dimension_semantics=(pltpu.PARALLEL,),
        )(idx_hbm, out_hbm)

    return gather(table, idx)


def _sc_scatter_rows(src, idx, n_out):
    assert idx.shape[0] == n_out
    n, w = n_out, src.shape[1]
    idx = idx.reshape(1, n)
    src_windows = src.shape[0] // SC_GATHER_WINDOW

    @functools.partial(pl.kernel, out_type=jax.ShapeDtypeStruct((n, w), src.dtype), mesh=_sc_mesh(),
                       scratch_types=[])
    def scatter(src_hbm, idx_hbm, out_hbm):
        def body(src_vmem, idx_vmem):
            pltpu.sync_copy(src_vmem, out_hbm.at[idx_vmem.at[0]])

        pltpu.emit_pipeline(
            body,
            grid=(n // SC_GATHER_WINDOW,),
            in_specs=[pl.BlockSpec((SC_GATHER_WINDOW, w), lambda i: (i % src_windows, 0)),
                      pl.BlockSpec((1, SC_GATHER_WINDOW), lambda i: (0, i))],
            out_specs=[],
            core_axis_name=("c", "s"),
            dimension_semantics=(pltpu.PARALLEL,),
        )(src_hbm, idx_hbm)

    return scatter(src, idx)


def _expert_kernel(te_ref, nv_ref, run_ref, xs_ref, wg_hbm, wu_hbm, wd_hbm, o_ref,
                   wg_f, wu_f, wd_f, wg_s, wu_s, wd_s, sems, *, slab0):
    i = pl.program_id(0)

    def weight_copies(expert, slot):
        return [pltpu.make_async_copy(hbm.at[slab0 + expert], buf.at[slot], sems.at[slot, k])
                for k, (hbm, buf) in enumerate(((wg_hbm, wg_f), (wu_hbm, wu_f), (wd_hbm, wd_f)))]

    @pl.when(i == 0)
    def _():
        for c in weight_copies(te_ref[0], 0):
            c.start()

    @pl.when(run_ref[0, i] == 1)
    def _():
        slot = run_ref[1, i]
        for c in weight_copies(te_ref[i], slot):
            c.wait()

        @pl.when(run_ref[2, i] >= 0)
        def _():
            for c in weight_copies(run_ref[2, i], 1 - slot):
                c.start()

        wg_s[...] = wg_f[slot].astype(BF16)
        wu_s[...] = wu_f[slot].astype(BF16)
        wd_s[...] = wd_f[slot].astype(BF16)

    @pl.when(i < nv_ref[0])
    def _():
        chunks = [c.astype(BF16) for c in _unpack_pieces([xs_ref[s] for s in range(N_PIECES)])]
        cw = chunks[0].shape[1]

        def up(w_s):
            acc = jnp.dot(chunks[0], w_s[:cw, :], preferred_element_type=F32)
            for j in range(1, len(chunks)):
                acc = acc + jnp.dot(chunks[j], w_s[j * cw:(j + 1) * cw, :], preferred_element_type=F32)
            return acc

        g = up(wg_s)
        hmid = (g * jax.nn.sigmoid(g) * up(wu_s)).astype(BF16)
        _store_pieces(o_ref, _pack_halves(jnp.dot(hmid, wd_s[...], preferred_element_type=F32)))

    @pl.when(i >= nv_ref[0])
    def _():
        o_ref[...] = jnp.zeros_like(o_ref)


def _expert_mlp(tile_expert, n_valid, xs, wg, wu, wd, layer, tm):
    p = xs.shape[1]
    n_tiles = p // tm
    d, de = wg.shape[1], wg.shape[2]
    tiles = jnp.arange(n_tiles, dtype=jnp.int32)
    prev = jnp.concatenate([tile_expert[:1], tile_expert[:-1]])
    is_start = (tiles == 0) | (tile_expert != prev)
    slot = (jnp.cumsum(is_start.astype(jnp.int32)) - 1) % 2
    start_at = jnp.where(is_start, tiles, n_tiles)
    next_start = jnp.concatenate([lax.cummin(start_at[::-1])[::-1][1:], jnp.full((1,), n_tiles, jnp.int32)])
    next_expert = jnp.where(next_start < n_tiles, tile_expert[jnp.minimum(next_start, n_tiles - 1)], -1)
    runs = jnp.stack([is_start.astype(jnp.int32), slot, next_expert]).astype(jnp.int32)

    xmap = lambda i, te, nv, run: (0, jnp.minimum(i, nv[0] - 1), 0)
    return pl.pallas_call(
        functools.partial(_expert_kernel, slab0=layer * N_EXPERTS),
        out_shape=jax.ShapeDtypeStruct(xs.shape, jnp.uint32),
        grid_spec=pltpu.PrefetchScalarGridSpec(
            num_scalar_prefetch=3,
            grid=(n_tiles,),
            in_specs=[pl.BlockSpec((N_PIECES, tm, PIECE_W), xmap),
                      pl.BlockSpec(memory_space=pl.ANY),
                      pl.BlockSpec(memory_space=pl.ANY),
                      pl.BlockSpec(memory_space=pl.ANY)],
            out_specs=pl.BlockSpec((N_PIECES, tm, PIECE_W), lambda i, te, nv, run: (0, i, 0)),
            scratch_shapes=[pltpu.VMEM((2, d, de), F32), pltpu.VMEM((2, d, de), F32), pltpu.VMEM((2, de, d), F32),
                            pltpu.VMEM((d, de), BF16), pltpu.VMEM((d, de), BF16), pltpu.VMEM((de, d), BF16),
                            pltpu.SemaphoreType.DMA((2, 3))]),
        compiler_params=_cparams(("arbitrary",)),
        name="expert_mlp",
    )(tile_expert, n_valid, runs, xs, wg, wu, wd)


def _combine_kernel(y_ref, route_ref, x_ref, g_ref, b_ref, o_ref, *, alpha):
    o_ref[...] = _combine_rows(y_ref, route_ref, x_ref, g_ref, b_ref, alpha)


def _combine(yg, route, x1, g2, b2, layer, alpha, tm):
    n_rows, d = x1.shape
    row = lambda w: pl.BlockSpec((tm, w), lambda i: (i, 0))
    return pl.pallas_call(
        functools.partial(_combine_kernel, alpha=alpha),
        out_shape=jax.ShapeDtypeStruct((n_rows, d), F32),
        grid=(n_rows // tm,),
        in_specs=[pl.BlockSpec((N_PIECES, 2, tm, PIECE_W), lambda i: (0, 0, i, 0)),
                  row(ROUTE_LANES), row(d), _layer_block(g2, layer), _layer_block(b2, layer)],
        out_specs=pl.BlockSpec((tm, d), lambda i: (i, 0)),
        compiler_params=_cparams(("parallel",)),
        name="combine_ln2",
    )(yg, route, x1, g2, b2)


def _rope_tables(seq):
    half = QK_DIM // 2
    inv_freq = 1.0 / (ROPE_BASE ** (jnp.arange(half, dtype=F32) / half))
    ang = jnp.arange(seq, dtype=F32)[:, None] * inv_freq[None, :]
    cos, sin = jnp.cos(ang), jnp.sin(ang)
    return jnp.concatenate([cos, cos], axis=1), jnp.concatenate([-sin, sin], axis=1)


def _channel_dft_matrix():
    n = np.arange(FG_DIM)
    ang = 2.0 * np.pi * np.outer(n, n) / FG_DIM
    eye = np.eye(N_FGROUPS)
    bd = np.concatenate([np.kron(eye, np.cos(ang)), np.kron(eye, np.sin(ang))], axis=1) / math.sqrt(FG_DIM)
    return jnp.asarray(bd, dtype=BF16)


def _dft_matrix_kernel(ch_ref, sh_ref, cl_ref, sl_ref, c_ref, s_ref):
    cl = cl_ref[...]
    sl = sl_ref[...]
    for blk in range(ch_ref.shape[1]):
        ch = ch_ref[:, blk:blk + 1]
        sh = sh_ref[:, blk:blk + 1]
        cols = slice(blk * LANE, (blk + 1) * LANE)
        c_ref[:, cols] = (ch * cl - sh * sl).astype(BF16)
        s_ref[:, cols] = (sh * cl + ch * sl).astype(BF16)


def _seq_dft_matrices(seq):
    half = seq // 2
    assert half % LANE == 0
    k = jnp.arange(seq, dtype=jnp.int32)[:, None]
    hi = jnp.arange(half // LANE, dtype=jnp.int32)[None, :] * LANE
    lo = jnp.arange(LANE, dtype=jnp.int32)[None, :]
    w = 2.0 * math.pi / seq
    a_hi = ((k * hi) % seq).astype(F32) * w
    a_lo = ((k * lo) % seq).astype(F32) * w
    scale = seq ** -0.5
    tm = min(256, seq)
    nb = half // LANE
    small = lambda width: pl.BlockSpec((tm, width), lambda i: (i, 0))
    return pl.pallas_call(
        _dft_matrix_kernel,
        out_shape=(jax.ShapeDtypeStruct((seq, half), BF16), jax.ShapeDtypeStruct((seq, half), BF16)),
        grid=(seq // tm,),
        in_specs=[small(nb), small(nb), small(LANE), small(LANE)],
        out_specs=(small(half), small(half)),
        compiler_params=_cparams(("parallel",)),
        name="dft_matrices",
    )(jnp.cos(a_hi) * scale, jnp.sin(a_hi) * scale, jnp.cos(a_lo), jnp.sin(a_lo))


def _pad_heads(w, axis):
    shp = list(w.shape)
    shp[axis:axis + 1] = [N_HEADS, V_DIM]
    w = w.reshape(shp)
    pad = [(0, 0)] * w.ndim
    pad[axis + 1] = (0, V_PAD - V_DIM)
    w = jnp.pad(w, pad)
    shp[axis:axis + 2] = [N_HEADS * V_PAD]
    return w.reshape(shp)


def _pad_in_weight(w_in):
    qk_w = 2 * N_HEADS * QK_DIM
    vw = N_HEADS * V_DIM
    qk = w_in[..., :qk_w]
    v = _pad_heads(w_in[..., qk_w:qk_w + vw], 2)
    g = _pad_heads(w_in[..., qk_w + vw:qk_w + 2 * vw], 2)
    rest = w_in[..., qk_w + 2 * vw:]
    return jnp.concatenate([qk, v, g, rest], axis=2).astype(BF16)


def _routing_plan(route, tile_counts, tm_tok, n_tiles, moe_tile):
    t = route.shape[0]
    experts = jnp.arange(N_EXPERTS, dtype=jnp.int32)
    cnt = tile_counts[:, 0, :N_EXPERTS].astype(jnp.int32)
    before = jnp.cumsum(cnt, axis=0) - cnt
    counts = jnp.sum(cnt, axis=0)
    tiles_per = (counts + moe_tile - 1) // moe_tile
    tile_end = jnp.cumsum(tiles_per)
    start_row = (tile_end - tiles_per) * moe_tile
    n_valid = tile_end[-1]

    base = (start_row[None, :] + before).astype(F32)
    base_tok = jnp.broadcast_to(base[:, None, :], (t // tm_tok, tm_tok, N_EXPERTS)).reshape(t, N_EXPERTS)
    e = route[:, 0:2]
    rank = route[:, 4:6]
    sel = e[:, :, None] == experts.astype(F32)[None, None, :]
    pos = (jnp.sum(jnp.where(sel, base_tok[:, None, :], 0.0), axis=-1) + rank).astype(jnp.int32)

    n_pad = n_tiles * moe_tile - 2 * t
    pads = tiles_per * moe_tile - counts
    pad_end = jnp.cumsum(pads)
    p = jnp.arange(n_pad, dtype=jnp.int32)
    pe = jnp.minimum(jnp.sum((p[:, None] >= pad_end[None, :]).astype(jnp.int32), axis=1), N_EXPERTS - 1)
    in_expert = start_row[pe] + counts[pe] + (p - (pad_end[pe] - pads[pe]))
    pad_pos = jnp.where(p < pad_end[-1], in_expert, n_valid * moe_tile + (p - pad_end[-1]))

    tile_ids = jnp.arange(n_tiles, dtype=jnp.int32)
    tile_expert = jnp.minimum(jnp.sum((tile_ids[:, None] >= tile_end[None, :]).astype(jnp.int32), axis=1),
                              N_EXPERTS - 1)
    last_e = tile_expert[jnp.maximum(n_valid - 1, 0)]
    tile_expert = jnp.where(tile_ids < n_valid, tile_expert, last_e).astype(jnp.int32)
    return pos, pad_pos.astype(jnp.int32), tile_expert, n_valid.reshape(1).astype(jnp.int32)


def _largest_tile(cands, *dims):
    for c in cands:
        if all(d % c == 0 for d in dims):
            return c
    raise ValueError(f"no tile in {cands} divides {dims}")


def kernel(x_prompt, x_sample, ln_in_g, ln_in_b, w_in, ret_decay_fwd, ret_decay_bwd, ret_gn_g, ret_gn_b, w_ret_up, w_four_up, w_out, ln1_g, ln1_b, w_route_group, b_route_group, w_route_expert, b_route_expert, w_expert_gate, w_expert_up, w_expert_down, ln2_g, ln2_b):
    bp, sp, d = x_prompt.shape
    bs, ss, _ = x_sample.shape
    depth = w_in.shape[0]
    alpha = float((2 * depth) ** 0.25)

    cos_t, sin_t = _rope_tables(max(sp, ss))
    bd = _channel_dft_matrix()
    dft = {s: _seq_dft_matrices(s) for s in sorted({sp, ss})}
    wg_all = w_expert_gate.reshape((depth * N_EXPERTS,) + w_expert_gate.shape[2:])
    wu_all = w_expert_up.reshape((depth * N_EXPERTS,) + w_expert_up.shape[2:])
    wd_all = w_expert_down.reshape((depth * N_EXPERTS,) + w_expert_down.shape[2:])

    batches = []
    for xin in (x_prompt, x_sample):
        nb, seq, _ = xin.shape
        t = nb * seq
        assert seq % CHUNK == 0 and t % SC_GATHER_WINDOW == 0
        moe_tile = MOE_TILE if 2 * t >= 4 * MOE_TILE * N_EXPERTS else MOE_TILE // 2
        assert (2 * t) % moe_tile == 0
        batches.append(dict(nb=nb, seq=seq, t=t, moe_tile=moe_tile, n_tiles=(2 * t) // moe_tile + N_EXPERTS,
                            tm_in=_largest_tile((512, 256, 128), seq),
                            tm_tok=_largest_tile((TOKEN_TILE, 128), t),
                            tm_merge=_largest_tile((MERGE_TILE, RANK_TILE), t),
                            prologue=(xin.reshape(t, d),)))

    vec3 = lambda a: a.reshape(a.shape[0], 1, a.shape[-1]).astype(F32)
    w_pad = _pad_in_weight(w_in)
    dec = jnp.stack([ret_decay_fwd, ret_decay_bwd], axis=1).astype(F32)
    gn_g = vec3(_pad_heads(ret_gn_g, 1))
    gn_b = vec3(_pad_heads(ret_gn_b, 1))
    wr = _pad_heads(w_ret_up, 1).astype(BF16)
    wf = w_four_up.astype(BF16)
    wo = w_out.astype(BF16)
    w_rt = jnp.concatenate([w_route_group, w_route_expert], axis=2)
    w_rt = jnp.pad(w_rt, ((0, 0), (0, 0), (0, ROUTE_LANES // 2 - w_rt.shape[2])))
    wrh = w_rt.astype(BF16)
    wr2 = jnp.concatenate([wrh, (w_rt - wrh.astype(F32)).astype(BF16)], axis=2)
    rbias = jnp.concatenate([b_route_group, b_route_expert], axis=1)
    rbias = vec3(jnp.pad(rbias, ((0, 0), (0, ROUTE_LANES - rbias.shape[1]))))
    ln_in = (vec3(ln_in_g[None]), vec3(ln_in_b[None]))
    ln1 = (vec3(ln1_g), vec3(ln1_b))
    ln2 = (vec3(ln2_g), vec3(ln2_b))

    for l in range(depth):
        pro_g, pro_b, pro_layer = (*ln_in, 0) if l == 0 else (*ln2, l - 1)

        for bt in batches:
            nb, seq, t, moe_tile, n_tiles = bt["nb"], bt["seq"], bt["t"], bt["moe_tile"], bt["n_tiles"]
            x, qk, v, sg, z, gates = _inproj(bt["prologue"], pro_g, pro_b, pro_layer, w_pad, l, cos_t, sin_t, bd,
                                             bt["tm_in"], seq, alpha)
            zrev = _seq_partner_rows(z, nb, seq)
            r = _retention(dec, qk, v, sg, gn_g, gn_b, l, nb, seq)
            fo = _seq_dft(dft[seq][0], dft[seq][1], z, zrev, nb, seq)
            x1, x1p, route, tile_counts = _merge(r, fo, gates, x, wr, wf, wo, *ln1, wr2, rbias, l,
                                                 alpha, bt["tm_merge"])

            pos, pad_pos, tile_expert, n_valid = _routing_plan(route, tile_counts, RANK_TILE, n_tiles, moe_tile)
            n_sorted = n_tiles * moe_tile
            piece_off = jnp.arange(N_PIECES, dtype=jnp.int32)[:, None] * n_sorted
            dst = jnp.concatenate([(piece_off + seg[None, :]).reshape(-1)
                                   for seg in (pos[:, 0], pos[:, 1], pad_pos)])
            xs = _sc_scatter_rows(x1p.reshape(N_PIECES * t, PIECE_W), dst, N_PIECES * n_sorted)
            ys = _expert_mlp(tile_expert, n_valid, xs.reshape(N_PIECES, n_sorted, PIECE_W),
                             wg_all, wu_all, wd_all, l, moe_tile)
            src = (piece_off[:, :, None] + pos.T[None, :, :]).reshape(-1)
            yg = _sc_gather_rows(ys.reshape(N_PIECES * n_sorted, PIECE_W), src).reshape(N_PIECES, 2, t, PIECE_W)
            bt["prologue"] = (yg, route, x1)

    outs = [_combine(*bt["prologue"], *ln2, depth - 1, alpha, bt["tm_tok"]) for bt in batches]
    return outs[0].reshape(bp, sp, d), outs[1].reshape(bs, ss, d)
```

```python
import functools
import math

import jax
import jax.numpy as jnp
import numpy as np
from jax import lax
from jax.experimental import pallas as pl
from jax.experimental.pallas import tpu as pltpu
from jax.experimental.pallas import tpu_sc as plsc

N_HEADS = 4
QK_DIM = 128
V_DIM = 192
V_PAD = 256
CHUNK = 128
ROPE_BASE = 10000.0
N_FGROUPS = 4
FG_DIM = 64
F_WIDTH = N_FGROUPS * FG_DIM
N_GROUPS = 4
EXPERTS_PER_GROUP = 8
N_EXPERTS = N_GROUPS * EXPERTS_PER_GROUP
LN_EPS = 1e-5
ROUTE_LANES = 128

LANE = 128
VMEM_LIMIT_BYTES = 56 * 1024 * 1024
MOE_TILE = 512
GATHER_TILE = 256
SC_GATHER_WINDOW = 128
N_PIECES = 2
PIECE_W = 256
TOKEN_TILE = 256
MERGE_TILE = 512
RANK_TILE = 256
RET_UNROLL = 32

F32 = jnp.float32
BF16 = jnp.bfloat16


def _cparams(sem):
    return pltpu.CompilerParams(dimension_semantics=sem, vmem_limit_bytes=VMEM_LIMIT_BYTES)


def _layer_norm_rows(y, g, b):
    mu = jnp.mean(y, axis=-1, keepdims=True)
    d = y - mu
    var = jnp.mean(d * d, axis=-1, keepdims=True)
    return d * lax.rsqrt(var + LN_EPS) * g + b


def _pack_halves(y):
    n = y.shape[1] // 2
    lo = pltpu.bitcast(y[:, :n].astype(BF16).astype(F32), jnp.uint32)
    hi = pltpu.bitcast(y[:, n:].astype(BF16).astype(F32), jnp.uint32)
    return (hi & jnp.uint32(0xFFFF0000)) | (lo >> 16)


def _unpack_halves(w):
    lo = pltpu.bitcast(w << 16, F32)
    hi = pltpu.bitcast(w & jnp.uint32(0xFFFF0000), F32)
    return lo, hi


def _store_pieces(ref, packed):
    for s in range(N_PIECES):
        ref[s] = packed[:, s * PIECE_W:(s + 1) * PIECE_W]


def _unpack_pieces(pieces):
    halves = [_unpack_halves(p) for p in pieces]
    return [h[0] for h in halves] + [h[1] for h in halves]


def _combine_rows(y_ref, route_ref, x_ref, g_ref, b_ref, alpha):
    c1 = route_ref[:, 2:3]
    c2 = route_ref[:, 3:4]
    first = _unpack_pieces([y_ref[s, 0] for s in range(N_PIECES)])
    second = _unpack_pieces([y_ref[s, 1] for s in range(N_PIECES)])
    ff = jnp.concatenate([c1 * a + c2 * b for a, b in zip(first, second)], axis=1)
    return _layer_norm_rows(alpha * x_ref[...] + ff, g_ref[...], b_ref[...])


def _inproj_first_kernel(xraw_ref, g_ref, b_ref, *rest, d_model):
    _project(_layer_norm_rows(xraw_ref[...], g_ref[...], b_ref[...]), *rest, d_model=d_model)


def _inproj_next_kernel(y_ref, route_ref, x1_ref, g_ref, b_ref, *rest, alpha, d_model):
    _project(_combine_rows(y_ref, route_ref, x1_ref, g_ref, b_ref, alpha), *rest, d_model=d_model)


def _project(x_f32, w_ref, cos_ref, sin_ref, bd_ref,
             x_out_ref, qk_ref, v_ref, sg_ref, z_ref, gates_ref, *, d_model):
    x_out_ref[...] = x_f32
    x = x_f32.astype(BF16)
    cos = cos_ref[...]
    sin = sin_ref[...]
    qk_w = 2 * N_HEADS * QK_DIM
    v_w = N_HEADS * V_PAD
    off_v = qk_w
    off_g = off_v + v_w
    off_f = off_g + v_w
    off_gr = off_f + F_WIDTH
    cw = 256

    def proj(c0):
        return jnp.dot(x, w_ref[:, c0:c0 + cw], preferred_element_type=F32)

    for c0 in range(0, qk_w, cw):
        acc = proj(c0)
        scale = QK_DIM ** -0.5 if c0 < N_HEADS * QK_DIM else 1.0
        parts = []
        for hh in range(cw // QK_DIM):
            a = acc[:, hh * QK_DIM:(hh + 1) * QK_DIM]
            rot = a * cos + pltpu.roll(a, QK_DIM // 2, axis=1) * sin
            if scale != 1.0:
                rot = rot * scale
            parts.append(rot.astype(BF16))
        qk_ref[:, c0:c0 + cw] = jnp.concatenate(parts, axis=1)
    for c0 in range(0, v_w, cw):
        v_ref[:, c0:c0 + cw] = proj(off_v + c0).astype(BF16)
    for c0 in range(0, v_w, cw):
        a = proj(off_g + c0)
        sg_ref[:, c0:c0 + cw] = (a * jax.nn.sigmoid(a)).astype(BF16)
    f = proj(off_f).astype(BF16)
    z_ref[...] = _pack_halves(jnp.dot(f, bd_ref[...], preferred_element_type=F32))
    for c0 in range(0, 2 * d_model, cw):
        gates_ref[:, c0:c0 + cw] = jax.nn.sigmoid(proj(off_gr + c0)).astype(BF16)


def _layer_block(stacked, layer):
    zeros = (0,) * (stacked.ndim - 1)
    return pl.BlockSpec((None,) + stacked.shape[1:], lambda *_: (layer,) + zeros)


def _inproj(prologue, g, b, pro_layer, w_pad, layer, cos_t, sin_t, bd, tm, seq, alpha):
    first = len(prologue) == 1
    t, d = prologue[-1].shape
    pos_map = lambda i: (i % (seq // tm), 0)
    row = lambda w: pl.BlockSpec((tm, w), lambda i: (i, 0))
    outs = (jax.ShapeDtypeStruct((t, d), F32),
            jax.ShapeDtypeStruct((t, 2 * N_HEADS * QK_DIM), BF16),
            jax.ShapeDtypeStruct((t, N_HEADS * V_PAD), BF16),
            jax.ShapeDtypeStruct((t, N_HEADS * V_PAD), BF16),
            jax.ShapeDtypeStruct((t, F_WIDTH), jnp.uint32),
            jax.ShapeDtypeStruct((t, 2 * d), BF16))
    if first:
        body = functools.partial(_inproj_first_kernel, d_model=d)
        pro_specs = [row(d)]
    else:
        body = functools.partial(_inproj_next_kernel, alpha=alpha, d_model=d)
        pro_specs = [pl.BlockSpec((N_PIECES, 2, tm, PIECE_W), lambda i: (0, 0, i, 0)), row(ROUTE_LANES), row(d)]
    return pl.pallas_call(
        body,
        out_shape=outs,
        grid=(t // tm,),
        in_specs=pro_specs + [_layer_block(g, pro_layer), _layer_block(b, pro_layer),
                              _layer_block(w_pad, layer),
                              pl.BlockSpec((tm, QK_DIM), pos_map),
                              pl.BlockSpec((tm, QK_DIM), pos_map),
                              pl.BlockSpec((F_WIDTH, 2 * F_WIDTH), lambda i: (0, 0))],
        out_specs=tuple(row(o.shape[1]) for o in outs),
        compiler_params=_cparams(("parallel",)),
        name="inproj",
    )(*prologue, g, b, w_pad, cos_t, sin_t, bd)


def _log_sigmoid(x):
    return jnp.minimum(x, 0.0) - jnp.log1p(jnp.exp(-jnp.abs(x)))


def _retention_kernel(dec_ref, q_ref, k_ref, v_ref, sg_ref, gng_ref, gnb_ref, o_ref,
                      r_all, rf_s, rb_s, *, n_chunks, layer):
    h = pl.program_id(1)
    c = CHUNK
    lf = _log_sigmoid(jnp.full((1, c), dec_ref[layer, 0, h], F32))
    lb = _log_sigmoid(jnp.full((1, c), dec_ref[layer, 1, h], F32))
    lf_w = _log_sigmoid(jnp.full((1, V_PAD), dec_ref[layer, 0, h], F32))
    lb_w = _log_sigmoid(jnp.full((1, V_PAD), dec_ref[layer, 1, h], F32))
    row = lax.broadcasted_iota(jnp.int32, (c, c), 0).astype(F32)
    col = lax.broadcasted_iota(jnp.int32, (c, c), 1).astype(F32)
    diff = row - col
    dmat = jnp.where(diff >= 0.0, jnp.exp(lf * jnp.maximum(diff, 0.0)),
                     jnp.exp(lb * jnp.maximum(-diff, 0.0)))
    qd_f = jnp.exp(lf * (row + 1.0))
    kd_f = jnp.exp(lf * (c - 1.0 - row))
    qd_b = jnp.exp(lb * (c - row))
    kd_b = jnp.exp(lb * row)
    cf = jnp.exp(lf_w * float(c))
    cb = jnp.exp(lb_w * float(c))
    gn_g = gng_ref[...]
    gn_b = gnb_ref[...]
    tdot = (((0,), (0,)), ((), ()))

    rf_s[...] = jnp.zeros_like(rf_s)
    rb_s[...] = jnp.zeros_like(rb_s)

    def state_step(t, carry):
        for (ci, kd, cdec, r_s, half) in ((t, kd_f, cf, rf_s, 0), (n_chunks - 1 - t, kd_b, cb, rb_s, 1)):
            r0 = pl.multiple_of(ci * c, c)
            kk = (k_ref[pl.ds(r0, c), :].astype(F32) * kd).astype(BF16)
            kv = lax.dot_general(kk, v_ref[pl.ds(r0, c), :], tdot, preferred_element_type=F32)
            r = r_s[...]
            r_all[ci, half * c:(half + 1) * c, :] = r.astype(BF16)
            r_s[...] = cdec * r + kv
        return carry

    lax.fori_loop(0, n_chunks, state_step, 0, unroll=RET_UNROLL)

    def out_step(ci, carry):
        r0 = pl.multiple_of(ci * c, c)
        q = q_ref[pl.ds(r0, c), :]
        qf = q.astype(F32)
        s = lax.dot_general(q, k_ref[pl.ds(r0, c), :], (((1,), (1,)), ((), ())), preferred_element_type=F32)
        p = (s * dmat).astype(BF16)
        o = jnp.dot(p, v_ref[pl.ds(r0, c), :], preferred_element_type=F32)
        qq = jnp.concatenate([(qf * qd_f).astype(BF16), (qf * qd_b).astype(BF16)], axis=1)
        o = o + jnp.dot(qq, r_all[ci], preferred_element_type=F32)
        mu = jnp.sum(o, axis=-1, keepdims=True) * (1.0 / V_DIM)
        d = o - mu
        var = (jnp.sum(d * d, axis=-1, keepdims=True) - float(V_PAD - V_DIM) * mu * mu) * (1.0 / V_DIM)
        y = d * lax.rsqrt(var + LN_EPS) * gn_g + gn_b
        o_ref[pl.ds(r0, c), :] = (y * sg_ref[pl.ds(r0, c), :].astype(F32)).astype(BF16)
        return carry

    lax.fori_loop(0, n_chunks, out_step, 0, unroll=RET_UNROLL)


def _retention(dec, qk, v, sg, gn_g, gn_b, layer, n_batch, seq):
    n_chunks = seq // CHUNK
    rb = lambda b: b
    return pl.pallas_call(
        functools.partial(_retention_kernel, n_chunks=n_chunks, layer=layer),
        out_shape=jax.ShapeDtypeStruct((n_batch * seq, N_HEADS * V_PAD), BF16),
        grid_spec=pltpu.PrefetchScalarGridSpec(
            num_scalar_prefetch=0,
            grid=(n_batch, N_HEADS),
            in_specs=[pl.BlockSpec(memory_space=pltpu.SMEM),
                      pl.BlockSpec((seq, QK_DIM), lambda b, h: (rb(b), h)),
                      pl.BlockSpec((seq, QK_DIM), lambda b, h: (rb(b), N_HEADS + h)),
                      pl.BlockSpec((seq, V_PAD), lambda b, h: (rb(b), h)),
                      pl.BlockSpec((seq, V_PAD), lambda b, h: (rb(b), h)),
                      pl.BlockSpec((None, 1, V_PAD), lambda b, h: (layer, 0, h)),
                      pl.BlockSpec((None, 1, V_PAD), lambda b, h: (layer, 0, h))],
            out_specs=pl.BlockSpec((seq, V_PAD), lambda b, h: (b, h)),
            scratch_shapes=[pltpu.VMEM((n_chunks, 2 * CHUNK, V_PAD), BF16),
                            pltpu.VMEM((CHUNK, V_PAD), F32),
                            pltpu.VMEM((CHUNK, V_PAD), F32)]),
        compiler_params=_cparams(("parallel", "arbitrary")),
        name="retention",
    )(dec, qk, qk, v, sg, gn_g, gn_b)


def _seqdft_kernel(c_ref, s_ref, z_ref, zrev_ref, zmid_ref, o_ref, acc_ref, *, scale):
    kk = pl.program_id(1)
    b = pl.program_id(2)
    zc, zs = _unpack_halves(z_ref[...])
    rc, rs = _unpack_halves(zrev_ref[...])
    no_partner = (lax.broadcasted_iota(jnp.int32, zc.shape, 0) == 0) & (kk == 0)
    fold_c = (zc + jnp.where(no_partner, 0.0, rc)).astype(BF16)
    fold_s = (zs - jnp.where(no_partner, 0.0, rs)).astype(BF16)
    part = (jnp.dot(c_ref[...], fold_c, preferred_element_type=F32)
            - jnp.dot(s_ref[...], fold_s, preferred_element_type=F32))

    @pl.when(kk == 0)
    def _():
        acc_ref[b] = part

    @pl.when(kk > 0)
    def _():
        acc_ref[b] += part

    @pl.when(kk == pl.num_programs(1) - 1)
    def _():
        tm = acc_ref.shape[1]
        odd = lax.broadcasted_iota(jnp.int32, (tm, F_WIDTH), 0) % 2 == 1
        mid = zmid_ref[b, 0:1, :] * scale
        o_ref[b] = (acc_ref[b] + jnp.where(odd, -mid, mid)).astype(o_ref.dtype)


def _seq_partner_rows(z, n_batch, seq):
    half = seq // 2
    assert (n_batch * half) % SC_GATHER_WINDOW == 0
    n = jnp.arange(half, dtype=jnp.int32)[None, :]
    partner = (jnp.arange(n_batch, dtype=jnp.int32)[:, None] * seq + (seq - n) % seq).reshape(-1)
    return _sc_gather_rows(z, partner).reshape(n_batch, half, F_WIDTH)


def _seq_dft(cmat, smat, z, zrev, n_batch, seq):
    half = seq // 2
    tm = min(1024, seq)
    tk = min(1024, half)
    assert tm % 2 == 0
    z3 = z.reshape(n_batch, seq, F_WIDTH)
    zmid = lax.bitcast_convert_type(z3[:, half, :] << 16, F32)
    zmid = jnp.broadcast_to(zmid[:, None, :], (n_batch, 8, F_WIDTH))
    zblock = pl.BlockSpec((None, tk, F_WIDTH), lambda i, k, b: (b, k, 0))
    out = pl.pallas_call(
        functools.partial(_seqdft_kernel, scale=seq ** -0.5),
        out_shape=jax.ShapeDtypeStruct((n_batch, seq, F_WIDTH), BF16),
        grid=(seq // tm, half // tk, n_batch),
        in_specs=[pl.BlockSpec((tm, tk), lambda i, k, b: (i, k)),
                  pl.BlockSpec((tm, tk), lambda i, k, b: (i, k)),
                  zblock, zblock,
                  pl.BlockSpec((n_batch, 8, F_WIDTH), lambda i, k, b: (0, 0, 0))],
        out_specs=pl.BlockSpec((n_batch, tm, F_WIDTH), lambda i, k, b: (0, i, 0)),
        scratch_shapes=[pltpu.VMEM((n_batch, tm, F_WIDTH), F32)],
        compiler_params=_cparams(("parallel", "arbitrary", "arbitrary")),
        name="seq_dft",
    )(cmat, smat, z3, zrev, zmid)
    return out.reshape(n_batch * seq, F_WIDTH)


def _merge_kernel(r_ref, fo_ref, gates_ref, x_ref, wr_ref, wf_ref, wo_ref, g1_ref, b1_ref,
                  wr2_ref, rb_ref, x1_ref, x1p_ref, route_ref, cnt_ref, *, alpha, d_model):
    for j in range(x_ref.shape[0] // RANK_TILE):
        rows = pl.ds(j * RANK_TILE, RANK_TILE)
        _merge_subtile(rows, j, r_ref, fo_ref, gates_ref, x_ref, wr_ref, wf_ref, wo_ref,
                       g1_ref, b1_ref, wr2_ref, rb_ref, x1_ref, x1p_ref, route_ref, cnt_ref,
                       alpha=alpha, d_model=d_model)


def _merge_subtile(rows, j, r_ref, fo_ref, gates_ref, x_ref, wr_ref, wf_ref, wo_ref,
                   g1_ref, b1_ref, wr2_ref, rb_ref, x1_ref, x1p_ref, route_ref, cnt_ref, *, alpha, d_model):
    a = jnp.dot(r_ref[rows, :], wr_ref[...], preferred_element_type=F32)
    b = jnp.dot(fo_ref[rows, :], wf_ref[...], preferred_element_type=F32)
    merged = (gates_ref[rows, :d_model].astype(F32) * a + gates_ref[rows, d_model:].astype(F32) * b)
    mix = jnp.dot(merged.astype(BF16), wo_ref[...], preferred_element_type=F32)
    x1 = _layer_norm_rows(alpha * x_ref[rows, :] + mix, g1_ref[...], b1_ref[...])
    x1_ref[rows, :] = x1
    packed = _pack_halves(x1)
    for s in range(N_PIECES):
        x1p_ref[s, rows, :] = packed[:, s * PIECE_W:(s + 1) * PIECE_W]

    xh = x1.astype(BF16)
    xl = (x1 - xh.astype(F32)).astype(BF16)
    s2 = (jnp.dot(xh, wr2_ref[...], preferred_element_type=F32)
          + jnp.dot(xl, wr2_ref[...], preferred_element_type=F32))
    logits = s2 + pltpu.roll(s2, ROUTE_LANES // 2, axis=1) + rb_ref[...]
    tm = logits.shape[0]
    lane = lax.broadcasted_iota(jnp.int32, (tm, ROUTE_LANES), 1)
    neg = jnp.float32(-jnp.inf)
    big = jnp.int32(ROUTE_LANES)

    def top1(mask):
        vals = jnp.where(mask, logits, neg)
        m = jnp.max(vals, axis=-1, keepdims=True)
        idx = jnp.min(jnp.where(mask & (vals == m), lane, big), axis=-1, keepdims=True)
        return m, idx

    gmask = lane < N_GROUPS
    gm, gi = top1(gmask)
    gsum = jnp.sum(jnp.where(gmask, jnp.exp(logits - gm), 0.0), axis=-1, keepdims=True)
    g_gate = 1.0 / gsum
    lo = N_GROUPS + gi * EXPERTS_PER_GROUP
    emask = (lane >= lo) & (lane < lo + EXPERTS_PER_GROUP)
    m1, i1 = top1(emask)
    m2, i2 = top1(emask & (lane != i1))
    e2 = jnp.exp(m2 - m1)
    w1 = 1.0 / (1.0 + e2)
    w2 = e2 / (1.0 + e2)
    oh1 = (lane == i1 - N_GROUPS).astype(F32)
    oh2 = (lane == i2 - N_GROUPS).astype(F32)
    earlier = (lax.broadcasted_iota(jnp.int32, (tm, tm), 1)
               < lax.broadcasted_iota(jnp.int32, (tm, tm), 0)).astype(BF16)
    tot1 = jnp.sum(oh1, axis=0, keepdims=True)
    tot2 = jnp.sum(oh2, axis=0, keepdims=True)
    pre1 = jnp.dot(earlier, oh1.astype(BF16), preferred_element_type=F32)
    pre2 = jnp.dot(earlier, oh2.astype(BF16), preferred_element_type=F32) + tot1
    r1 = jnp.sum(pre1 * oh1, axis=-1, keepdims=True)
    r2 = jnp.sum(pre2 * oh2, axis=-1, keepdims=True)
    cnt_ref[j] = tot1 + tot2

    rec = jnp.where(lane == 0, (i1 - N_GROUPS).astype(F32), 0.0)
    rec = jnp.where(lane == 1, (i2 - N_GROUPS).astype(F32), rec)
    rec = jnp.where(lane == 2, g_gate * w1, rec)
    rec = jnp.where(lane == 3, g_gate * w2, rec)
    rec = jnp.where(lane == 4, r1, rec)
    rec = jnp.where(lane == 5, r2, rec)
    route_ref[rows, :] = rec


def _merge(r, fo, gates, x, wr, wf, wo, g1, b1, wr2, rbias, layer, alpha, tm):
    t, d = x.shape
    row = lambda w: pl.BlockSpec((tm, w), lambda i: (i, 0))
    full = lambda a: _layer_block(a, layer)
    return pl.pallas_call(
        functools.partial(_merge_kernel, alpha=alpha, d_model=d),
        out_shape=(jax.ShapeDtypeStruct((t, d), F32),
                   jax.ShapeDtypeStruct((N_PIECES, t, PIECE_W), jnp.uint32),
                   jax.ShapeDtypeStruct((t, ROUTE_LANES), F32),
                   jax.ShapeDtypeStruct((t // RANK_TILE, 1, ROUTE_LANES), F32)),
        grid=(t // tm,),
        in_specs=[row(r.shape[1]), row(fo.shape[1]), row(gates.shape[1]), row(d),
                  full(wr), full(wf), full(wo), full(g1), full(b1), full(wr2), full(rbias)],
        out_specs=(row(d), pl.BlockSpec((N_PIECES, tm, PIECE_W), lambda i: (0, i, 0)), row(ROUTE_LANES),
                   pl.BlockSpec((tm // RANK_TILE, 1, ROUTE_LANES), lambda i: (i, 0, 0))),
        compiler_params=_cparams(("parallel",)),
        name="merge_router",
    )(r, fo, gates, x, wr, wf, wo, g1, b1, wr2, rbias)


def _sc_mesh():
    return plsc.VectorSubcoreMesh(core_axis_name="c", subcore_axis_name="s")


def _sc_gather_rows(table, idx):
    n, w = idx.shape[0], table.shape[1]
    idx = idx.reshape(1, n)

    @functools.partial(pl.kernel, out_type=jax.ShapeDtypeStruct((n, w), table.dtype), mesh=_sc_mesh(),
                       scratch_types=[])
    def gather(tab_hbm, idx_hbm, out_hbm):
        def body(idx_vmem, out_vmem):
            pltpu.sync_copy(tab_hbm.at[idx_vmem.at[0]], out_vmem)

        pltpu.emit_pipeline(
            body,
            grid=(n // SC_GATHER_WINDOW,),
            in_specs=[pl.BlockSpec((1, SC_GATHER_WINDOW), lambda i: (0, i))],
            out_specs=[pl.BlockSpec((SC_GATHER_WINDOW, w), lambda i: (i, 0))],
            core_axis_name=("c", "s"),
            dimension_semantics=(pltpu.PARALLEL,),
        )(idx_hbm, out_hbm)

    return gather(table, idx)


def _sc_scatter_rows(src, idx, n_out):
    assert idx.shape[0] == n_out
    n, w = n_out, src.shape[1]
    idx = idx.reshape(1, n)
    src_windows = src.shape[0] // SC_GATHER_WINDOW

    @functools.partial(pl.kernel, out_type=jax.ShapeDtypeStruct((n, w), src.dtype), mesh=_sc_mesh(),
                       scratch_types=[])
    def scatter(src_hbm, idx_hbm, out_hbm):
        def body(src_vmem, idx_vmem):
            pltpu.sync_copy(src_vmem, out_hbm.at[idx_vmem.at[0]])

        pltpu.emit_pipeline(
            body,
            grid=(n // SC_GATHER_WINDOW,),
            in_specs=[pl.BlockSpec((SC_GATHER_WINDOW, w), lambda i: (i % src_windows, 0)),
                      pl.BlockSpec((1, SC_GATHER_WINDOW), lambda i: (0, i))],
            out_specs=[],
            core_axis_name=("c", "s"),
            dimension_semantics=(pltpu.PARALLEL,),
        )(src_hbm, idx_hbm)

    return scatter(src, idx)


def _expert_kernel(te_ref, nv_ref, run_ref, xs_ref, wg_hbm, wu_hbm, wd_hbm, o_ref,
                   wg_f, wu_f, wd_f, wg_s, wu_s, wd_s, sems, *, slab0):
    i = pl.program_id(0)

    def weight_copies(expert, slot):
        return [pltpu.make_async_copy(hbm.at[slab0 + expert], buf.at[slot], sems.at[slot, k])
                for k, (hbm, buf) in enumerate(((wg_hbm, wg_f), (wu_hbm, wu_f), (wd_hbm, wd_f)))]

    @pl.when(i == 0)
    def _():
        for c in weight_copies(te_ref[0], 0):
            c.start()

    @pl.when(run_ref[0, i] == 1)
    def _():
        slot = run_ref[1, i]
        for c in weight_copies(te_ref[i], slot):
            c.wait()

        @pl.when(run_ref[2, i] >= 0)
        def _():
            for c in weight_copies(run_ref[2, i], 1 - slot):
                c.start()

        wg_s[...] = wg_f[slot].astype(BF16)
        wu_s[...] = wu_f[slot].astype(BF16)
        wd_s[...] = wd_f[slot].astype(BF16)

    @pl.when(i < nv_ref[0])
    def _():
        chunks = [c.astype(BF16) for c in _unpack_pieces([xs_ref[s] for s in range(N_PIECES)])]
        cw = chunks[0].shape[1]

        def up(w_s):
            acc = jnp.dot(chunks[0], w_s[:cw, :], preferred_element_type=F32)
            for j in range(1, len(chunks)):
                acc = acc + jnp.dot(chunks[j], w_s[j * cw:(j + 1) * cw, :], preferred_element_type=F32)
            return acc

        g = up(wg_s)
        hmid = (g * jax.nn.sigmoid(g) * up(wu_s)).astype(BF16)
        _store_pieces(o_ref, _pack_halves(jnp.dot(hmid, wd_s[...], preferred_element_type=F32)))

    @pl.when(i >= nv_ref[0])
    def _():
        o_ref[...] = jnp.zeros_like(o_ref)


def _expert_mlp(tile_expert, n_valid, xs, wg, wu, wd, layer, tm):
    p = xs.shape[1]
    n_tiles = p // tm
    d, de = wg.shape[1], wg.shape[2]
    tiles = jnp.arange(n_tiles, dtype=jnp.int32)
    prev = jnp.concatenate([tile_expert[:1], tile_expert[:-1]])
    is_start = (tiles == 0) | (tile_expert != prev)
    slot = (jnp.cumsum(is_start.astype(jnp.int32)) - 1) % 2
    start_at = jnp.where(is_start, tiles, n_tiles)
    next_start = jnp.concatenate([lax.cummin(start_at[::-1])[::-1][1:], jnp.full((1,), n_tiles, jnp.int32)])
    next_expert = jnp.where(next_start < n_tiles, tile_expert[jnp.minimum(next_start, n_tiles - 1)], -1)
    runs = jnp.stack([is_start.astype(jnp.int32), slot, next_expert]).astype(jnp.int32)

    xmap = lambda i, te, nv, run: (0, jnp.minimum(i, nv[0] - 1), 0)
    return pl.pallas_call(
        functools.partial(_expert_kernel, slab0=layer * N_EXPERTS),
        out_shape=jax.ShapeDtypeStruct(xs.shape, jnp.uint32),
        grid_spec=pltpu.PrefetchScalarGridSpec(
            num_scalar_prefetch=3,
            grid=(n_tiles,),
            in_specs=[pl.BlockSpec((N_PIECES, tm, PIECE_W), xmap),
                      pl.BlockSpec(memory_space=pl.ANY),
                      pl.BlockSpec(memory_space=pl.ANY),
                      pl.BlockSpec(memory_space=pl.ANY)],
            out_specs=pl.BlockSpec((N_PIECES, tm, PIECE_W), lambda i, te, nv, run: (0, i, 0)),
            scratch_shapes=[pltpu.VMEM((2, d, de), F32), pltpu.VMEM((2, d, de), F32), pltpu.VMEM((2, de, d), F32),
                            pltpu.VMEM((d, de), BF16), pltpu.VMEM((d, de), BF16), pltpu.VMEM((de, d), BF16),
                            pltpu.SemaphoreType.DMA((2, 3))]),
        compiler_params=_cparams(("arbitrary",)),
        name="expert_mlp",
    )(tile_expert, n_valid, runs, xs, wg, wu, wd)


def _combine_kernel(y_ref, route_ref, x_ref, g_ref, b_ref, o_ref, *, alpha):
    o_ref[...] = _combine_rows(y_ref, route_ref, x_ref, g_ref, b_ref, alpha)


def _combine(yg, route, x1, g2, b2, layer, alpha, tm):
    n_rows, d = x1.shape
    row = lambda w: pl.BlockSpec((tm, w), lambda i: (i, 0))
    return pl.pallas_call(
        functools.partial(_combine_kernel, alpha=alpha),
        out_shape=jax.ShapeDtypeStruct((n_rows, d), F32),
        grid=(n_rows // tm,),
        in_specs=[pl.BlockSpec((N_PIECES, 2, tm, PIECE_W), lambda i: (0, 0, i, 0)),
                  row(ROUTE_LANES), row(d), _layer_block(g2, layer), _layer_block(b2, layer)],
        out_specs=pl.BlockSpec((tm, d), lambda i: (i, 0)),
        compiler_params=_cparams(("parallel",)),
        name="combine_ln2",
    )(yg, route, x1, g2, b2)


def _rope_tables(seq):
    half = QK_DIM // 2
    inv_freq = 1.0 / (ROPE_BASE ** (jnp.arange(half, dtype=F32) / half))
    ang = jnp.arange(seq, dtype=F32)[:, None] * inv_freq[None, :]
    cos, sin = jnp.cos(ang), jnp.sin(ang)
    return jnp.concatenate([cos, cos], axis=1), jnp.concatenate([-sin, sin], axis=1)


def _channel_dft_matrix():
    n = np.arange(FG_DIM)
    ang = 2.0 * np.pi * np.outer(n, n) / FG_DIM
    eye = np.eye(N_FGROUPS)
    bd = np.concatenate([np.kron(eye, np.cos(ang)), np.kron(eye, np.sin(ang))], axis=1) / math.sqrt(FG_DIM)
    return jnp.asarray(bd, dtype=BF16)


def _dft_matrix_kernel(ch_ref, sh_ref, cl_ref, sl_ref, c_ref, s_ref):
    cl = cl_ref[...]
    sl = sl_ref[...]
    for blk in range(ch_ref.shape[1]):
        ch = ch_ref[:, blk:blk + 1]
        sh = sh_ref[:, blk:blk + 1]
        cols = slice(blk * LANE, (blk + 1) * LANE)
        c_ref[:, cols] = (ch * cl - sh * sl).astype(BF16)
        s_ref[:, cols] = (sh * cl + ch * sl).astype(BF16)


def _seq_dft_matrices(seq):
    half = seq // 2
    assert half % LANE == 0
    k = jnp.arange(seq, dtype=jnp.int32)[:, None]
    hi = jnp.arange(half // LANE, dtype=jnp.int32)[None, :] * LANE
    lo = jnp.arange(LANE, dtype=jnp.int32)[None, :]
    w = 2.0 * math.pi / seq
    a_hi = ((k * hi) % seq).astype(F32) * w
    a_lo = ((k * lo) % seq).astype(F32) * w
    scale = seq ** -0.5
    tm = min(256, seq)
    nb = half // LANE
    small = lambda width: pl.BlockSpec((tm, width), lambda i: (i, 0))
    return pl.pallas_call(
        _dft_matrix_kernel,
        out_shape=(jax.ShapeDtypeStruct((seq, half), BF16), jax.ShapeDtypeStruct((seq, half), BF16)),
        grid=(seq // tm,),
        in_specs=[small(nb), small(nb), small(LANE), small(LANE)],
        out_specs=(small(half), small(half)),
        compiler_params=_cparams(("parallel",)),
        name="dft_matrices",
    )(jnp.cos(a_hi) * scale, jnp.sin(a_hi) * scale, jnp.cos(a_lo), jnp.sin(a_lo))


def _pad_heads(w, axis):
    shp = list(w.shape)
    shp[axis:axis + 1] = [N_HEADS, V_DIM]
    w = w.reshape(shp)
    pad = [(0, 0)] * w.ndim
    pad[axis + 1] = (0, V_PAD - V_DIM)
    w = jnp.pad(w, pad)
    shp[axis:axis + 2] = [N_HEADS * V_PAD]
    return w.reshape(shp)


def _pad_in_weight(w_in):
    qk_w = 2 * N_HEADS * QK_DIM
    vw = N_HEADS * V_DIM
    qk = w_in[..., :qk_w]
    v = _pad_heads(w_in[..., qk_w:qk_w + vw], 2)
    g = _pad_heads(w_in[..., qk_w + vw:qk_w + 2 * vw], 2)
    rest = w_in[..., qk_w + 2 * vw:]
    return jnp.concatenate([qk, v, g, rest], axis=2).astype(BF16)


def _routing_plan(route, tile_counts, tm_tok, n_tiles, moe_tile):
    t = route.shape[0]
    experts = jnp.arange(N_EXPERTS, dtype=jnp.int32)
    cnt = tile_counts[:, 0, :N_EXPERTS].astype(jnp.int32)
    before = jnp.cumsum(cnt, axis=0) - cnt
    counts = jnp.sum(cnt, axis=0)
    tiles_per = (counts + moe_tile - 1) // moe_tile
    tile_end = jnp.cumsum(tiles_per)
    start_row = (tile_end - tiles_per) * moe_tile
    n_valid = tile_end[-1]

    base = (start_row[None, :] + before).astype(F32)
    base_tok = jnp.broadcast_to(base[:, None, :], (t // tm_tok, tm_tok, N_EXPERTS)).reshape(t, N_EXPERTS)
    e = route[:, 0:2]
    rank = route[:, 4:6]
    sel = e[:, :, None] == experts.astype(F32)[None, None, :]
    pos = (jnp.sum(jnp.where(sel, base_tok[:, None, :], 0.0), axis=-1) + rank).astype(jnp.int32)

    n_pad = n_tiles * moe_tile - 2 * t
    pads = tiles_per * moe_tile - counts
    pad_end = jnp.cumsum(pads)
    p = jnp.arange(n_pad, dtype=jnp.int32)
    pe = jnp.minimum(jnp.sum((p[:, None] >= pad_end[None, :]).astype(jnp.int32), axis=1), N_EXPERTS - 1)
    in_expert = start_row[pe] + counts[pe] + (p - (pad_end[pe] - pads[pe]))
    pad_pos = jnp.where(p < pad_end[-1], in_expert, n_valid * moe_tile + (p - pad_end[-1]))

    tile_ids = jnp.arange(n_tiles, dtype=jnp.int32)
    tile_expert = jnp.minimum(jnp.sum((tile_ids[:, None] >= tile_end[None, :]).astype(jnp.int32), axis=1),
                              N_EXPERTS - 1)
    last_e = tile_expert[jnp.maximum(n_valid - 1, 0)]
    tile_expert = jnp.where(tile_ids < n_valid, tile_expert, last_e).astype(jnp.int32)
    return pos, pad_pos.astype(jnp.int32), tile_expert, n_valid.reshape(1).astype(jnp.int32)


def _largest_tile(cands, *dims):
    for c in cands:
        if all(d % c == 0 for d in dims):
            return c
    raise ValueError(f"no tile in {cands} divides {dims}")


def kernel(x_prompt, x_sample, ln_in_g, ln_in_b, w_in, ret_decay_fwd, ret_decay_bwd, ret_gn_g, ret_gn_b, w_ret_up, w_four_up, w_out, ln1_g, ln1_b, w_route_group, b_route_group, w_route_expert, b_route_expert, w_expert_gate, w_expert_up, w_expert_down, ln2_g, ln2_b):
    bp, sp, d = x_prompt.shape
    bs, ss, _ = x_sample.shape
    depth = w_in.shape[0]
    alpha = float((2 * depth) ** 0.25)

    cos_t, sin_t = _rope_tables(max(sp, ss))
    bd = _channel_dft_matrix()
    dft = {s: _seq_dft_matrices(s) for s in sorted({sp, ss})}
    wg_all = w_expert_gate.reshape((depth * N_EXPERTS,) + w_expert_gate.shape[2:])
    wu_all = w_expert_up.reshape((depth * N_EXPERTS,) + w_expert_up.shape[2:])
    wd_all = w_expert_down.reshape((depth * N_EXPERTS,) + w_expert_down.shape[2:])

    batches = []
    for xin in (x_prompt, x_sample):
        nb, seq, _ = xin.shape
        t = nb * seq
        assert seq % CHUNK == 0 and t % SC_GATHER_WINDOW == 0
        moe_tile = MOE_TILE if 2 * t >= 4 * MOE_TILE * N_EXPERTS else MOE_TILE // 2
        assert (2 * t) % moe_tile == 0
        batches.append(dict(nb=nb, seq=seq, t=t, moe_tile=moe_tile, n_tiles=(2 * t) // moe_tile + N_EXPERTS,
                            tm_in=_largest_tile((512, 256, 128), seq),
                            tm_tok=_largest_tile((TOKEN_TILE, 128), t),
                            tm_merge=_largest_tile((MERGE_TILE, RANK_TILE), t),
                            prologue=(xin.reshape(t, d),)))

    vec3 = lambda a: a.reshape(a.shape[0], 1, a.shape[-1]).astype(F32)
    w_pad = _pad_in_weight(w_in)
    dec = jnp.stack([ret_decay_fwd, ret_decay_bwd], axis=1).astype(F32)
    gn_g = vec3(_pad_heads(ret_gn_g, 1))
    gn_b = vec3(_pad_heads(ret_gn_b, 1))
    wr = _pad_heads(w_ret_up, 1).astype(BF16)
    wf = w_four_up.astype(BF16)
    wo = w_out.astype(BF16)
    w_rt = jnp.concatenate([w_route_group, w_route_expert], axis=2)
    w_rt = jnp.pad(w_rt, ((0, 0), (0, 0), (0, ROUTE_LANES // 2 - w_rt.shape[2])))
    wrh = w_rt.astype(BF16)
    wr2 = jnp.concatenate([wrh, (w_rt - wrh.astype(F32)).astype(BF16)], axis=2)
    rbias = jnp.concatenate([b_route_group, b_route_expert], axis=1)
    rbias = vec3(jnp.pad(rbias, ((0, 0), (0, ROUTE_LANES - rbias.shape[1]))))
    ln_in = (vec3(ln_in_g[None]), vec3(ln_in_b[None]))
    ln1 = (vec3(ln1_g), vec3(ln1_b))
    ln2 = (vec3(ln2_g), vec3(ln2_b))

    for l in range(depth):
        pro_g, pro_b, pro_layer = (*ln_in, 0) if l == 0 else (*ln2, l - 1)

        for bt in batches:
            nb, seq, t, moe_tile, n_tiles = bt["nb"], bt["seq"], bt["t"], bt["moe_tile"], bt["n_tiles"]
            x, qk, v, sg, z, gates = _inproj(bt["prologue"], pro_g, pro_b, pro_layer, w_pad, l, cos_t, sin_t, bd,
                                             bt["tm_in"], seq, alpha)
            zrev = _seq_partner_rows(z, nb, seq)
            r = _retention(dec, qk, v, sg, gn_g, gn_b, l, nb, seq)
            fo = _seq_dft(dft[seq][0], dft[seq][1], z, zrev, nb, seq)
            x1, x1p, route, tile_counts = _merge(r, fo, gates, x, wr, wf, wo, *ln1, wr2, rbias, l,
                                                 alpha, bt["tm_merge"])

            pos, pad_pos, tile_expert, n_valid = _routing_plan(route, tile_counts, RANK_TILE, n_tiles, moe_tile)
            n_sorted = n_tiles * moe_tile
            piece_off = jnp.arange(N_PIECES, dtype=jnp.int32)[:, None] * n_sorted
            dst = jnp.concatenate([(piece_off + seg[None, :]).reshape(-1)
                                   for seg in (pos[:, 0], pos[:, 1], pad_pos)])
            xs = _sc_scatter_rows(x1p.reshape(N_PIECES * t, PIECE_W), dst, N_PIECES * n_sorted)
            ys = _expert_mlp(tile_expert, n_valid, xs.reshape(N_PIECES, n_sorted, PIECE_W),
                             wg_all, wu_all, wd_all, l, moe_tile)
            src = (piece_off[:, :, None] + pos.T[None, :, :]).reshape(-1)
            yg = _sc_gather_rows(ys.reshape(N_PIECES * n_sorted, PIECE_W), src).reshape(N_PIECES, 2, t, PIECE_W)
            bt["prologue"] = (yg, route, x1)

    outs = [_combine(*bt["prologue"], *ln2, depth - 1, alpha, bt["tm_tok"]) for bt in batches]
    return outs[0].reshape(bp, sp, d), outs[1].reshape(bs, ss, d)
```

```python
import functools
import math

import jax
import jax.numpy as jnp
import numpy as np
from jax import lax
from jax.experimental import pallas as pl
from jax.experimental.pallas import tpu as pltpu
from jax.experimental.pallas import tpu_sc as plsc

N_HEADS = 4
QK_DIM = 128
V_DIM = 192
V_PAD = 256
CHUNK = 128
ROPE_BASE = 10000.0
N_FGROUPS = 4
FG_DIM = 64
F_WIDTH = N_FGROUPS * FG_DIM
N_GROUPS = 4
EXPERTS_PER_GROUP = 8
N_EXPERTS = N_GROUPS * EXPERTS_PER_GROUP
LN_EPS = 1e-5
ROUTE_LANES = 128

LANE = 128
VMEM_LIMIT_BYTES = 56 * 1024 * 1024
MOE_TILE = 512
GATHER_TILE = 256
SC_GATHER_WINDOW = 128
N_PIECES = 2
PIECE_W = 256
TOKEN_TILE = 256
MERGE_TILE = 512
RANK_TILE = 256
RET_UNROLL = 32

F32 = jnp.float32
BF16 = jnp.bfloat16


def _cparams(sem):
    return pltpu.CompilerParams(dimension_semantics=sem, vmem_limit_bytes=VMEM_LIMIT_BYTES)


def _layer_norm_rows(y, g, b):
    mu = jnp.mean(y, axis=-1, keepdims=True)
    d = y - mu
    var = jnp.mean(d * d, axis=-1, keepdims=True)
    return d * lax.rsqrt(var + LN_EPS) * g + b


def _pack_halves(y):
    n = y.shape[1] // 2
    lo = pltpu.bitcast(y[:, :n].astype(BF16).astype(F32), jnp.uint32)
    hi = pltpu.bitcast(y[:, n:].astype(BF16).astype(F32), jnp.uint32)
    return (hi & jnp.uint32(0xFFFF0000)) | (lo >> 16)


def _unpack_halves(w):
    lo = pltpu.bitcast(w << 16, F32)
    hi = pltpu.bitcast(w & jnp.uint32(0xFFFF0000), F32)
    return lo, hi


def _store_pieces(ref, packed):
    for s in range(N_PIECES):
        ref[s] = packed[:, s * PIECE_W:(s + 1) * PIECE_W]


def _unpack_pieces(pieces):
    halves = [_unpack_halves(p) for p in pieces]
    return [h[0] for h in halves] + [h[1] for h in halves]


def _combine_rows(y_ref, route_ref, x_ref, g_ref, b_ref, alpha):
    c1 = route_ref[:, 2:3]
    c2 = route_ref[:, 3:4]
    first = _unpack_pieces([y_ref[s, 0] for s in range(N_PIECES)])
    second = _unpack_pieces([y_ref[s, 1] for s in range(N_PIECES)])
    ff = jnp.concatenate([c1 * a + c2 * b for a, b in zip(first, second)], axis=1)
    return _layer_norm_rows(alpha * x_ref[...] + ff, g_ref[...], b_ref[...])


def _inproj_first_kernel(xraw_ref, g_ref, b_ref, *rest, d_model):
    _project(_layer_norm_rows(xraw_ref[...], g_ref[...], b_ref[...]), *rest, d_model=d_model)


def _inproj_next_kernel(y_ref, route_ref, x1_ref, g_ref, b_ref, *rest, alpha, d_model):
    _project(_combine_rows(y_ref, route_ref, x1_ref, g_ref, b_ref, alpha), *rest, d_model=d_model)


def _project(x_f32, w_ref, cos_ref, sin_ref, bd_ref,
             x_out_ref, qk_ref, v_ref, sg_ref, z_ref, gates_ref, *, d_model):
    x_out_ref[...] = x_f32
    x = x_f32.astype(BF16)
    cos = cos_ref[...]
    sin = sin_ref[...]
    qk_w = 2 * N_HEADS * QK_DIM
    v_w = N_HEADS * V_PAD
    off_v = qk_w
    off_g = off_v + v_w
    off_f = off_g + v_w
    off_gr = off_f + F_WIDTH
    cw = 256

    def proj(c0):
        return jnp.dot(x, w_ref[:, c0:c0 + cw], preferred_element_type=F32)

    for c0 in range(0, qk_w, cw):
        acc = proj(c0)
        scale = QK_DIM ** -0.5 if c0 < N_HEADS * QK_DIM else 1.0
        parts = []
        for hh in range(cw // QK_DIM):
            a = acc[:, hh * QK_DIM:(hh + 1) * QK_DIM]
            rot = a * cos + pltpu.roll(a, QK_DIM // 2, axis=1) * sin
            if scale != 1.0:
                rot = rot * scale
            parts.append(rot.astype(BF16))
        qk_ref[:, c0:c0 + cw] = jnp.concatenate(parts, axis=1)
    for c0 in range(0, v_w, cw):
        v_ref[:, c0:c0 + cw] = proj(off_v + c0).astype(BF16)
    for c0 in range(0, v_w, cw):
        a = proj(off_g + c0)
        sg_ref[:, c0:c0 + cw] = (a * jax.nn.sigmoid(a)).astype(BF16)
    f = proj(off_f).astype(BF16)
    z_ref[...] = _pack_halves(jnp.dot(f, bd_ref[...], preferred_element_type=F32))
    for c0 in range(0, 2 * d_model, cw):
        gates_ref[:, c0:c0 + cw] = jax.nn.sigmoid(proj(off_gr + c0)).astype(BF16)


def _layer_block(stacked, layer):
    zeros = (0,) * (stacked.ndim - 1)
    return pl.BlockSpec((None,) + stacked.shape[1:], lambda *_: (layer,) + zeros)


def _inproj(prologue, g, b, pro_layer, w_pad, layer, cos_t, sin_t, bd, tm, seq, alpha):
    first = len(prologue) == 1
    t, d = prologue[-1].shape
    pos_map = lambda i: (i % (seq // tm), 0)
    row = lambda w: pl.BlockSpec((tm, w), lambda i: (i, 0))
    outs = (jax.ShapeDtypeStruct((t, d), F32),
            jax.ShapeDtypeStruct((t, 2 * N_HEADS * QK_DIM), BF16),
            jax.ShapeDtypeStruct((t, N_HEADS * V_PAD), BF16),
            jax.ShapeDtypeStruct((t, N_HEADS * V_PAD), BF16),
            jax.ShapeDtypeStruct((t, F_WIDTH), jnp.uint32),
            jax.ShapeDtypeStruct((t, 2 * d), BF16))
    if first:
        body = functools.partial(_inproj_first_kernel, d_model=d)
        pro_specs = [row(d)]
    else:
        body = functools.partial(_inproj_next_kernel, alpha=alpha, d_model=d)
        pro_specs = [pl.BlockSpec((N_PIECES, 2, tm, PIECE_W), lambda i: (0, 0, i, 0)), row(ROUTE_LANES), row(d)]
    return pl.pallas_call(
        body,
        out_shape=outs,
        grid=(t // tm,),
        in_specs=pro_specs + [_layer_block(g, pro_layer), _layer_block(b, pro_layer),
                              _layer_block(w_pad, layer),
                              pl.BlockSpec((tm, QK_DIM), pos_map),
                              pl.BlockSpec((tm, QK_DIM), pos_map),
                              pl.BlockSpec((F_WIDTH, 2 * F_WIDTH), lambda i: (0, 0))],
        out_specs=tuple(row(o.shape[1]) for o in outs),
        compiler_params=_cparams(("parallel",)),
        name="inproj",
    )(*prologue, g, b, w_pad, cos_t, sin_t, bd)


def _log_sigmoid(x):
    return jnp.minimum(x, 0.0) - jnp.log1p(jnp.exp(-jnp.abs(x)))


def _retention_kernel(dec_ref, q_ref, k_ref, v_ref, sg_ref, gng_ref, gnb_ref, o_ref,
                      r_all, rf_s, rb_s, *, n_chunks, layer):
    h = pl.program_id(1)
    c = CHUNK
    lf = _log_sigmoid(jnp.full((1, c), dec_ref[layer, 0, h], F32))
    lb = _log_sigmoid(jnp.full((1, c), dec_ref[layer, 1, h], F32))
    lf_w = _log_sigmoid(jnp.full((1, V_PAD), dec_ref[layer, 0, h], F32))
    lb_w = _log_sigmoid(jnp.full((1, V_PAD), dec_ref[layer, 1, h], F32))
    row = lax.broadcasted_iota(jnp.int32, (c, c), 0).astype(F32)
    col = lax.broadcasted_iota(jnp.int32, (c, c), 1).astype(F32)
    diff = row - col
    dmat = jnp.where(diff >= 0.0, jnp.exp(lf * jnp.maximum(diff, 0.0)),
                     jnp.exp(lb * jnp.maximum(-diff, 0.0)))
    qd_f = jnp.exp(lf * (row + 1.0))
    kd_f = jnp.exp(lf * (c - 1.0 - row))
    qd_b = jnp.exp(lb * (c - row))
    kd_b = jnp.exp(lb * row)
    cf = jnp.exp(lf_w * float(c))
    cb = jnp.exp(lb_w * float(c))
    gn_g = gng_ref[...]
    gn_b = gnb_ref[...]
    tdot = (((0,), (0,)), ((), ()))

    rf_s[...] = jnp.zeros_like(rf_s)
    rb_s[...] = jnp.zeros_like(rb_s)

    def state_step(t, carry):
        for (ci, kd, cdec, r_s, half) in ((t, kd_f, cf, rf_s, 0), (n_chunks - 1 - t, kd_b, cb, rb_s, 1)):
            r0 = pl.multiple_of(ci * c, c)
            kk = (k_ref[pl.ds(r0, c), :].astype(F32) * kd).astype(BF16)
            kv = lax.dot_general(kk, v_ref[pl.ds(r0, c), :], tdot, preferred_element_type=F32)
            r = r_s[...]
            r_all[ci, half * c:(half + 1) * c, :] = r.astype(BF16)
            r_s[...] = cdec * r + kv
        return carry

    lax.fori_loop(0, n_chunks, state_step, 0, unroll=RET_UNROLL)

    def out_step(ci, carry):
        r0 = pl.multiple_of(ci * c, c)
        q = q_ref[pl.ds(r0, c), :]
        qf = q.astype(F32)
        s = lax.dot_general(q, k_ref[pl.ds(r0, c), :], (((1,), (1,)), ((), ())), preferred_element_type=F32)
        p = (s * dmat).astype(BF16)
        o = jnp.dot(p, v_ref[pl.ds(r0, c), :], preferred_element_type=F32)
        qq = jnp.concatenate([(qf * qd_f).astype(BF16), (qf * qd_b).astype(BF16)], axis=1)
        o = o + jnp.dot(qq, r_all[ci], preferred_element_type=F32)
        mu = jnp.sum(o, axis=-1, keepdims=True) * (1.0 / V_DIM)
        d = o - mu
        var = (jnp.sum(d * d, axis=-1, keepdims=True) - float(V_PAD - V_DIM) * mu * mu) * (1.0 / V_DIM)
        y = d * lax.rsqrt(var + LN_EPS) * gn_g + gn_b
        o_ref[pl.ds(r0, c), :] = (y * sg_ref[pl.ds(r0, c), :].astype(F32)).astype(BF16)
        return carry

    lax.fori_loop(0, n_chunks, out_step, 0, unroll=RET_UNROLL)


def _retention(dec, qk, v, sg, gn_g, gn_b, layer, n_batch, seq):
    n_chunks = seq // CHUNK
    rb = lambda b: b
    return pl.pallas_call(
        functools.partial(_retention_kernel, n_chunks=n_chunks, layer=layer),
        out_shape=jax.ShapeDtypeStruct((n_batch * seq, N_HEADS * V_PAD), BF16),
        grid_spec=pltpu.PrefetchScalarGridSpec(
            num_scalar_prefetch=0,
            grid=(n_batch, N_HEADS),
            in_specs=[pl.BlockSpec(memory_space=pltpu.SMEM),
                      pl.BlockSpec((seq, QK_DIM), lambda b, h: (rb(b), h)),
                      pl.BlockSpec((seq, QK_DIM), lambda b, h: (rb(b), N_HEADS + h)),
                      pl.BlockSpec((seq, V_PAD), lambda b, h: (rb(b), h)),
                      pl.BlockSpec((seq, V_PAD), lambda b, h: (rb(b), h)),
                      pl.BlockSpec((None, 1, V_PAD), lambda b, h: (layer, 0, h)),
                      pl.BlockSpec((None, 1, V_PAD), lambda b, h: (layer, 0, h))],
            out_specs=pl.BlockSpec((seq, V_PAD), lambda b, h: (b, h)),
            scratch_shapes=[pltpu.VMEM((n_chunks, 2 * CHUNK, V_PAD), BF16),
                            pltpu.VMEM((CHUNK, V_PAD), F32),
                            pltpu.VMEM((CHUNK, V_PAD), F32)]),
        compiler_params=_cparams(("parallel", "arbitrary")),
        name="retention",
    )(dec, qk, qk, v, sg, gn_g, gn_b)


def _seqdft_kernel(c_ref, s_ref, z_ref, zrev_ref, zmid_ref, o_ref, acc_ref, *, scale):
    kk = pl.program_id(1)
    b = pl.program_id(2)
    zc, zs = _unpack_halves(z_ref[...])
    rc, rs = _unpack_halves(zrev_ref[...])
    no_partner = (lax.broadcasted_iota(jnp.int32, zc.shape, 0) == 0) & (kk == 0)
    fold_c = (zc + jnp.where(no_partner, 0.0, rc)).astype(BF16)
    fold_s = (zs - jnp.where(no_partner, 0.0, rs)).astype(BF16)
    part = (jnp.dot(c_ref[...], fold_c, preferred_element_type=F32)
            - jnp.dot(s_ref[...], fold_s, preferred_element_type=F32))

    @pl.when(kk == 0)
    def _():
        acc_ref[b] = part

    @pl.when(kk > 0)
    def _():
        acc_ref[b] += part

    @pl.when(kk == pl.num_programs(1) - 1)
    def _():
        tm = acc_ref.shape[1]
        odd = lax.broadcasted_iota(jnp.int32, (tm, F_WIDTH), 0) % 2 == 1
        mid = zmid_ref[b, 0:1, :] * scale
        o_ref[b] = (acc_ref[b] + jnp.where(odd, -mid, mid)).astype(o_ref.dtype)


def _seq_partner_rows(z, n_batch, seq):
    half = seq // 2
    assert (n_batch * half) % SC_GATHER_WINDOW == 0
    n = jnp.arange(half, dtype=jnp.int32)[None, :]
    partner = (jnp.arange(n_batch, dtype=jnp.int32)[:, None] * seq + (seq - n) % seq).reshape(-1)
    return _sc_gather_rows(z, partner).reshape(n_batch, half, F_WIDTH)


def _seq_dft(cmat, smat, z, zrev, n_batch, seq):
    half = seq // 2
    tm = min(1024, seq)
    tk = min(1024, half)
    assert tm % 2 == 0
    z3 = z.reshape(n_batch, seq, F_WIDTH)
    zmid = lax.bitcast_convert_type(z3[:, half, :] << 16, F32)
    zmid = jnp.broadcast_to(zmid[:, None, :], (n_batch, 8, F_WIDTH))
    zblock = pl.BlockSpec((None, tk, F_WIDTH), lambda i, k, b: (b, k, 0))
    out = pl.pallas_call(
        functools.partial(_seqdft_kernel, scale=seq ** -0.5),
        out_shape=jax.ShapeDtypeStruct((n_batch, seq, F_WIDTH), BF16),
        grid=(seq // tm, half // tk, n_batch),
        in_specs=[pl.BlockSpec((tm, tk), lambda i, k, b: (i, k)),
                  pl.BlockSpec((tm, tk), lambda i, k, b: (i, k)),
                  zblock, zblock,
                  pl.BlockSpec((n_batch, 8, F_WIDTH), lambda i, k, b: (0, 0, 0))],
        out_specs=pl.BlockSpec((n_batch, tm, F_WIDTH), lambda i, k, b: (0, i, 0)),
        scratch_shapes=[pltpu.VMEM((n_batch, tm, F_WIDTH), F32)],
        compiler_params=_cparams(("parallel", "arbitrary", "arbitrary")),
        name="seq_dft",
    )(cmat, smat, z3, zrev, zmid)
    return out.reshape(n_batch * seq, F_WIDTH)


def _merge_kernel(r_ref, fo_ref, gates_ref, x_ref, wr_ref, wf_ref, wo_ref, g1_ref, b1_ref,
                  wr2_ref, rb_ref, x1_ref, x1p_ref, route_ref, cnt_ref, *, alpha, d_model):
    for j in range(x_ref.shape[0] // RANK_TILE):
        rows = pl.ds(j * RANK_TILE, RANK_TILE)
        _merge_subtile(rows, j, r_ref, fo_ref, gates_ref, x_ref, wr_ref, wf_ref, wo_ref,
                       g1_ref, b1_ref, wr2_ref, rb_ref, x1_ref, x1p_ref, route_ref, cnt_ref,
                       alpha=alpha, d_model=d_model)


def _merge_subtile(rows, j, r_ref, fo_ref, gates_ref, x_ref, wr_ref, wf_ref, wo_ref,
                   g1_ref, b1_ref, wr2_ref, rb_ref, x1_ref, x1p_ref, route_ref, cnt_ref, *, alpha, d_model):
    a = jnp.dot(r_ref[rows, :], wr_ref[...], preferred_element_type=F32)
    b = jnp.dot(fo_ref[rows, :], wf_ref[...], preferred_element_type=F32)
    merged = (gates_ref[rows, :d_model].astype(F32) * a + gates_ref[rows, d_model:].astype(F32) * b)
    mix = jnp.dot(merged.astype(BF16), wo_ref[...], preferred_element_type=F32)
    x1 = _layer_norm_rows(alpha * x_ref[rows, :] + mix, g1_ref[...], b1_ref[...])
    x1_ref[rows, :] = x1
    packed = _pack_halves(x1)
    for s in range(N_PIECES):
        x1p_ref[s, rows, :] = packed[:, s * PIECE_W:(s + 1) * PIECE_W]

    xh = x1.astype(BF16)
    xl = (x1 - xh.astype(F32)).astype(BF16)
    s2 = (jnp.dot(xh, wr2_ref[...], preferred_element_type=F32)
          + jnp.dot(xl, wr2_ref[...], preferred_element_type=F32))
    logits = s2 + pltpu.roll(s2, ROUTE_LANES // 2, axis=1) + rb_ref[...]
    tm = logits.shape[0]
    lane = lax.broadcasted_iota(jnp.int32, (tm, ROUTE_LANES), 1)
    neg = jnp.float32(-jnp.inf)
    big = jnp.int32(ROUTE_LANES)

    def top1(mask):
        vals = jnp.where(mask, logits, neg)
        m = jnp.max(vals, axis=-1, keepdims=True)
        idx = jnp.min(jnp.where(mask & (vals == m), lane, big), axis=-1, keepdims=True)
        return m, idx

    gmask = lane < N_GROUPS
    gm, gi = top1(gmask)
    gsum = jnp.sum(jnp.where(gmask, jnp.exp(logits - gm), 0.0), axis=-1, keepdims=True)
    g_gate = 1.0 / gsum
    lo = N_GROUPS + gi * EXPERTS_PER_GROUP
    emask = (lane >= lo) & (lane < lo + EXPERTS_PER_GROUP)
    m1, i1 = top1(emask)
    m2, i2 = top1(emask & (lane != i1))
    e2 = jnp.exp(m2 - m1)
    w1 = 1.0 / (1.0 + e2)
    w2 = e2 / (1.0 + e2)
    oh1 = (lane == i1 - N_GROUPS).astype(F32)
    oh2 = (lane == i2 - N_GROUPS).astype(F32)
    earlier = (lax.broadcasted_iota(jnp.int32, (tm, tm), 1)
               < lax.broadcasted_iota(jnp.int32, (tm, tm), 0)).astype(BF16)
    tot1 = jnp.sum(oh1, axis=0, keepdims=True)
    tot2 = jnp.sum(oh2, axis=0, keepdims=True)
    pre1 = jnp.dot(earlier, oh1.astype(BF16), preferred_element_type=F32)
    pre2 = jnp.dot(earlier, oh2.astype(BF16), preferred_element_type=F32) + tot1
    r1 = jnp.sum(pre1 * oh1, axis=-1, keepdims=True)
    r2 = jnp.sum(pre2 * oh2, axis=-1, keepdims=True)
    cnt_ref[j] = tot1 + tot2

    rec = jnp.where(lane == 0, (i1 - N_GROUPS).astype(F32), 0.0)
    rec = jnp.where(lane == 1, (i2 - N_GROUPS).astype(F32), rec)
    rec = jnp.where(lane == 2, g_gate * w1, rec)
    rec = jnp.where(lane == 3, g_gate * w2, rec)
    rec = jnp.where(lane == 4, r1, rec)
    rec = jnp.where(lane == 5, r2, rec)
    route_ref[rows, :] = rec


def _merge(r, fo, gates, x, wr, wf, wo, g1, b1, wr2, rbias, layer, alpha, tm):
    t, d = x.shape
    row = lambda w: pl.BlockSpec((tm, w), lambda i: (i, 0))
    full = lambda a: _layer_block(a, layer)
    return pl.pallas_call(
        functools.partial(_merge_kernel, alpha=alpha, d_model=d),
        out_shape=(jax.ShapeDtypeStruct((t, d), F32),
                   jax.ShapeDtypeStruct((N_PIECES, t, PIECE_W), jnp.uint32),
                   jax.ShapeDtypeStruct((t, ROUTE_LANES), F32),
                   jax.ShapeDtypeStruct((t // RANK_TILE, 1, ROUTE_LANES), F32)),
        grid=(t // tm,),
        in_specs=[row(r.shape[1]), row(fo.shape[1]), row(gates.shape[1]), row(d),
                  full(wr), full(wf), full(wo), full(g1), full(b1), full(wr2), full(rbias)],
        out_specs=(row(d), pl.BlockSpec((N_PIECES, tm, PIECE_W), lambda i: (0, i, 0)), row(ROUTE_LANES),
                   pl.BlockSpec((tm // RANK_TILE, 1, ROUTE_LANES), lambda i: (i, 0, 0))),
        compiler_params=_cparams(("parallel",)),
        name="merge_router",
    )(r, fo, gates, x, wr, wf, wo, g1, b1, wr2, rbias)


def _sc_mesh():
    return plsc.VectorSubcoreMesh(core_axis_name="c", subcore_axis_name="s")


def _sc_gather_rows(table, idx):
    n, w = idx.shape[0], table.shape[1]
    idx = idx.reshape(1, n)

    @functools.partial(pl.kernel, out_type=jax.ShapeDtypeStruct((n, w), table.dtype), mesh=_sc_mesh(),
                       scratch_types=[])
    def gather(tab_hbm, idx_hbm, out_hbm):
        def body(idx_vmem, out_vmem):
            pltpu.sync_copy(tab_hbm.at[idx_vmem.at[0]], out_vmem)

        pltpu.emit_pipeline(
            body,
            grid=(n // SC_GATHER_WINDOW,),
            in_specs=[pl.BlockSpec((1, SC_GATHER_WINDOW), lambda i: (0, i))],
            out_specs=[pl.BlockSpec((SC_GATHER_WINDOW, w), lambda i: (i, 0))],
            core_axis_name=("c", "s"),
            dimension_semantics=(pltpu.PARALLEL,),
        )(idx_hbm, out_hbm)

    return gather(table, idx)


def _sc_scatter_rows(src, idx, n_out):
    assert idx.shape[0] == n_out
    n, w = n_out, src.shape[1]
    idx = idx.reshape(1, n)
    src_windows = src.shape[0] // SC_GATHER_WINDOW

    @functools.partial(pl.kernel, out_type=jax.ShapeDtypeStruct((n, w), src.dtype), mesh=_sc_mesh(),
                       scratch_types=[])
    def scatter(src_hbm, idx_hbm, out_hbm):
        def body(src_vmem, idx_vmem):
            pltpu.sync_copy(src_vmem, out_hbm.at[idx_vmem.at[0]])

        pltpu.emit_pipeline(
            body,
            grid=(n // SC_GATHER_WINDOW,),
            in_specs=[pl.BlockSpec((SC_GATHER_WINDOW, w), lambda i: (i % src_windows, 0)),
                      pl.BlockSpec((1, SC_GATHER_WINDOW), lambda i: (0, i))],
            out_specs=[],
            core_axis_name=("c", "s"),
            dimension_semantics=(pltpu.PARALLEL,),
        )(src_hbm, idx_hbm)

    return scatter(src, idx)


def _expert_kernel(te_ref, nv_ref, run_ref, xs_ref, wg_hbm, wu_hbm, wd_hbm, o_ref,
                   wg_f, wu_f, wd_f, wg_s, wu_s, wd_s, sems, *, slab0):
    i = pl.program_id(0)

    def weight_copies(expert, slot):
        return [pltpu.make_async_copy(hbm.at[slab0 + expert], buf.at[slot], sems.at[slot, k])
                for k, (hbm, buf) in enumerate(((wg_hbm, wg_f), (wu_hbm, wu_f), (wd_hbm, wd_f)))]

    @pl.when(i == 0)
    def _():
        for c in weight_copies(te_ref[0], 0):
            c.start()

    @pl.when(run_ref[0, i] == 1)
    def _():
        slot = run_ref[1, i]
        for c in weight_copies(te_ref[i], slot):
            c.wait()

        @pl.when(run_ref[2, i] >= 0)
        def _():
            for c in weight_copies(run_ref[2, i], 1 - slot):
                c.start(priority=1)

        wg_s[...] = wg_f[slot].astype(BF16)
        wu_s[...] = wu_f[slot].astype(BF16)
        wd_s[...] = wd_f[slot].astype(BF16)

    @pl.when(i < nv_ref[0])
    def _():
        chunks = [c.astype(BF16) for c in _unpack_pieces([xs_ref[s] for s in range(N_PIECES)])]
        cw = chunks[0].shape[1]

        def up(w_s):
            acc = jnp.dot(chunks[0], w_s[:cw, :], preferred_element_type=F32)
            for j in range(1, len(chunks)):
                acc = acc + jnp.dot(chunks[j], w_s[j * cw:(j + 1) * cw, :], preferred_element_type=F32)
            return acc

        g = up(wg_s)
        hmid = (g * jax.nn.sigmoid(g) * up(wu_s)).astype(BF16)
        _store_pieces(o_ref, _pack_halves(jnp.dot(hmid, wd_s[...], preferred_element_type=F32)))

    @pl.when(i >= nv_ref[0])
    def _():
        o_ref[...] = jnp.zeros_like(o_ref)


def _expert_mlp(tile_expert, n_valid, xs, wg, wu, wd, layer, tm):
    p = xs.shape[1]
    n_tiles = p // tm
    d, de = wg.shape[1], wg.shape[2]
    tiles = jnp.arange(n_tiles, dtype=jnp.int32)
    prev = jnp.concatenate([tile_expert[:1], tile_expert[:-1]])
    is_start = (tiles == 0) | (tile_expert != prev)
    slot = (jnp.cumsum(is_start.astype(jnp.int32)) - 1) % 2
    start_at = jnp.where(is_start, tiles, n_tiles)
    next_start = jnp.concatenate([lax.cummin(start_at[::-1])[::-1][1:], jnp.full((1,), n_tiles, jnp.int32)])
    next_expert = jnp.where(next_start < n_tiles, tile_expert[jnp.minimum(next_start, n_tiles - 1)], -1)
    runs = jnp.stack([is_start.astype(jnp.int32), slot, next_expert]).astype(jnp.int32)

    xmap = lambda i, te, nv, run: (0, jnp.minimum(i, nv[0] - 1), 0)
    return pl.pallas_call(
        functools.partial(_expert_kernel, slab0=layer * N_EXPERTS),
        out_shape=jax.ShapeDtypeStruct(xs.shape, jnp.uint32),
        grid_spec=pltpu.PrefetchScalarGridSpec(
            num_scalar_prefetch=3,
            grid=(n_tiles,),
            in_specs=[pl.BlockSpec((N_PIECES, tm, PIECE_W), xmap),
                      pl.BlockSpec(memory_space=pl.ANY),
                      pl.BlockSpec(memory_space=pl.ANY),
                      pl.BlockSpec(memory_space=pl.ANY)],
            out_specs=pl.BlockSpec((N_PIECES, tm, PIECE_W), lambda i, te, nv, run: (0, i, 0)),
            scratch_shapes=[pltpu.VMEM((2, d, de), F32), pltpu.VMEM((2, d, de), F32), pltpu.VMEM((2, de, d), F32),
                            pltpu.VMEM((d, de), BF16), pltpu.VMEM((d, de), BF16), pltpu.VMEM((de, d), BF16),
                            pltpu.SemaphoreType.DMA((2, 3))]),
        compiler_params=_cparams(("arbitrary",)),
        name="expert_mlp",
    )(tile_expert, n_valid, runs, xs, wg, wu, wd)


def _combine_kernel(y_ref, route_ref, x_ref, g_ref, b_ref, o_ref, *, alpha):
    o_ref[...] = _combine_rows(y_ref, route_ref, x_ref, g_ref, b_ref, alpha)


def _combine(yg, route, x1, g2, b2, layer, alpha, tm):
    n_rows, d = x1.shape
    row = lambda w: pl.BlockSpec((tm, w), lambda i: (i, 0))
    return pl.pallas_call(
        functools.partial(_combine_kernel, alpha=alpha),
        out_shape=jax.ShapeDtypeStruct((n_rows, d), F32),
        grid=(n_rows // tm,),
        in_specs=[pl.BlockSpec((N_PIECES, 2, tm, PIECE_W), lambda i: (0, 0, i, 0)),
                  row(ROUTE_LANES), row(d), _layer_block(g2, layer), _layer_block(b2, layer)],
        out_specs=pl.BlockSpec((tm, d), lambda i: (i, 0)),
        compiler_params=_cparams(("parallel",)),
        name="combine_ln2",
    )(yg, route, x1, g2, b2)


def _rope_tables(seq):
    half = QK_DIM // 2
    inv_freq = 1.0 / (ROPE_BASE ** (jnp.arange(half, dtype=F32) / half))
    ang = jnp.arange(seq, dtype=F32)[:, None] * inv_freq[None, :]
    cos, sin = jnp.cos(ang), jnp.sin(ang)
    return jnp.concatenate([cos, cos], axis=1), jnp.concatenate([-sin, sin], axis=1)


def _channel_dft_matrix():
    n = np.arange(FG_DIM)
    ang = 2.0 * np.pi * np.outer(n, n) / FG_DIM
    eye = np.eye(N_FGROUPS)
    bd = np.concatenate([np.kron(eye, np.cos(ang)), np.kron(eye, np.sin(ang))], axis=1) / math.sqrt(FG_DIM)
    return jnp.asarray(bd, dtype=BF16)


def _dft_matrix_kernel(ch_ref, sh_ref, cl_ref, sl_ref, c_ref, s_ref):
    cl = cl_ref[...]
    sl = sl_ref[...]
    for blk in range(ch_ref.shape[1]):
        ch = ch_ref[:, blk:blk + 1]
        sh = sh_ref[:, blk:blk + 1]
        cols = slice(blk * LANE, (blk + 1) * LANE)
        c_ref[:, cols] = (ch * cl - sh * sl).astype(BF16)
        s_ref[:, cols] = (sh * cl + ch * sl).astype(BF16)


def _seq_dft_matrices(seq):
    half = seq // 2
    assert half % LANE == 0
    k = jnp.arange(seq, dtype=jnp.int32)[:, None]
    hi = jnp.arange(half // LANE, dtype=jnp.int32)[None, :] * LANE
    lo = jnp.arange(LANE, dtype=jnp.int32)[None, :]
    w = 2.0 * math.pi / seq
    a_hi = ((k * hi) % seq).astype(F32) * w
    a_lo = ((k * lo) % seq).astype(F32) * w
    scale = seq ** -0.5
    tm = min(256, seq)
    nb = half // LANE
    small = lambda width: pl.BlockSpec((tm, width), lambda i: (i, 0))
    return pl.pallas_call(
        _dft_matrix_kernel,
        out_shape=(jax.ShapeDtypeStruct((seq, half), BF16), jax.ShapeDtypeStruct((seq, half), BF16)),
        grid=(seq // tm,),
        in_specs=[small(nb), small(nb), small(LANE), small(LANE)],
        out_specs=(small(half), small(half)),
        compiler_params=_cparams(("parallel",)),
        name="dft_matrices",
    )(jnp.cos(a_hi) * scale, jnp.sin(a_hi) * scale, jnp.cos(a_lo), jnp.sin(a_lo))


def _pad_heads(w, axis):
    shp = list(w.shape)
    shp[axis:axis + 1] = [N_HEADS, V_DIM]
    w = w.reshape(shp)
    pad = [(0, 0)] * w.ndim
    pad[axis + 1] = (0, V_PAD - V_DIM)
    w = jnp.pad(w, pad)
    shp[axis:axis + 2] = [N_HEADS * V_PAD]
    return w.reshape(shp)


def _pad_in_weight(w_in):
    qk_w = 2 * N_HEADS * QK_DIM
    vw = N_HEADS * V_DIM
    qk = w_in[..., :qk_w]
    v = _pad_heads(w_in[..., qk_w:qk_w + vw], 2)
    g = _pad_heads(w_in[..., qk_w + vw:qk_w + 2 * vw], 2)
    rest = w_in[..., qk_w + 2 * vw:]
    return jnp.concatenate([qk, v, g, rest], axis=2).astype(BF16)


def _routing_plan(route, tile_counts, tm_tok, n_tiles, moe_tile):
    t = route.shape[0]
    experts = jnp.arange(N_EXPERTS, dtype=jnp.int32)
    cnt = tile_counts[:, 0, :N_EXPERTS].astype(jnp.int32)
    before = jnp.cumsum(cnt, axis=0) - cnt
    counts = jnp.sum(cnt, axis=0)
    tiles_per = (counts + moe_tile - 1) // moe_tile
    tile_end = jnp.cumsum(tiles_per)
    start_row = (tile_end - tiles_per) * moe_tile
    n_valid = tile_end[-1]

    base = (start_row[None, :] + before).astype(F32)
    base_tok = jnp.broadcast_to(base[:, None, :], (t // tm_tok, tm_tok, N_EXPERTS)).reshape(t, N_EXPERTS)
    e = route[:, 0:2]
    rank = route[:, 4:6]
    sel = e[:, :, None] == experts.astype(F32)[None, None, :]
    pos = (jnp.sum(jnp.where(sel, base_tok[:, None, :], 0.0), axis=-1) + rank).astype(jnp.int32)

    n_pad = n_tiles * moe_tile - 2 * t
    pads = tiles_per * moe_tile - counts
    pad_end = jnp.cumsum(pads)
    p = jnp.arange(n_pad, dtype=jnp.int32)
    pe = jnp.minimum(jnp.sum((p[:, None] >= pad_end[None, :]).astype(jnp.int32), axis=1), N_EXPERTS - 1)
    in_expert = start_row[pe] + counts[pe] + (p - (pad_end[pe] - pads[pe]))
    pad_pos = jnp.where(p < pad_end[-1], in_expert, n_valid * moe_tile + (p - pad_end[-1]))

    tile_ids = jnp.arange(n_tiles, dtype=jnp.int32)
    tile_expert = jnp.minimum(jnp.sum((tile_ids[:, None] >= tile_end[None, :]).astype(jnp.int32), axis=1),
                              N_EXPERTS - 1)
    last_e = tile_expert[jnp.maximum(n_valid - 1, 0)]
    tile_expert = jnp.where(tile_ids < n_valid, tile_expert, last_e).astype(jnp.int32)
    return pos, pad_pos.astype(jnp.int32), tile_expert, n_valid.reshape(1).astype(jnp.int32)


def _largest_tile(cands, *dims):
    for c in cands:
        if all(d % c == 0 for d in dims):
            return c
    raise ValueError(f"no tile in {cands} divides {dims}")


def kernel(x_prompt, x_sample, ln_in_g, ln_in_b, w_in, ret_decay_fwd, ret_decay_bwd, ret_gn_g, ret_gn_b, w_ret_up, w_four_up, w_out, ln1_g, ln1_b, w_route_group, b_route_group, w_route_expert, b_route_expert, w_expert_gate, w_expert_up, w_expert_down, ln2_g, ln2_b):
    bp, sp, d = x_prompt.shape
    bs, ss, _ = x_sample.shape
    depth = w_in.shape[0]
    alpha = float((2 * depth) ** 0.25)

    cos_t, sin_t = _rope_tables(max(sp, ss))
    bd = _channel_dft_matrix()
    dft = {s: _seq_dft_matrices(s) for s in sorted({sp, ss})}
    wg_all = w_expert_gate.reshape((depth * N_EXPERTS,) + w_expert_gate.shape[2:])
    wu_all = w_expert_up.reshape((depth * N_EXPERTS,) + w_expert_up.shape[2:])
    wd_all = w_expert_down.reshape((depth * N_EXPERTS,) + w_expert_down.shape[2:])

    batches = []
    for xin in (x_prompt, x_sample):
        nb, seq, _ = xin.shape
        t = nb * seq
        assert seq % CHUNK == 0 and t % SC_GATHER_WINDOW == 0
        moe_tile = MOE_TILE if 2 * t >= 4 * MOE_TILE * N_EXPERTS else MOE_TILE // 2
        assert (2 * t) % moe_tile == 0
        batches.append(dict(nb=nb, seq=seq, t=t, moe_tile=moe_tile, n_tiles=(2 * t) // moe_tile + N_EXPERTS,
                            tm_in=_largest_tile((512, 256, 128), seq),
                            tm_tok=_largest_tile((TOKEN_TILE, 128), t),
                            tm_merge=_largest_tile((MERGE_TILE, RANK_TILE), t),
                            prologue=(xin.reshape(t, d),)))

    vec3 = lambda a: a.reshape(a.shape[0], 1, a.shape[-1]).astype(F32)
    w_pad = _pad_in_weight(w_in)
    dec = jnp.stack([ret_decay_fwd, ret_decay_bwd], axis=1).astype(F32)
    gn_g = vec3(_pad_heads(ret_gn_g, 1))
    gn_b = vec3(_pad_heads(ret_gn_b, 1))
    wr = _pad_heads(w_ret_up, 1).astype(BF16)
    wf = w_four_up.astype(BF16)
    wo = w_out.astype(BF16)
    w_rt = jnp.concatenate([w_route_group, w_route_expert], axis=2)
    w_rt = jnp.pad(w_rt, ((0, 0), (0, 0), (0, ROUTE_LANES // 2 - w_rt.shape[2])))
    wrh = w_rt.astype(BF16)
    wr2 = jnp.concatenate([wrh, (w_rt - wrh.astype(F32)).astype(BF16)], axis=2)
    rbias = jnp.concatenate([b_route_group, b_route_expert], axis=1)
    rbias = vec3(jnp.pad(rbias, ((0, 0), (0, ROUTE_LANES - rbias.shape[1]))))
    ln_in = (vec3(ln_in_g[None]), vec3(ln_in_b[None]))
    ln1 = (vec3(ln1_g), vec3(ln1_b))
    ln2 = (vec3(ln2_g), vec3(ln2_b))

    for l in range(depth):
        pro_g, pro_b, pro_layer = (*ln_in, 0) if l == 0 else (*ln2, l - 1)

        for bt in batches:
            nb, seq, t, moe_tile, n_tiles = bt["nb"], bt["seq"], bt["t"], bt["moe_tile"], bt["n_tiles"]
            x, qk, v, sg, z, gates = _inproj(bt["prologue"], pro_g, pro_b, pro_layer, w_pad, l, cos_t, sin_t, bd,
                                             bt["tm_in"], seq, alpha)
            zrev = _seq_partner_rows(z, nb, seq)
            r = _retention(dec, qk, v, sg, gn_g, gn_b, l, nb, seq)
            fo = _seq_dft(dft[seq][0], dft[seq][1], z, zrev, nb, seq)
            x1, x1p, route, tile_counts = _merge(r, fo, gates, x, wr, wf, wo, *ln1, wr2, rbias, l,
                                                 alpha, bt["tm_merge"])

            pos, pad_pos, tile_expert, n_valid = _routing_plan(route, tile_counts, RANK_TILE, n_tiles, moe_tile)
            n_sorted = n_tiles * moe_tile
            piece_off = jnp.arange(N_PIECES, dtype=jnp.int32)[:, None] * n_sorted
            dst = jnp.concatenate([(piece_off + seg[None, :]).reshape(-1)
                                   for seg in (pos[:, 0], pos[:, 1], pad_pos)])
            xs = _sc_scatter_rows(x1p.reshape(N_PIECES * t, PIECE_W), dst, N_PIECES * n_sorted)
            ys = _expert_mlp(tile_expert, n_valid, xs.reshape(N_PIECES, n_sorted, PIECE_W),
                             wg_all, wu_all, wd_all, l, moe_tile)
            src = (piece_off[:, :, None] + pos.T[None, :, :]).reshape(-1)
            yg = _sc_gather_rows(ys.reshape(N_PIECES * n_sorted, PIECE_W), src).reshape(N_PIECES, 2, t, PIECE_W)
            bt["prologue"] = (yg, route, x1)

    outs = [_combine(*bt["prologue"], *ln2, depth - 1, alpha, bt["tm_tok"]) for bt in batches]
    return outs[0].reshape(bp, sp, d), outs[1].reshape(bs, ss, d)
```
